```python
import functools
import jax
import jax.numpy as jnp
from jax import lax
import numpy as np

D_MODEL = 2048
BATCH = 1
SEQ = 8192
DEPTH = 1
DEC_BATCH = 128
DEC_SEQ = 1
PAST_LEN = 2048
PAGE_SIZE = 128

N_HEADS = 16
HEAD_DIM = 128
N_KV_HEADS = 4
Q_PER_KV = N_HEADS // N_KV_HEADS
IDX_HEADS = 16
IDX_DIM = 64
IDX_SCALE = (IDX_HEADS * IDX_DIM) ** -0.5
TOPK_MAX = 256
Q_BLOCK = 128
ROPE_THETA = 10000.0
D_RNN = D_MODEL
RNN_BLOCKS = 16
RNN_BW = D_RNN // RNN_BLOCKS
CONV_W = 4
RG_C = 8.0
N_EXPERTS = 32
TOP_K = 4
D_FF = D_MODEL
SWIGLU_LIMIT = 7.0
SWIGLU_ALPHA = 1.702
MOE_BLOCK = 128
PLE_DIM = 256
LN_EPS = 1e-5
DEEPNORM_ALPHA = (2 * DEPTH) ** 0.25
DEEPNORM_BETA = (8 * DEPTH) ** -0.25
SPLIT_WIDTHS = (N_HEADS * HEAD_DIM, N_KV_HEADS * HEAD_DIM, N_KV_HEADS * HEAD_DIM,
                IDX_HEADS * IDX_DIM, IDX_DIM, IDX_HEADS, D_RNN, D_RNN, D_MODEL, D_MODEL)
D_IN = (N_HEADS * HEAD_DIM + 2 * N_KV_HEADS * HEAD_DIM + IDX_HEADS * IDX_DIM + IDX_DIM + IDX_HEADS
        + 2 * D_RNN + 2 * D_MODEL)

kernel_name = 'hawk_dsa_moe_hybrid_step'


def layer_norm(x, g, b):
    xf = x.astype(jnp.float32)
    mu = xf.mean(-1, keepdims=True)
    var = jnp.square(xf - mu).mean(-1, keepdims=True)
    y = (xf - mu) * lax.rsqrt(var + LN_EPS) * g.astype(jnp.float32) + b.astype(jnp.float32)
    return y.astype(x.dtype)


def rope(x, pos):
    half = x.shape[-1] // 2
    inv_freq = ROPE_THETA ** (-jnp.arange(half, dtype=jnp.float32) / half)
    ang = pos.astype(jnp.float32)[:, None] * inv_freq
    cos = jnp.cos(ang)[:, None, :]
    sin = jnp.sin(ang)[:, None, :]
    xf = x.astype(jnp.float32)
    x1, x2 = xf[..., :half], xf[..., half:]
    return jnp.concatenate([x1 * cos - x2 * sin, x2 * cos + x1 * sin], axis=-1).astype(x.dtype)


def split_projection(z):
    parts, start = [], 0
    for w in SPLIT_WIDTHS:
        parts.append(z[..., start:start + w])
        start += w
    return parts


def causal_conv(x, buf, w, b):
    T = x.shape[1]
    xp = jnp.concatenate([buf, x], axis=1)
    y = b + sum(xp[:, j:j + T] * w[j] for j in range(CONV_W))
    return y, xp[:, T:]


def rg_lru(x, pos, h0, w_a, b_a, w_x, b_x, lam):
    B, T, _ = x.shape
    xb = x.reshape(B, T, RNN_BLOCKS, RNN_BW)
    r = jax.nn.sigmoid(jnp.einsum('btnc,ncd->btnd', xb, w_a).reshape(B, T, D_RNN) + b_a)
    i = jax.nn.sigmoid(jnp.einsum('btnc,ncd->btnd', xb, w_x).reshape(B, T, D_RNN) + b_x)
    log_a = -RG_C * r.astype(jnp.float32) * jax.nn.softplus(-lam.astype(jnp.float32))
    a = jnp.exp(log_a)
    mult = jnp.sqrt(-jnp.expm1(2.0 * log_a))
    mult = jnp.where((pos == 0)[None, :, None], 1.0, mult)
    bterm = mult * (i * x).astype(jnp.float32)
    bterm = bterm.at[:, 0].add(a[:, 0] * h0.astype(jnp.float32))

    def combine(lhs, rhs):
        return lhs[0] * rhs[0], rhs[0] * lhs[1] + rhs[1]

    _, h = lax.associative_scan(combine, (a, bterm), axis=1)
    return h.astype(x.dtype), h[:, -1].astype(x.dtype)


def indexer_scores(q_idx, w_idx, k_idx):
    s = jax.nn.relu(jnp.einsum('bqhd,bsd->bqhs', q_idx, k_idx).astype(jnp.float32))
    return jnp.einsum('bqhs,bqh->bqs', s, w_idx.astype(jnp.float32))


def sparse_attend(q, k_sel, v_sel, valid):
    B, Q = q.shape[:2]
    qg = q.reshape(B, Q, N_KV_HEADS, Q_PER_KV, HEAD_DIM)
    s = jnp.einsum('bqvgd,bqnvd->bqvgn', qg, k_sel).astype(jnp.float32) * (HEAD_DIM ** -0.5)
    s = jnp.where(valid[:, :, None, None, :], s, -jnp.inf)
    p = jax.nn.softmax(s, axis=-1).astype(v_sel.dtype)
    o = jnp.einsum('bqvgn,bqnvd->bqvgd', p, v_sel)
    return o.reshape(B, Q, N_HEADS * HEAD_DIM)


def gather_rows(rows, idx):
    return jax.vmap(lambda r, ii: r[ii])(rows, idx)


def prompt_sparse_attention(q, k, v, q_idx, w_idx, k_idx):
    B, S = q.shape[:2]
    n_sel = min(TOPK_MAX, S // 4)
    nb = S // Q_BLOCK
    key_pos = jnp.arange(S)

    def to_blocks(a):
        return jnp.swapaxes(a.reshape((B, nb, Q_BLOCK) + a.shape[2:]), 0, 1)

    def one_block(args):
        qb, qib, wb, t0 = args
        tq = t0 + jnp.arange(Q_BLOCK)
        sc = indexer_scores(qib, wb, k_idx)
        sc = jnp.where(key_pos[None, None, :] <= tq[None, :, None], sc, -jnp.inf)
        _, sel = lax.top_k(sc, n_sel)
        valid = sel <= tq[None, :, None]
        flat = sel.reshape(B, Q_BLOCK * n_sel)
        ks = gather_rows(k, flat).reshape(B, Q_BLOCK, n_sel, N_KV_HEADS, HEAD_DIM)
        vs = gather_rows(v, flat).reshape(B, Q_BLOCK, n_sel, N_KV_HEADS, HEAD_DIM)
        return sparse_attend(qb, ks, vs, valid)

    t0s = jnp.arange(nb, dtype=jnp.int32) * Q_BLOCK
    out = lax.map(one_block, (to_blocks(q), to_blocks(q_idx), to_blocks(w_idx), t0s))
    return jnp.swapaxes(out, 0, 1).reshape(B, S, N_HEADS * HEAD_DIM)


def sample_sparse_attention(q, k, v, q_idx, w_idx, k_idx, cache_k, cache_v, cache_kidx, page_table):
    DB, T = q.shape[:2]
    past = page_table.shape[1] * PAGE_SIZE
    L = past + T
    n_sel = min(TOPK_MAX, L // 4)
    kidx_past = cache_kidx[page_table].reshape(DB, past, IDX_DIM)
    kidx_all = jnp.concatenate([kidx_past, k_idx], axis=1)
    tq = past + jnp.arange(T)
    sc = indexer_scores(q_idx, w_idx, kidx_all)
    sc = jnp.where(jnp.arange(L)[None, None, :] <= tq[None, :, None], sc, -jnp.inf)
    _, sel = lax.top_k(sc, n_sel)
    valid = sel <= tq[None, :, None]
    in_past = sel < past
    p_idx = jnp.minimum(sel, past - 1)
    phys = gather_rows(page_table, (p_idx // PAGE_SIZE).reshape(DB, T * n_sel)).reshape(DB, T, n_sel)
    row = p_idx % PAGE_SIZE
    n_idx = jnp.clip(sel - past, 0, T - 1).reshape(DB, T * n_sel)
    k_new = gather_rows(k, n_idx).reshape(DB, T, n_sel, N_KV_HEADS, HEAD_DIM)
    v_new = gather_rows(v, n_idx).reshape(DB, T, n_sel, N_KV_HEADS, HEAD_DIM)
    ks = jnp.where(in_past[..., None, None], cache_k[phys, row], k_new)
    vs = jnp.where(in_past[..., None, None], cache_v[phys, row], v_new)
    return sparse_attend(q, ks, vs, valid)


def moe(x, lw):
    shp = x.shape
    xt = x.reshape(-1, D_MODEL)
    T = xt.shape[0]
    logits = (xt @ lw['w_router']).astype(jnp.float32) + lw['b_router'].astype(jnp.float32)
    top_val, top_idx = lax.top_k(logits, TOP_K)
    gates = jax.nn.softmax(top_val, axis=-1)
    n_assign = T * TOP_K
    e = top_idx.reshape(n_assign)
    tok = jnp.repeat(jnp.arange(T, dtype=jnp.int32), TOP_K)
    g = gates.reshape(n_assign)
    order = jnp.argsort(e)
    e_s, tok_s, g_s = e[order], tok[order], g[order]
    counts = jnp.zeros((N_EXPERTS,), jnp.int32).at[e].add(1)
    padded = (counts + MOE_BLOCK - 1) // MOE_BLOCK * MOE_BLOCK
    start = jnp.cumsum(counts) - counts
    ends_p = jnp.cumsum(padded)
    pstart = ends_p - padded
    dest = pstart[e_s] + jnp.arange(n_assign, dtype=jnp.int32) - start[e_s]
    n_blocks = -(-n_assign // MOE_BLOCK) + N_EXPERTS
    n_slots = n_blocks * MOE_BLOCK
    slot_tok = jnp.full((n_slots,), T, jnp.int32).at[dest].set(tok_s)
    slot_gate = jnp.zeros((n_slots,), jnp.float32).at[dest].set(g_s)
    block_expert = jnp.minimum(
        jnp.searchsorted(ends_p, jnp.arange(n_blocks, dtype=jnp.int32) * MOE_BLOCK, side='right'),
        N_EXPERTS - 1).astype(jnp.int32)
    x_pad = jnp.concatenate([xt, jnp.zeros((1, D_MODEL), xt.dtype)], axis=0)
    xs = x_pad[slot_tok].reshape(n_blocks, MOE_BLOCK, D_MODEL)
    w_up, b_up, w_down, b_down = lw['w_up'], lw['b_up'], lw['w_down'], lw['b_down']

    def expert_block(args):
        xb, ei = args
        h = xb @ w_up[ei] + b_up[ei]
        glu, lin = h[:, :D_FF], h[:, D_FF:]
        glu = jnp.minimum(glu, SWIGLU_LIMIT)
        lin = jnp.clip(lin, -SWIGLU_LIMIT, SWIGLU_LIMIT)
        act = glu * jax.nn.sigmoid(SWIGLU_ALPHA * glu) * (lin + 1.0)
        return act @ w_down[ei] + b_down[ei]

    ys = lax.map(expert_block, (xs, block_expert)).reshape(n_slots, D_MODEL)
    out = jnp.zeros((T + 1, D_MODEL), jnp.float32).at[slot_tok].add(ys.astype(jnp.float32) * slot_gate[:, None])
    return out[:T].astype(x.dtype).reshape(shp)


def trunk_layer(x, p, pos, conv_buf, h0, attend, lw):
    B, T, _ = x.shape
    z = x @ lw['w_in']
    q, k, v, qi, ki, wi, xr, gr, ga, gb = split_projection(z)
    q = rope(q.reshape(B, T, N_HEADS, HEAD_DIM), pos)
    k = rope(k.reshape(B, T, N_KV_HEADS, HEAD_DIM), pos)
    v = v.reshape(B, T, N_KV_HEADS, HEAD_DIM)
    qi = rope(qi.reshape(B, T, IDX_HEADS, IDX_DIM), pos)
    ki = rope(ki[:, :, None, :], pos)[:, :, 0, :]
    wi = wi * IDX_SCALE
    xc, new_buf = causal_conv(xr, conv_buf, lw['conv_w'], lw['conv_b'])
    h, h_last = rg_lru(xc, pos, h0, lw['rg_wa'], lw['rg_ba'], lw['rg_wx'], lw['rg_bx'], lw['rg_lambda'])
    y_rnn = jax.nn.gelu(gr) * h
    y_att = attend(q, k, v, qi, wi, ki)
    m = jax.nn.sigmoid(ga) * (y_rnn @ lw['w_o_rnn']) + jax.nn.sigmoid(gb) * (y_att @ lw['w_o_att'])
    x1 = layer_norm(DEEPNORM_ALPHA * x + m @ lw['w_out'], lw['ln1_g'], lw['ln1_b'])
    x2 = layer_norm(DEEPNORM_ALPHA * x1 + moe(x1, lw), lw['ln2_g'], lw['ln2_b'])
    y = x2 + jax.nn.sigmoid(x2 @ lw['w_ple_gate']) * (p @ lw['w_ple_proj'])
    return y, (k, v, ki, h_last, new_buf)


def setup_inputs(seed: int = 0) -> dict:
    key = jax.random.key(seed)
    ks = jax.random.split(key, 40)
    counter = iter(range(40))

    def nrm(shape, scale):
        return jax.random.normal(ks[next(counter)], shape, jnp.float32) * scale

    n_pages = PAST_LEN // PAGE_SIZE
    n_used = DEC_BATCH * n_pages
    n_phys = (n_used * 5) // 4
    perm = jax.random.permutation(ks[next(counter)], n_phys)
    page_table = perm[:n_used].reshape(DEC_BATCH, n_pages).astype(jnp.int32)
    u = jax.random.uniform(ks[next(counter)], (DEPTH, D_RNN), jnp.float32, 0.9, 0.999)
    s = u ** (1.0 / RG_C)
    rg_lambda = jnp.log(s) - jnp.log1p(-s)
    return {
        'x_prompt': nrm((BATCH, SEQ, D_MODEL), 1.0),
        'x_sample': nrm((DEC_BATCH, DEC_SEQ, D_MODEL), 1.0),
        'cache_k': nrm((DEPTH, n_phys, PAGE_SIZE, N_KV_HEADS, HEAD_DIM), 1.0),
        'cache_v': nrm((DEPTH, n_phys, PAGE_SIZE, N_KV_HEADS, HEAD_DIM), 1.0),
        'cache_kidx': nrm((DEPTH, n_phys, PAGE_SIZE, IDX_DIM), 1.0),
        'state_h': nrm((DEPTH, DEC_BATCH, D_RNN), 0.5),
        'state_conv': nrm((DEPTH, DEC_BATCH, CONV_W - 1, D_RNN), 1.0),
        'page_table': page_table,
        'p_prompt': nrm((DEPTH, BATCH, SEQ, PLE_DIM), 1.0),
        'p_sample': nrm((DEPTH, DEC_BATCH, DEC_SEQ, PLE_DIM), 1.0),
        'w_in': nrm((DEPTH, D_MODEL, D_IN), D_MODEL ** -0.5),
        'conv_w': nrm((DEPTH, CONV_W, D_RNN), CONV_W ** -0.5),
        'conv_b': nrm((DEPTH, D_RNN), 0.02),
        'rg_wa': nrm((DEPTH, RNN_BLOCKS, RNN_BW, RNN_BW), RNN_BW ** -0.5),
        'rg_ba': nrm((DEPTH, D_RNN), 0.02),
        'rg_wx': nrm((DEPTH, RNN_BLOCKS, RNN_BW, RNN_BW), RNN_BW ** -0.5),
        'rg_bx': nrm((DEPTH, D_RNN), 0.02),
        'rg_lambda': rg_lambda,
        'w_o_rnn': nrm((DEPTH, D_RNN, D_MODEL), D_RNN ** -0.5),
        'w_o_att': nrm((DEPTH, N_HEADS * HEAD_DIM, D_MODEL), (N_HEADS * HEAD_DIM) ** -0.5),
        'w_out': nrm((DEPTH, D_MODEL, D_MODEL), DEEPNORM_BETA * D_MODEL ** -0.5),
        'ln1_g': 1.0 + nrm((DEPTH, D_MODEL), 0.02),
        'ln1_b': nrm((DEPTH, D_MODEL), 0.02),
        'w_router': nrm((DEPTH, D_MODEL, N_EXPERTS), D_MODEL ** -0.5),
        'b_router': nrm((DEPTH, N_EXPERTS), 0.01),
        'w_up': nrm((DEPTH, N_EXPERTS, D_MODEL, 2 * D_FF), D_MODEL ** -0.5),
        'b_up': nrm((DEPTH, N_EXPERTS, 2 * D_FF), 0.02),
        'w_down': nrm((DEPTH, N_EXPERTS, D_FF, D_MODEL), DEEPNORM_BETA * D_FF ** -0.5),
        'b_down': nrm((DEPTH, N_EXPERTS, D_MODEL), 0.02),
        'ln2_g': 1.0 + nrm((DEPTH, D_MODEL), 0.02),
        'ln2_b': nrm((DEPTH, D_MODEL), 0.02),
        'w_ple_gate': nrm((DEPTH, D_MODEL, D_MODEL), D_MODEL ** -0.5),
        'w_ple_proj': nrm((DEPTH, PLE_DIM, D_MODEL), PLE_DIM ** -0.5),
    }


def reference(x_prompt, x_sample, cache_k, cache_v, cache_kidx, state_h, state_conv, page_table,
              p_prompt, p_sample, w_in, conv_w, conv_b, rg_wa, rg_ba, rg_wx, rg_bx, rg_lambda,
              w_o_rnn, w_o_att, w_out, ln1_g, ln1_b, w_router, b_router, w_up, b_up, w_down, b_down,
              ln2_g, ln2_b, w_ple_gate, w_ple_proj):
    B, S, _ = x_prompt.shape
    DB, T, _ = x_sample.shape
    past = page_table.shape[1] * PAGE_SIZE
    pos_p = jnp.arange(S, dtype=jnp.int32)
    pos_s = past + jnp.arange(T, dtype=jnp.int32)
    hp, hs = x_prompt, x_sample
    st_p = [[], [], [], [], []]
    st_s = [[], [], [], [], []]
    for i in range(DEPTH):
        lw = {'w_in': w_in[i], 'conv_w': conv_w[i], 'conv_b': conv_b[i], 'rg_wa': rg_wa[i],
              'rg_ba': rg_ba[i], 'rg_wx': rg_wx[i], 'rg_bx': rg_bx[i], 'rg_lambda': rg_lambda[i],
              'w_o_rnn': w_o_rnn[i], 'w_o_att': w_o_att[i], 'w_out': w_out[i], 'ln1_g': ln1_g[i],
              'ln1_b': ln1_b[i], 'w_router': w_router[i], 'b_router': b_router[i], 'w_up': w_up[i],
              'b_up': b_up[i], 'w_down': w_down[i], 'b_down': b_down[i], 'ln2_g': ln2_g[i],
              'ln2_b': ln2_b[i], 'w_ple_gate': w_ple_gate[i], 'w_ple_proj': w_ple_proj[i]}
        conv0 = jnp.zeros((B, CONV_W - 1, D_RNN), hp.dtype)
        h0 = jnp.zeros((B, D_RNN), hp.dtype)
        hp, new_p = trunk_layer(hp, p_prompt[i], pos_p, conv0, h0, prompt_sparse_attention, lw)
        attend_s = functools.partial(sample_sparse_attention, cache_k=cache_k[i], cache_v=cache_v[i],
                                     cache_kidx=cache_kidx[i], page_table=page_table)
        hs, new_s = trunk_layer(hs, p_sample[i], pos_s, state_conv[i], state_h[i], attend_s, lw)
        for lst, a in zip(st_p, new_p):
            lst.append(a)
        for lst, a in zip(st_s, new_s):
            lst.append(a)
    k_p, v_p, kidx_p, h_p, conv_p = [jnp.stack(l) for l in st_p]
    k_s, v_s, kidx_s, h_s, conv_s = [jnp.stack(l) for l in st_s]
    return (hp, hs, k_p, v_p, kidx_p, h_p, conv_p, k_s, v_s, kidx_s, h_s, conv_s)
```

```python
import functools

import jax
import jax.numpy as jnp
from jax import lax
from jax.experimental import pallas as pl
from jax.experimental.pallas import tpu as pltpu

F32 = jnp.float32
BF16 = jnp.bfloat16
I32 = jnp.int32

N_HEADS = 16
HEAD_DIM = 128
N_KV = 4
Q_PER_KV = N_HEADS // N_KV
IDX_HEADS = 16
IDX_DIM = 64
TOPK_MAX = 256
Q_BLOCK = 128
ROPE_THETA = 10000.0
RNN_BW = 128
CONV_W = 4
RG_C = 8.0
N_EXPERTS = 32
TOP_K = 4
SWIGLU_LIMIT = 7.0
SWIGLU_ALPHA = 1.702
LN_EPS = 1e-5
LANES = 128
INT_MIN = -(2 ** 31)
NEG_BIG = -1e30
VMEM_LIMIT = 56 * 1024 * 1024


def _pick(n, cands):
    for c in cands:
        if n % c == 0:
            return c
    return n


def _cparams(sem):
    return pltpu.CompilerParams(dimension_semantics=sem, vmem_limit_bytes=VMEM_LIMIT)


def _const_spec(shape):
    nd = len(shape)
    return pl.BlockSpec(shape, lambda *a: (0,) * nd, pipeline_mode=pl.Buffered(1))


def _proj_kernel(*refs, mode, n_tab, scale):
    x_ref, w_ref = refs[0], refs[1]
    tabs = refs[2:2 + n_tab]
    outs = refs[2 + n_tab:]
    z = jnp.dot(x_ref[...], w_ref[...], preferred_element_type=F32)
    if mode == "plain":
        for o in outs:
            o[...] = z.astype(o.dtype)
        return
    for h in range(z.shape[1] // LANES):
        zh = z[:, h * LANES:(h + 1) * LANES]
        if mode == "rope128":
            r = zh * tabs[0][...] + pltpu.roll(zh, 64, 1) * tabs[1][...]
        else:
            r = (zh * tabs[0][...] + pltpu.roll(zh, 96, 1) * tabs[1][...]
                 + pltpu.roll(zh, 32, 1) * tabs[2][...])
        if scale != 1.0:
            r = r * scale
        for o in outs:
            o[:, h * LANES:(h + 1) * LANES] = r.astype(o.dtype)


def _proj(x, w, col0, ncols, tabs, mode, out_dtypes, scale=1.0):
    t, k = x.shape
    tm = _pick(t, (1024, 512, 256, 128))
    tn = _pick(ncols, (512, 256, 128))
    c0 = col0 // tn
    in_specs = [pl.BlockSpec((tm, k), lambda i, j: (i, 0)),
                pl.BlockSpec((k, tn), lambda i, j: (0, c0 + j))]
    in_specs += [pl.BlockSpec((tm, LANES), lambda i, j: (i, 0)) for _ in tabs]
    outs = pl.pallas_call(
        functools.partial(_proj_kernel, mode=mode, n_tab=len(tabs), scale=scale),
        out_shape=[jax.ShapeDtypeStruct((t, ncols), d) for d in out_dtypes],
        grid=(t // tm, ncols // tn),
        in_specs=in_specs,
        out_specs=[pl.BlockSpec((tm, tn), lambda i, j: (i, j)) for _ in out_dtypes],
        compiler_params=_cparams(("parallel", "parallel")),
        name="proj_" + mode,
    )(x, w, *tabs)
    return outs


def _rope_tables(pos):
    posf = pos.astype(F32)[:, None]
    h128 = HEAD_DIM // 2
    inv = ROPE_THETA ** (-jnp.arange(h128, dtype=F32) / h128)
    c, s = jnp.cos(posf * inv), jnp.sin(posf * inv)
    t128 = (jnp.concatenate([c, c], 1), jnp.concatenate([-s, s], 1))
    h64 = IDX_DIM // 2
    inv = ROPE_THETA ** (-jnp.arange(h64, dtype=F32) / h64)
    c, s = jnp.cos(posf * inv), jnp.sin(posf * inv)
    z = jnp.zeros_like(s)
    c64 = jnp.concatenate([c, c], 1)
    sa64 = jnp.concatenate([-s, z], 1)
    sb64 = jnp.concatenate([z, s], 1)
    t64 = tuple(jnp.concatenate([a, a], 1) for a in (c64, sa64, sb64))
    idx_scale = (IDX_HEADS * IDX_DIM) ** -0.5
    n = pos.shape[0]
    ck = jnp.concatenate([c64, jnp.full((n, IDX_HEADS), idx_scale, F32),
                          jnp.zeros((n, LANES - IDX_DIM - IDX_HEADS), F32)], 1)
    z64 = jnp.zeros((n, LANES - IDX_DIM), F32)
    tkw = (ck, jnp.concatenate([sa64, z64], 1), jnp.concatenate([sb64, z64], 1))
    return t128, t64, tkw


def _softplus_neg(lam):
    return jnp.maximum(-lam, 0.0) + jnp.log1p(jnp.exp(-jnp.abs(lam)))


def _gelu_tanh(x):
    return 0.5 * x * (1.0 + jnp.tanh(0.7978845608028654 * (x + 0.044715 * (x * x * x))))


def _rglru_gates(xc, wa_ref, wx_ref, ba, bx, lam):
    nblk = xc.shape[1] // RNN_BW
    rs, gs = [], []
    for j in range(nblk):
        xj = xc[:, j * RNN_BW:(j + 1) * RNN_BW].astype(BF16)
        rs.append(jnp.dot(xj, wa_ref[j], preferred_element_type=F32))
        gs.append(jnp.dot(xj, wx_ref[j], preferred_element_type=F32))
    r = jax.nn.sigmoid(jnp.concatenate(rs, 1) + ba)
    g = jax.nn.sigmoid(jnp.concatenate(gs, 1) + bx)
    log_a = -RG_C * r * _softplus_neg(lam)
    a = jnp.exp(log_a)
    mult = jnp.sqrt(jnp.tanh(-log_a) * (1.0 + a * a))
    return a, mult, g * xc


def _rnn_prompt_kernel(xr_ref, gr_ref, cw_ref, cb_ref, wa_ref, wx_ref, ba_ref, bx_ref, lam_ref,
                       y_ref, hl_ref, xbuf, hcar, a_scr, b_scr, h_scr):
    t = pl.program_id(1)
    tm, cw = xr_ref.shape

    @pl.when(t == 0)
    def _():
        xbuf[0:8, :] = jnp.zeros((8, cw), F32)
        hcar[...] = jnp.zeros_like(hcar)

    x = xr_ref[...]
    xbuf[8:8 + tm, :] = x
    w = cw_ref[...]
    xc = (cb_ref[...] + w[3:4] * x + w[2:3] * xbuf[7:7 + tm, :]
          + w[1:2] * xbuf[6:6 + tm, :] + w[0:1] * xbuf[5:5 + tm, :])
    xbuf[0:8, :] = x[tm - 8:tm, :]
    a, mult, gx = _rglru_gates(xc, wa_ref, wx_ref, ba_ref[...], bx_ref[...], lam_ref[...])
    pos = t * tm + lax.broadcasted_iota(I32, (tm, cw), 0)
    mult = jnp.where(pos == 0, 1.0, mult)
    a_scr[...] = a
    b_scr[...] = mult * gx
    row8 = lax.broadcasted_iota(I32, (8, cw), 0)

    def group(g, carry):
        r0 = pl.multiple_of(g * 8, 8)
        av = a_scr[pl.ds(r0, 8), :]
        bv = b_scr[pl.ds(r0, 8), :]
        for d in (1, 2, 4):
            a_s = pltpu.roll(av, d, 0)
            b_s = pltpu.roll(bv, d, 0)
            m = row8 >= d
            bv = jnp.where(m, av * b_s + bv, bv)
            av = jnp.where(m, av * a_s, av)
        h = av * carry + bv
        h_scr[pl.ds(r0, 8), :] = h
        return h[7:8, :]

    carry = lax.fori_loop(0, tm // 8, group, hcar[0:1, :])
    hcar[0:1, :] = carry
    y_ref[...] = (_gelu_tanh(gr_ref[...]) * h_scr[...]).astype(y_ref.dtype)

    @pl.when(t == pl.num_programs(1) - 1)
    def _():
        hl_ref[...] = carry


def _rnn_prompt(zr, s, conv_w, conv_b, wa, wx, ba, bx, lam):
    d = conv_w.shape[1]
    cw = 512
    tm = _pick(s, (256, 128))
    ncb = d // cw
    nb = cw // RNN_BW
    vec = lambda: pl.BlockSpec((1, cw), lambda c, t: (0, c))
    y, hl = pl.pallas_call(
        _rnn_prompt_kernel,
        out_shape=[jax.ShapeDtypeStruct((s, d), BF16), jax.ShapeDtypeStruct((1, d), F32)],
        grid=(ncb, s // tm),
        in_specs=[pl.BlockSpec((tm, cw), lambda c, t: (t, c)),
                  pl.BlockSpec((tm, cw), lambda c, t: (t, ncb + c)),
                  pl.BlockSpec((CONV_W, cw), lambda c, t: (0, c)),
                  vec(),
                  pl.BlockSpec((nb, RNN_BW, RNN_BW), lambda c, t: (c, 0, 0)),
                  pl.BlockSpec((nb, RNN_BW, RNN_BW), lambda c, t: (c, 0, 0)),
                  vec(), vec(), vec()],
        out_specs=[pl.BlockSpec((tm, cw), lambda c, t: (t, c)),
                   pl.BlockSpec((1, cw), lambda c, t: (0, c))],
        scratch_shapes=[pltpu.VMEM((tm + 8, cw), F32), pltpu.VMEM((8, cw), F32),
                        pltpu.VMEM((tm, cw), F32), pltpu.VMEM((tm, cw), F32),
                        pltpu.VMEM((tm, cw), F32)],
        compiler_params=_cparams(("parallel", "arbitrary")),
        name="rnn_prompt",
    )(zr, zr, conv_w, conv_b, wa, wx, ba, bx, lam)
    return y, hl


def _rnn_sample_kernel(xr_ref, gr_ref, c0_ref, c1_ref, c2_ref, h0_ref, cw_ref, cb_ref,
                       wa_ref, wx_ref, ba_ref, bx_ref, lam_ref, y_ref, h_ref):
    w = cw_ref[...]
    xc = (cb_ref[...] + w[3:4] * xr_ref[...] + w[2:3] * c2_ref[...]
          + w[1:2] * c1_ref[...] + w[0:1] * c0_ref[...])
    a, mult, gx = _rglru_gates(xc, wa_ref, wx_ref, ba_ref[...], bx_ref[...], lam_ref[...])
    h = a * h0_ref[...] + mult * gx
    h_ref[...] = h
    y_ref[...] = (_gelu_tanh(gr_ref[...]) * h).astype(y_ref.dtype)


def _rnn_sample(zr, conv_state, h0, conv_w, conv_b, wa, wx, ba, bx, lam):
    n, d = h0.shape
    cw = 512
    ncb = d // cw
    nb = cw // RNN_BW
    blk = lambda off: pl.BlockSpec((n, cw), lambda c: (0, off + c))
    vec = lambda: pl.BlockSpec((1, cw), lambda c: (0, c))
    wsp = lambda: pl.BlockSpec((nb, RNN_BW, RNN_BW), lambda c: (c, 0, 0))
    return pl.pallas_call(
        _rnn_sample_kernel,
        out_shape=[jax.ShapeDtypeStruct((n, d), BF16), jax.ShapeDtypeStruct((n, d), F32)],
        grid=(ncb,),
        in_specs=[blk(0), blk(ncb), blk(0), blk(0), blk(0), blk(0),
                  pl.BlockSpec((CONV_W, cw), lambda c: (0, c)), vec(), wsp(), wsp(),
                  vec(), vec(), vec()],
        out_specs=[blk(0), blk(0)],
        compiler_params=_cparams(("parallel",)),
        name="rnn_sample",
    )(zr, zr, conv_state[:, 0], conv_state[:, 1], conv_state[:, 2], h0,
      conv_w, conv_b, wa, wx, ba, bx, lam)


def _layer_norm(x, g, b):
    mu = jnp.mean(x, axis=-1, keepdims=True)
    xc = x - mu
    var = jnp.mean(xc * xc, axis=-1, keepdims=True)
    return xc * lax.rsqrt(var + LN_EPS) * g + b


def _merge_kernel(yr_ref, ya_ref, wr_ref, wa_ref, ga_ref, gb_ref, o_ref):
    a = jnp.dot(yr_ref[...], wr_ref[...], preferred_element_type=F32)
    b = jnp.dot(ya_ref[...], wa_ref[...], preferred_element_type=F32)
    m = jax.nn.sigmoid(ga_ref[...]) * a + jax.nn.sigmoid(gb_ref[...]) * b
    o_ref[...] = m.astype(o_ref.dtype)


def _merge(y_rnn, y_att, w_o_rnn, w_o_att, zr):
    t, d = y_rnn.shape
    tm = _pick(t, (512, 256, 128))
    tn = 512
    nj = d // tn
    return pl.pallas_call(
        _merge_kernel,
        out_shape=jax.ShapeDtypeStruct((t, d), BF16),
        grid=(t // tm, nj),
        in_specs=[pl.BlockSpec((tm, d), lambda i, j: (i, 0)),
                  pl.BlockSpec((tm, d), lambda i, j: (i, 0)),
                  pl.BlockSpec((d, tn), lambda i, j: (0, j)),
                  pl.BlockSpec((d, tn), lambda i, j: (0, j)),
                  pl.BlockSpec((tm, tn), lambda i, j: (i, 2 * nj + j)),
                  pl.BlockSpec((tm, tn), lambda i, j: (i, 3 * nj + j))],
        out_specs=pl.BlockSpec((tm, tn), lambda i, j: (i, j)),
        compiler_params=_cparams(("parallel", "parallel")),
        name="merge",
    )(y_rnn, y_att, w_o_rnn, w_o_att, zr, zr)


def _ln1_router_kernel(*refs, alpha, n_skip):
    m_ref, x_ref, w_ref, g_ref, b_ref, wr_ref, br_ref, x1_ref, sel_ref, gate_ref = refs[n_skip:]
    y = alpha * x_ref[...] + jnp.dot(m_ref[...], w_ref[...], preferred_element_type=F32)
    x1 = _layer_norm(y, g_ref[...], b_ref[...])
    x1_ref[...] = x1
    logits = jnp.dot(x1, wr_ref[...], preferred_element_type=F32,
                     precision=lax.Precision.HIGHEST) + br_ref[...]
    lane = lax.broadcasted_iota(I32, logits.shape, 1)
    live = lane < N_EXPERTS
    cur = jnp.where(live, logits, -jnp.inf)
    top = jnp.max(cur, axis=1, keepdims=True)
    sel = jnp.zeros(logits.shape, jnp.bool_)
    for _ in range(TOP_K):
        mx = jnp.max(cur, axis=1, keepdims=True)
        first = jnp.min(jnp.where(cur == mx, lane, LANES), axis=1, keepdims=True)
        pick = lane == first
        sel = jnp.logical_or(sel, pick)
        cur = jnp.where(pick, -jnp.inf, cur)
    e = jnp.where(sel, jnp.exp(logits - top), 0.0)
    sel_ref[...] = jnp.where(sel, 1.0, 0.0)
    gate_ref[...] = e / jnp.sum(e, axis=1, keepdims=True)


def _ln1_router(m, x, w_out, g, b, w_router, b_router, alpha, row0, t_all, prev=None):
    t, d = x.shape
    tm = _pick(t, (256, 128))
    r0 = row0 // tm
    in_specs = [pl.BlockSpec((tm, d), lambda i: (i, 0)),
                pl.BlockSpec((tm, d), lambda i: (i, 0)),
                _const_spec((d, d)), _const_spec((1, d)), _const_spec((1, d)),
                _const_spec((d, LANES)), _const_spec((1, LANES))]
    args = [m, x, w_out, g, b, w_router, b_router]
    aliases = {}
    if prev is not None:
        in_specs = [pl.BlockSpec(memory_space=pl.ANY)] * 3 + in_specs
        args = list(prev) + args
        aliases = {0: 0, 1: 1, 2: 2}

    return pl.pallas_call(
        functools.partial(_ln1_router_kernel, alpha=alpha, n_skip=len(aliases)),
        out_shape=[jax.ShapeDtypeStruct((t_all, d), F32),
                   jax.ShapeDtypeStruct((t_all, LANES), F32),
                   jax.ShapeDtypeStruct((t_all, LANES), F32)],
        grid=(t // tm,),
        in_specs=in_specs,
        out_specs=[pl.BlockSpec((tm, d), lambda i: (r0 + i, 0)),
                   pl.BlockSpec((tm, LANES), lambda i: (r0 + i, 0)),
                   pl.BlockSpec((tm, LANES), lambda i: (r0 + i, 0))],
        input_output_aliases=aliases,
        compiler_params=_cparams(("parallel",)),
        name="ln1_router",
    )(*args)


def _sort_key(x):
    bits = lax.bitcast_convert_type(x, I32)
    return bits ^ (jnp.right_shift(bits, 31) & 0x7FFFFFFF)


def _count(keys_ref, nch, cw, tvec, strict):
    rows = keys_ref.shape[0]
    tb = jnp.broadcast_to(tvec, (rows, LANES))

    def body(c, cnt):
        off = c * cw
        for s in range(cw // LANES):
            k = keys_ref[:, pl.ds(pl.multiple_of(off + s * LANES, LANES), LANES)]
            hit = (k > tb) if strict else (k >= tb)
            cnt = cnt + jnp.where(hit, 1.0, 0.0)
        return cnt

    cnt = lax.fori_loop(0, nch, body, jnp.zeros((rows, LANES), F32))
    return jnp.sum(cnt, axis=1, keepdims=True)


def _select_threshold(keys_ref, nch, cw, n_sel):
    rows = keys_ref.shape[0]

    def bit_step(b, cur):
        cand = cur | lax.shift_left(jnp.int32(1), 31 - b)
        cnt = _count(keys_ref, nch, cw, cand ^ INT_MIN, False)
        return jnp.where(cnt >= n_sel, cand, cur)

    cur = lax.fori_loop(0, 32, bit_step, jnp.zeros((rows, 1), I32))
    t = cur ^ INT_MIN
    n_ge = _count(keys_ref, nch, cw, t, False)
    n_gt = _count(keys_ref, nch, cw, t, True)
    floor = t == INT_MIN
    need = jnp.where(floor, 1e9, n_sel - n_gt)
    tied = jnp.where(jnp.logical_or(floor, n_ge <= n_sel), 0.0, 1.0)

    @pl.when(jnp.max(tied) > 0.0)
    def _():
        tb = jnp.broadcast_to(t, (rows, LANES))
        needb = jnp.broadcast_to(need, (rows, LANES))
        r = lax.broadcasted_iota(I32, (LANES, LANES), 0)
        c = lax.broadcasted_iota(I32, (LANES, LANES), 1)
        upper = jnp.where(r < c, 1.0, 0.0).astype(BF16)

        def fix(j, run):
            sl = pl.ds(pl.multiple_of(j * LANES, LANES), LANES)
            k = keys_ref[:, sl]
            eq = k == tb
            eqf = jnp.where(eq, 1.0, 0.0)
            before = jnp.dot(eqf.astype(BF16), upper, preferred_element_type=F32) + run
            drop = jnp.logical_and(eq, before >= needb)
            keys_ref[:, sl] = jnp.where(drop, INT_MIN, k)
            return run + jnp.sum(eqf, axis=1, keepdims=True)

        lax.fori_loop(0, nch * (cw // LANES), fix, jnp.zeros((rows, 1), F32))

    return jnp.maximum(t, INT_MIN + 1)


def _attn_prompt_kernel(qi_ref, kw_ref, q_ref, kit_ref, kt_ref, v_ref, o_ref,
                        keys_scr, lhs_scr, wb_scr, *, n_sel, kc):
    i = pl.program_id(0)
    qb = Q_BLOCK
    nch = (i * qb + qb + kc - 1) // kc
    kw = kw_ref[...]
    lane = lax.broadcasted_iota(I32, (qb, LANES), 1)
    for h in range(IDX_HEADS):
        wb_scr[h] = jnp.broadcast_to(kw[:, IDX_DIM + h:IDX_DIM + h + 1], (qb, LANES))
        blk = qi_ref[:, (h // 2) * LANES:(h // 2 + 1) * LANES]
        keep = (lane < IDX_DIM) if h % 2 == 0 else (lane >= IDX_DIM)
        lhs_scr[h] = jnp.where(keep, blk, jnp.zeros_like(blk))

    sw = min(256, kc)

    def score_chunk(c, carry):
        for s in range(kc // sw):
            o2 = pl.multiple_of(c * kc + s * sw, sw)
            kt = kit_ref[:, pl.ds(o2, sw)]
            acc = jnp.zeros((qb, sw), F32)
            for h in range(IDX_HEADS):
                sc = jnp.dot(lhs_scr[h], kt, preferred_element_type=F32)
                wb = wb_scr[h]
                acc = acc + jnp.maximum(sc, 0.0) * jnp.concatenate([wb] * (sw // LANES), axis=1)
            kpos = o2 + lax.broadcasted_iota(I32, (qb, sw), 1)
            qpos = i * qb + lax.broadcasted_iota(I32, (qb, sw), 0)
            keys_scr[:, pl.ds(o2, sw)] = jnp.where(kpos <= qpos, _sort_key(acc), INT_MIN)
        return carry

    lax.fori_loop(0, nch, score_chunk, 0)
    t = _select_threshold(keys_scr, nch, kc, n_sel)
    tb = jnp.broadcast_to(t, (qb, kc))

    for g in range(N_KV):
        qg = jnp.concatenate(
            [q_ref[:, (g * Q_PER_KV + j) * HEAD_DIM:(g * Q_PER_KV + j + 1) * HEAD_DIM]
             for j in range(Q_PER_KV)], axis=0)

        def body(c, carry, g=g, qg=qg):
            m, l, acc = carry
            off = pl.multiple_of(c * kc, kc)
            s = jnp.dot(qg, kt_ref[g * HEAD_DIM:(g + 1) * HEAD_DIM, pl.ds(off, kc)],
                        preferred_element_type=F32)
            km = keys_scr[:, pl.ds(off, kc)] >= tb
            s = jnp.concatenate([jnp.where(km, s[j * qb:(j + 1) * qb], NEG_BIG)
                                 for j in range(Q_PER_KV)], axis=0)
            m_new = jnp.maximum(m, jnp.max(s, axis=1, keepdims=True))
            alpha = jnp.exp(m - m_new)
            p = jnp.exp(s - m_new)
            l = alpha * l + jnp.sum(p, axis=1, keepdims=True)
            pv = jnp.dot(p.astype(BF16), v_ref[pl.ds(off, kc), g * HEAD_DIM:(g + 1) * HEAD_DIM],
                         preferred_element_type=F32)
            return m_new, l, alpha * acc + pv

        rows = Q_PER_KV * qb
        m, l, acc = lax.fori_loop(
            0, nch, body,
            (jnp.full((rows, 1), NEG_BIG, F32), jnp.zeros((rows, 1), F32),
             jnp.zeros((rows, HEAD_DIM), F32)))
        out = acc / l
        for j in range(Q_PER_KV):
            h = g * Q_PER_KV + j
            o_ref[:, h * HEAD_DIM:(h + 1) * HEAD_DIM] = out[j * qb:(j + 1) * qb].astype(o_ref.dtype)


def _attn_prompt(qi, kw, q, kit2, kt, v, s, n_sel):
    kc = min(512, s)
    d = N_HEADS * HEAD_DIM
    return pl.pallas_call(
        functools.partial(_attn_prompt_kernel, n_sel=n_sel, kc=kc),
        out_shape=jax.ShapeDtypeStruct((s, d), BF16),
        grid=(s // Q_BLOCK,),
        in_specs=[pl.BlockSpec((Q_BLOCK, IDX_HEADS * IDX_DIM), lambda i: (i, 0)),
                  pl.BlockSpec((Q_BLOCK, LANES), lambda i: (i, 0)),
                  pl.BlockSpec((Q_BLOCK, d), lambda i: (i, 0)),
                  _const_spec((LANES, s)), _const_spec((N_KV * HEAD_DIM, s)),
                  _const_spec((s, N_KV * HEAD_DIM))],
        out_specs=pl.BlockSpec((Q_BLOCK, d), lambda i: (i, 0)),
        scratch_shapes=[pltpu.VMEM((Q_BLOCK, s), I32),
                        pltpu.VMEM((IDX_HEADS, Q_BLOCK, LANES), BF16),
                        pltpu.VMEM((IDX_HEADS, Q_BLOCK, LANES), F32)],
        compiler_params=_cparams(("parallel",)),
        name="attn_prompt",
    )(qi, kw, q, kit2, kt, v)


def _idx_sample_kernel(pt_ref, qi_ref, w_ref, kn_ref, *refs, npg, past):
    pages, o_ref = refs[:npg], refs[npg]
    qi = qi_ref[...]
    kp = jnp.concatenate([p[...] for p in pages], axis=0).astype(BF16)
    sc = lax.dot_general(qi, kp, (((1,), (1,)), ((), ())), preferred_element_type=F32)
    w = w_ref[...]
    o_ref[:, 0:past] = jnp.sum(jnp.maximum(sc, 0.0) * w, axis=0, keepdims=True)
    kn = kn_ref[...][:, :IDX_DIM].astype(BF16).astype(F32)
    sn = jnp.sum(qi.astype(F32) * kn, axis=1, keepdims=True)
    new = jnp.sum(jnp.maximum(sn, 0.0) * w, axis=0, keepdims=True)
    lane = lax.broadcasted_iota(I32, (1, LANES), 1)
    o_ref[:, past:past + LANES] = jnp.where(lane == 0, new, -jnp.inf)


def _idx_sample(page_table, qi3, w3, kn3, cache_kidx):
    n, npg = page_table.shape
    page = cache_kidx.shape[1]
    past = npg * page
    page_specs = [pl.BlockSpec((None, page, IDX_DIM), lambda b, pt, p=p: (pt[b * npg + p], 0, 0))
                  for p in range(npg)]
    return pl.pallas_call(
        functools.partial(_idx_sample_kernel, npg=npg, past=past),
        out_shape=jax.ShapeDtypeStruct((n, 1, past + LANES), F32),
        grid_spec=pltpu.PrefetchScalarGridSpec(
            num_scalar_prefetch=1, grid=(n,),
            in_specs=[pl.BlockSpec((None, IDX_HEADS, IDX_DIM), lambda b, pt: (b, 0, 0)),
                      pl.BlockSpec((None, IDX_HEADS, 1), lambda b, pt: (b, 0, 0)),
                      pl.BlockSpec((None, 1, LANES), lambda b, pt: (b, 0, 0))] + page_specs,
            out_specs=pl.BlockSpec((None, 1, past + LANES), lambda b, pt: (b, 0, 0))),
        compiler_params=_cparams(("parallel",)),
        name="idx_sample",
    )(page_table.reshape(-1), qi3, w3, kn3, *([cache_kidx] * npg))


def _mask_sample_kernel(sc_ref, m_ref, keys_scr, *, n_valid, n_sel):
    rows, width = sc_ref.shape
    pos = lax.broadcasted_iota(I32, (rows, width), 1)
    keys_scr[...] = jnp.where(pos < n_valid, _sort_key(sc_ref[...]), INT_MIN)
    t = _select_threshold(keys_scr, width // LANES, LANES, n_sel)
    m_ref[...] = jnp.where(keys_scr[...] >= t, 1.0, 0.0)


def _mask_sample(sc, n_valid, n_sel):
    n, width = sc.shape
    return pl.pallas_call(
        functools.partial(_mask_sample_kernel, n_valid=n_valid, n_sel=n_sel),
        out_shape=jax.ShapeDtypeStruct((n, width), F32),
        scratch_shapes=[pltpu.VMEM((n, width), I32)],
        name="mask_sample",
    )(sc)


def _attn_sample_kernel(pt_ref, q_ref, m_ref, kn_ref, vn_ref, *refs, npg, past):
    kpages, vpages, o_ref = refs[:npg], refs[npg:2 * npg], refs[2 * npg]
    qbd = q_ref[...]
    kp = jnp.concatenate([p[...] for p in kpages], axis=0).astype(BF16)
    s = lax.dot_general(qbd, kp, (((1,), (1,)), ((), ())), preferred_element_type=F32)
    s = jnp.where(m_ref[:, 0:past] > 0.5, s, NEG_BIG)
    sn = jnp.sum(qbd.astype(F32) * kn_ref[...].astype(F32), axis=1, keepdims=True)
    sn = jnp.where(m_ref[:, past:past + 1] > 0.5, sn, NEG_BIG)
    m = jnp.maximum(jnp.max(s, axis=1, keepdims=True), sn)
    p = jnp.exp(s - m)
    pn = jnp.exp(sn - m)
    l = jnp.sum(p, axis=1, keepdims=True) + pn
    vp = jnp.concatenate([r[...] for r in vpages], axis=0).astype(BF16)
    o = jnp.dot(p.astype(BF16), vp, preferred_element_type=F32)
    o = (o + pn.astype(BF16).astype(F32) * vn_ref[...].astype(F32)) / l
    grp = lax.broadcasted_iota(I32, (N_HEADS, HEAD_DIM), 0) // Q_PER_KV
    out = jnp.zeros((N_HEADS, HEAD_DIM), F32)
    for g in range(N_KV):
        out = out + jnp.where(grp == g, o[:, g * HEAD_DIM:(g + 1) * HEAD_DIM], 0.0)
    o_ref[...] = out.astype(o_ref.dtype)


def _attn_sample(page_table, qbd, mask3, kn3, vn3, cache_k, cache_v):
    n, npg = page_table.shape
    page = cache_k.shape[1]
    past = npg * page
    dkv = N_KV * HEAD_DIM
    pspec = lambda p: pl.BlockSpec((None, page, dkv), lambda b, pt, p=p: (pt[b * npg + p], 0, 0))
    return pl.pallas_call(
        functools.partial(_attn_sample_kernel, npg=npg, past=past),
        out_shape=jax.ShapeDtypeStruct((n, N_HEADS, HEAD_DIM), BF16),
        grid_spec=pltpu.PrefetchScalarGridSpec(
            num_scalar_prefetch=1, grid=(n,),
            in_specs=[pl.BlockSpec((None, N_HEADS, dkv), lambda b, pt: (b, 0, 0)),
                      pl.BlockSpec((None, 1, past + LANES), lambda b, pt: (b, 0, 0)),
                      pl.BlockSpec((None, 1, dkv), lambda b, pt: (b, 0, 0)),
                      pl.BlockSpec((None, 1, dkv), lambda b, pt: (b, 0, 0))]
            + [pspec(p) for p in range(npg)] + [pspec(p) for p in range(npg)],
            out_specs=pl.BlockSpec((None, N_HEADS, HEAD_DIM), lambda b, pt: (b, 0, 0))),
        compiler_params=_cparams(("parallel",)),
        name="attn_sample",
    )(page_table.reshape(-1), qbd, mask3, kn3, vn3, *([cache_k] * npg), *([cache_v] * npg))


MOE_RB = 256
MOE_RC = 1280
MOE_TF = 512
MOE_TN = 512


def _moe_sizes(n_tok):
    n_assign = n_tok * TOP_K
    n_slots = (n_assign // MOE_RB + N_EXPERTS) * MOE_RB
    n_units = N_EXPERTS + -(-n_slots // MOE_RC)
    return n_slots, n_units


def _moe_rank_kernel(sel_ref, rank_ref, cnt_ref, carry):
    i = pl.program_id(0)
    tp = sel_ref.shape[0]

    @pl.when(i == 0)
    def _():
        carry[...] = jnp.zeros_like(carry)

    a = sel_ref[...]
    r = lax.broadcasted_iota(I32, (tp, tp), 0)
    c = lax.broadcasted_iota(I32, (tp, tp), 1)
    lower = jnp.where(c < r, 1.0, 0.0).astype(BF16)
    rank_ref[...] = jnp.dot(lower, a.astype(BF16), preferred_element_type=F32) + carry[0:1, :]
    carry[...] = carry[...] + jnp.sum(a, axis=0, keepdims=True)
    cnt_ref[...] = carry[...]


def _moe_rank(sel):
    t = sel.shape[0]
    tp = _pick(t, (256, 128))
    return pl.pallas_call(
        _moe_rank_kernel,
        out_shape=[jax.ShapeDtypeStruct((t, LANES), F32), jax.ShapeDtypeStruct((8, LANES), F32)],
        grid=(t // tp,),
        in_specs=[pl.BlockSpec((tp, LANES), lambda i: (i, 0))],
        out_specs=[pl.BlockSpec((tp, LANES), lambda i: (i, 0)),
                   pl.BlockSpec((8, LANES), lambda i: (0, 0))],
        scratch_shapes=[pltpu.VMEM((8, LANES), F32)],
        compiler_params=_cparams(("arbitrary",)),
        name="moe_rank",
    )(sel)


def _moe_dest_kernel(sel_ref, gate_ref, rank_ref, cnt_ref, dest_ref, g4_ref, tab_ref):
    cnt = cnt_ref[...]
    lane8 = lax.broadcasted_iota(I32, cnt.shape, 1)
    padded = jnp.ceil(cnt * (1.0 / MOE_RB)) * MOE_RB
    incl = padded
    for d in (1, 2, 4, 8, 16, 32, 64):
        incl = incl + jnp.where(lane8 >= d, pltpu.roll(incl, d, 1), 0.0)
    start = incl - padded
    row8 = lax.broadcasted_iota(I32, cnt.shape, 0)
    tab_ref[...] = jnp.where(row8 == 0, start, jnp.where(row8 == 1, padded, 0.0)).astype(I32)
    dest = start[0:1, :] + rank_ref[...]
    gate = gate_ref[...]
    cur = sel_ref[...]
    lane = lax.broadcasted_iota(I32, cur.shape, 1)
    d4 = jnp.zeros(cur.shape, F32)
    g4 = jnp.zeros(cur.shape, F32)
    for k in range(TOP_K):
        first = jnp.min(jnp.where(cur > 0.5, lane, LANES), axis=1, keepdims=True)
        pick = lane == first
        dk = jnp.sum(jnp.where(pick, dest, 0.0), axis=1, keepdims=True)
        gk = jnp.sum(jnp.where(pick, gate, 0.0), axis=1, keepdims=True)
        d4 = jnp.where(lane == k, dk, d4)
        g4 = jnp.where(lane == k, gk, g4)
        cur = jnp.where(pick, 0.0, cur)
    dest_ref[...] = d4.astype(I32)
    g4_ref[...] = g4


def _moe_dest(sel, gate, rank, cnt):
    t = sel.shape[0]
    tp = _pick(t, (256, 128))
    row = lambda: pl.BlockSpec((tp, LANES), lambda i: (i, 0))
    one = lambda: pl.BlockSpec((8, LANES), lambda i: (0, 0))
    return pl.pallas_call(
        _moe_dest_kernel,
        out_shape=[jax.ShapeDtypeStruct((t, LANES), I32), jax.ShapeDtypeStruct((t, LANES), F32),
                   jax.ShapeDtypeStruct((8, LANES), I32)],
        grid=(t // tp,),
        in_specs=[row(), row(), row(), one()],
        out_specs=[row(), row(), one()],
        compiler_params=_cparams(("arbitrary",)),
        name="moe_dest",
    )(sel, gate, rank, cnt)


def _moe_units_kernel(tab_ref, unit_ref, *, n_units):
    rcb = MOE_RC // MOE_RB

    def per_expert(e, state):
        u0, _ = state
        nb = tab_ref[1, e] // MOE_RB
        b0 = tab_ref[0, e] // MOE_RB
        nu = (nb + rcb - 1) // rcb

        def per_unit(j, carry):
            unit_ref[0, u0 + j] = e
            unit_ref[1, u0 + j] = b0 + j * rcb
            unit_ref[2, u0 + j] = jnp.minimum(nb - j * rcb, rcb)
            return carry

        lax.fori_loop(0, nu, per_unit, 0)
        return u0 + nu, jnp.where(nu > 0, e, state[1])

    used, last = lax.fori_loop(0, N_EXPERTS, per_expert, (jnp.int32(0), jnp.int32(0)))

    def fill(u, carry):
        unit_ref[0, u] = last
        unit_ref[1, u] = 0
        unit_ref[2, u] = 0
        return carry

    lax.fori_loop(used, n_units, fill, 0)


def _moe_units(tab, n_units):
    return pl.pallas_call(
        functools.partial(_moe_units_kernel, n_units=n_units),
        out_shape=jax.ShapeDtypeStruct((3, n_units), I32),
        in_specs=[pl.BlockSpec(memory_space=pltpu.SMEM)],
        out_specs=pl.BlockSpec(memory_space=pltpu.SMEM),
        name="moe_units",
    )(tab)


def _moe_scatter_kernel(dest_ref, x_ref, xs_ref, sem):
    tm = x_ref.shape[0]

    def issue(r, carry):
        for k in range(TOP_K):
            pltpu.make_async_copy(x_ref.at[pl.ds(r, 1)], xs_ref.at[pl.ds(dest_ref[r * TOP_K + k], 1)],
                                  sem).start()
        return carry

    lax.fori_loop(0, tm, issue, 0)
    for _ in range(TOP_K):
        pltpu.make_async_copy(x_ref, xs_ref.at[pl.ds(0, tm)], sem).wait()


def _moe_scatter(dest_flat, x1, n_slots):
    t, d = x1.shape
    tm = _pick(t, (256, 128))
    return pl.pallas_call(
        _moe_scatter_kernel,
        out_shape=jax.ShapeDtypeStruct((n_slots, d), F32),
        grid=(t // tm,),
        in_specs=[pl.BlockSpec((tm * TOP_K,), lambda i: (i,), memory_space=pltpu.SMEM),
                  pl.BlockSpec((tm, d), lambda i: (i, 0))],
        out_specs=pl.BlockSpec(memory_space=pl.ANY),
        scratch_shapes=[pltpu.SemaphoreType.DMA(())],
        compiler_params=_cparams(("arbitrary",)),
        name="moe_scatter",
    )(dest_flat, x1)


def _moe_expert_kernel(unit_ref, xs_ref, wg_ref, wl_ref, bg_ref, bl_ref, wd_ref, bd_ref, ys_ref,
                       x_scr, act_scr, xst, yst, wg_bf, wl_bf, wd_bf, xsem, ysem, *, ju):
    u = pl.program_id(0)
    j = pl.program_id(1)
    b0 = unit_ref[1, u]
    ns = unit_ref[2, u]
    rb = MOE_RB

    def x_copy(s, slot):
        return pltpu.make_async_copy(xs_ref.at[pl.ds((b0 + s) * rb, rb)], xst.at[slot], xsem.at[slot])

    def up_block(s):
        xb = x_scr[pl.ds(pl.multiple_of(s * rb, rb), rb), :]
        hg = jnp.dot(xb, wg_bf[...], preferred_element_type=F32) + bg_ref[...]
        hl = jnp.dot(xb, wl_bf[...], preferred_element_type=F32) + bl_ref[...]
        glu = jnp.minimum(hg, SWIGLU_LIMIT)
        lin = jnp.clip(hl, -SWIGLU_LIMIT, SWIGLU_LIMIT)
        act = glu * jax.nn.sigmoid(SWIGLU_ALPHA * glu) * (lin + 1.0)
        act_scr[pl.ds(pl.multiple_of(s * rb, rb), rb),
                pl.ds(pl.multiple_of(j * MOE_TF, MOE_TF), MOE_TF)] = act.astype(BF16)

    @pl.when(jnp.logical_and(ns > 0, j < ju))
    def _():
        wg_bf[...] = wg_ref[...].astype(BF16)
        wl_bf[...] = wl_ref[...].astype(BF16)

    @pl.when(jnp.logical_and(ns > 0, j == 0))
    def _():
        x_copy(0, 0).start()

        def body(s, carry):
            slot = s % 2

            @pl.when(s + 1 < ns)
            def _():
                x_copy(s + 1, 1 - slot).start()

            x_copy(s, slot).wait()
            x_scr[pl.ds(pl.multiple_of(s * rb, rb), rb), :] = xst[slot].astype(BF16)
            up_block(s)
            return carry

        lax.fori_loop(0, ns, body, 0)

    @pl.when(jnp.logical_and(ns > 0, jnp.logical_and(j > 0, j < ju)))
    def _():
        def body(s, carry):
            up_block(s)
            return carry

        lax.fori_loop(0, ns, body, 0)

    @pl.when(jnp.logical_and(ns > 0, j >= ju))
    def _():
        wd_bf[...] = wd_ref[...].astype(BF16)
        col = pl.multiple_of((j - ju) * MOE_TN, MOE_TN)

        def y_copy(s, slot):
            return pltpu.make_async_copy(
                yst.at[slot], ys_ref.at[pl.ds((b0 + s) * rb, rb), pl.ds(col, MOE_TN)], ysem.at[slot])

        def body(s, carry):
            slot = s % 2

            @pl.when(s >= 2)
            def _():
                y_copy(s - 2, slot).wait()

            ab = act_scr[pl.ds(pl.multiple_of(s * rb, rb), rb), :]
            yst[slot] = jnp.dot(ab, wd_bf[...], preferred_element_type=F32) + bd_ref[...]
            y_copy(s, slot).start()
            return carry

        lax.fori_loop(0, ns, body, 0)

        @pl.when(ns >= 2)
        def _():
            y_copy(ns - 2, ns % 2).wait()

        y_copy(ns - 1, (ns - 1) % 2).wait()


def _moe_experts(units, xs, w_up, b_up, w_down, b_down):
    n_slots, d = xs.shape
    n_units = units.shape[1]
    dff = w_down.shape[1]
    ju, jd = dff // MOE_TF, d // MOE_TN

    def up_idx(off):
        return lambda u, j, un: (un[0, u], 0, off + jnp.where(un[2, u] > 0, jnp.minimum(j, ju - 1), ju - 1))

    def dn_idx(u, j, un):
        return (un[0, u], 0, jnp.where(un[2, u] > 0, jnp.maximum(j - ju, 0), jd - 1))

    return pl.pallas_call(
        functools.partial(_moe_expert_kernel, ju=ju),
        out_shape=jax.ShapeDtypeStruct((n_slots, d), F32),
        grid_spec=pltpu.PrefetchScalarGridSpec(
            num_scalar_prefetch=1, grid=(n_units, ju + jd),
            in_specs=[pl.BlockSpec(memory_space=pl.ANY),
                      pl.BlockSpec((None, d, MOE_TF), up_idx(0)),
                      pl.BlockSpec((None, d, MOE_TF), up_idx(ju)),
                      pl.BlockSpec((None, 1, MOE_TF), up_idx(0)),
                      pl.BlockSpec((None, 1, MOE_TF), up_idx(ju)),
                      pl.BlockSpec((None, dff, MOE_TN), dn_idx),
                      pl.BlockSpec((None, 1, MOE_TN), dn_idx)],
            out_specs=pl.BlockSpec(memory_space=pl.ANY),
            scratch_shapes=[pltpu.VMEM((MOE_RC, d), BF16), pltpu.VMEM((MOE_RC, dff), BF16),
                            pltpu.VMEM((2, MOE_RB, d), F32), pltpu.VMEM((2, MOE_RB, MOE_TN), F32),
                            pltpu.VMEM((d, MOE_TF), BF16), pltpu.VMEM((d, MOE_TF), BF16),
                            pltpu.VMEM((dff, MOE_TN), BF16),
                            pltpu.SemaphoreType.DMA((2,)), pltpu.SemaphoreType.DMA((2,))]),
        compiler_params=_cparams(("arbitrary", "arbitrary")),
        name="moe_experts",
    )(units, xs, w_up, w_up, b_up, b_up, w_down, b_down)


def _moe_combine_kernel(dest_ref, g4_ref, x1_ref, ys_ref, p_ref, wg_ref, wp_ref, g_ref, b_ref,
                        y_ref, gbuf, sem, *, alpha):
    tm = x1_ref.shape[0]

    def issue(r, carry):
        for k in range(TOP_K):
            pltpu.make_async_copy(ys_ref.at[pl.ds(dest_ref[r * TOP_K + k], 1)],
                                  gbuf.at[k, pl.ds(r, 1)], sem).start()
        return carry

    lax.fori_loop(0, tm, issue, 0)
    for k in range(TOP_K):
        pltpu.make_async_copy(ys_ref.at[pl.ds(0, tm)], gbuf.at[k], sem).wait()
    g4 = g4_ref[...]
    moe = g4[:, 0:1] * gbuf[0]
    for k in range(1, TOP_K):
        moe = moe + g4[:, k:k + 1] * gbuf[k]
    x2 = _layer_norm(alpha * x1_ref[...] + moe, g_ref[...], b_ref[...])
    gate = jax.nn.sigmoid(jnp.dot(x2.astype(BF16), wg_ref[...], preferred_element_type=F32))
    y_ref[...] = x2 + gate * jnp.dot(p_ref[...].astype(BF16), wp_ref[...], preferred_element_type=F32)


def _moe_combine(dest_flat, g4, x1, ys, p, w_gate, w_proj, g, b, alpha, row0, t):
    d = x1.shape[1]
    tm = _pick(t, (128,))
    r0 = row0 // tm
    pd = p.shape[1]
    return pl.pallas_call(
        functools.partial(_moe_combine_kernel, alpha=alpha),
        out_shape=jax.ShapeDtypeStruct((t, d), F32),
        grid=(t // tm,),
        in_specs=[pl.BlockSpec((tm * TOP_K,), lambda i: (r0 + i,), memory_space=pltpu.SMEM),
                  pl.BlockSpec((tm, LANES), lambda i: (r0 + i, 0)),
                  pl.BlockSpec((tm, d), lambda i: (r0 + i, 0)),
                  pl.BlockSpec(memory_space=pl.ANY),
                  pl.BlockSpec((tm, pd), lambda i: (i, 0)),
                  _const_spec((d, d)), _const_spec((pd, d)), _const_spec((1, d)), _const_spec((1, d))],
        out_specs=pl.BlockSpec((tm, d), lambda i: (i, 0)),
        scratch_shapes=[pltpu.VMEM((TOP_K, tm, d), F32), pltpu.SemaphoreType.DMA(())],
        compiler_params=_cparams(("arbitrary",)),
        name="moe_combine",
    )(dest_flat, g4, x1, ys, p, w_gate, w_proj, g, b)


def _project_all(x, w_p, pos):
    t128, t64, tkw = _rope_tables(pos)
    xb = x.astype(BF16)
    dq = N_HEADS * HEAD_DIM
    dkv = N_KV * HEAD_DIM
    dqi = IDX_HEADS * IDX_DIM
    d = x.shape[1]
    (q,) = _proj(xb, w_p, 0, dq, t128, "rope128", (BF16,), scale=HEAD_DIM ** -0.5)
    k32, kb = _proj(xb, w_p, dq, dkv, t128, "rope128", (F32, BF16))
    v32, vb = _proj(xb, w_p, dq + dkv, dkv, (), "plain", (F32, BF16))
    (qi,) = _proj(xb, w_p, dq + 2 * dkv, dqi, t64, "rope64", (BF16,))
    c0 = dq + 2 * dkv + dqi
    (zr,) = _proj(xb, w_p, c0, 4 * d, (), "plain", (F32,))
    (kw,) = _proj(xb, w_p, c0 + 4 * d, LANES, tkw, "rope64", (F32,))
    return q, k32, kb, v32, vb, qi, zr, kw


def kernel(x_prompt, x_sample, cache_k, cache_v, cache_kidx, state_h, state_conv, page_table, p_prompt, p_sample, w_in, conv_w, conv_b, rg_wa, rg_ba, rg_wx, rg_bx, rg_lambda, w_o_rnn, w_o_att, w_out, ln1_g, ln1_b, w_router, b_router, w_up, b_up, w_down, b_down, ln2_g, ln2_b, w_ple_gate, w_ple_proj):
    bsz, s, d = x_prompt.shape
    n, dec_t, _ = x_sample.shape
    assert bsz == 1 and dec_t == 1 and s % LANES == 0 and n % LANES == 0
    depth = w_in.shape[0]
    npg = page_table.shape[1]
    n_phys, page = cache_k.shape[1], cache_k.shape[2]
    past = npg * page
    alpha = (2 * depth) ** 0.25
    dq, dkv, dqi = N_HEADS * HEAD_DIM, N_KV * HEAD_DIM, IDX_HEADS * IDX_DIM
    c_ki = dq + 2 * dkv + dqi
    c_r = c_ki + IDX_DIM + IDX_HEADS
    t_all = s + n
    n_slots, n_units = _moe_sizes(t_all)
    pos_p = jnp.arange(s, dtype=I32)
    pos_s = jnp.full((n,), past, I32)
    grp = (jnp.arange(N_HEADS) // Q_PER_KV)[:, None] == jnp.arange(N_KV)[None, :]
    row = lambda a: a[None, :]

    hp, hs = x_prompt[0], x_sample[:, 0]
    st_p = [[], [], [], [], []]
    st_s = [[], [], [], [], []]
    for i in range(depth):
        w = w_in[i]
        w_p = jnp.concatenate(
            [w[:, :c_ki], w[:, c_r:], w[:, c_ki:c_r],
             jnp.zeros((d, LANES - IDX_DIM - IDX_HEADS), w.dtype)], axis=1).astype(BF16)
        wa, wx = rg_wa[i].astype(BF16), rg_wx[i].astype(BF16)
        rnn_w = (conv_w[i], row(conv_b[i]), wa, wx, row(rg_ba[i]), row(rg_bx[i]), row(rg_lambda[i]))
        wor, woa, wo = w_o_rnn[i].astype(BF16), w_o_att[i].astype(BF16), w_out[i].astype(BF16)
        wr = jnp.pad(w_router[i], ((0, 0), (0, LANES - N_EXPERTS)))
        br = jnp.pad(b_router[i], (0, LANES - N_EXPERTS))[None, :]
        ln1 = (row(ln1_g[i]), row(ln1_b[i]))

        q, k32, kb, v32, vb, qi, zr, kw = _project_all(hp, w_p, pos_p)
        y_rnn, h_p = _rnn_prompt(zr, s, *rnn_w)
        kit = kw[:, :IDX_DIM].T.astype(BF16)
        y_att = _attn_prompt(qi, kw, q, jnp.concatenate([kit, kit], 0), kb.T, vb, s,
                             min(TOPK_MAX, s // 4))
        m = _merge(y_rnn, y_att, wor, woa, zr)
        bufs = _ln1_router(m, hp, wo, *ln1, wr, br, alpha, 0, t_all)
        st = (k32.reshape(1, s, N_KV, HEAD_DIM), v32.reshape(1, s, N_KV, HEAD_DIM),
              kw[None, :, :IDX_DIM], h_p, zr[None, s - (CONV_W - 1):s, :d])
        for lst, a in zip(st_p, st):
            lst.append(a)

        q, k32, kb, v32, vb, qi, zr, kw = _project_all(hs, w_p, pos_s)
        y_rnn, h_s = _rnn_sample(zr, state_conv[i], state_h[i], *rnn_w)
        sc = _idx_sample(page_table, qi.reshape(n, IDX_HEADS, IDX_DIM),
                         kw[:, IDX_DIM:IDX_DIM + IDX_HEADS, None], kw[:, None, :], cache_kidx[i])
        mask = _mask_sample(sc[:, 0], past + 1, min(TOPK_MAX, (past + 1) // 4))
        qbd = jnp.where(grp[None, :, :, None], q.reshape(n, N_HEADS, 1, HEAD_DIM),
                        jnp.zeros((), BF16)).reshape(n, N_HEADS, dkv)
        y_att = _attn_sample(page_table, qbd, mask[:, None, :], kb[:, None, :], vb[:, None, :],
                             cache_k[i].reshape(n_phys, page, dkv),
                             cache_v[i].reshape(n_phys, page, dkv)).reshape(n, dq)
        m = _merge(y_rnn, y_att, wor, woa, zr)
        x1, sel, gate = _ln1_router(m, hs, wo, *ln1, wr, br, alpha, s, t_all, prev=bufs)
        st = (k32.reshape(n, 1, N_KV, HEAD_DIM), v32.reshape(n, 1, N_KV, HEAD_DIM),
              kw[:, None, :IDX_DIM], h_s,
              jnp.concatenate([state_conv[i][:, 1:], zr[:, None, :d]], axis=1))
        for lst, a in zip(st_s, st):
            lst.append(a)

        rank, cnt = _moe_rank(sel)
        dest4, g4, tab = _moe_dest(sel, gate, rank, cnt)
        units = _moe_units(tab, n_units)
        dest_flat = dest4[:, :TOP_K].reshape(-1)
        xs = _moe_scatter(dest_flat, x1, n_slots)
        ys = _moe_experts(units, xs, w_up[i], b_up[i][:, None, :], w_down[i], b_down[i][:, None, :])
        tail = (w_ple_gate[i].astype(BF16), w_ple_proj[i].astype(BF16), row(ln2_g[i]), row(ln2_b[i]),
                alpha)
        hp = _moe_combine(dest_flat, g4, x1, ys, p_prompt[i][0], *tail, 0, s)
        hs = _moe_combine(dest_flat, g4, x1, ys, p_sample[i][:, 0], *tail, s, n)

    outs_p = [jnp.stack(l) for l in st_p]
    outs_s = [jnp.stack(l) for l in st_s]
    return (hp[None], hs[:, None], *outs_p, *outs_s)
```

```python
import functools

import jax
import jax.numpy as jnp
from jax import lax
from jax.experimental import pallas as pl
from jax.experimental.pallas import tpu as pltpu

F32 = jnp.float32
BF16 = jnp.bfloat16
I32 = jnp.int32

N_HEADS = 16
HEAD_DIM = 128
N_KV = 4
Q_PER_KV = N_HEADS // N_KV
IDX_HEADS = 16
IDX_DIM = 64
TOPK_MAX = 256
Q_BLOCK = 128
ROPE_THETA = 10000.0
RNN_BW = 128
CONV_W = 4
RG_C = 8.0
N_EXPERTS = 32
TOP_K = 4
SWIGLU_LIMIT = 7.0
SWIGLU_ALPHA = 1.702
LN_EPS = 1e-5
LANES = 128
INT_MIN = -(2 ** 31)
NEG_BIG = -1e30
LOG2_E = 1.4426950408889634
VMEM_LIMIT = 56 * 1024 * 1024


def _pick(n, cands):
    for c in cands:
        if n % c == 0:
            return c
    return n


def _cparams(sem):
    return pltpu.CompilerParams(dimension_semantics=sem, vmem_limit_bytes=VMEM_LIMIT)


def _const_spec(shape):
    nd = len(shape)
    return pl.BlockSpec(shape, lambda *a: (0,) * nd, pipeline_mode=pl.Buffered(1))


def _proj_kernel(*refs, mode, n_tab, scale):
    x_ref, w_ref = refs[0], refs[1]
    tabs = refs[2:2 + n_tab]
    outs = refs[2 + n_tab:]
    z = jnp.dot(x_ref[...], w_ref[...], preferred_element_type=F32)
    if mode == "plain":
        for o in outs:
            o[...] = z.astype(o.dtype)
        return
    for h in range(z.shape[1] // LANES):
        zh = z[:, h * LANES:(h + 1) * LANES]
        if mode == "rope128":
            r = zh * tabs[0][...] + pltpu.roll(zh, 64, 1) * tabs[1][...]
        else:
            r = (zh * tabs[0][...] + pltpu.roll(zh, 96, 1) * tabs[1][...]
                 + pltpu.roll(zh, 32, 1) * tabs[2][...])
        if scale != 1.0:
            r = r * scale
        for o in outs:
            o[:, h * LANES:(h + 1) * LANES] = r.astype(o.dtype)


def _proj(x, w, col0, ncols, tabs, mode, out_dtypes, scale=1.0):
    t, k = x.shape
    tm = _pick(t, (1024, 512, 256, 128))
    tn = _pick(ncols, (512, 256, 128))
    c0 = col0 // tn
    in_specs = [pl.BlockSpec((tm, k), lambda i, j: (i, 0)),
                pl.BlockSpec((k, tn), lambda i, j: (0, c0 + j))]
    in_specs += [pl.BlockSpec((tm, LANES), lambda i, j: (i, 0)) for _ in tabs]
    outs = pl.pallas_call(
        functools.partial(_proj_kernel, mode=mode, n_tab=len(tabs), scale=scale),
        out_shape=[jax.ShapeDtypeStruct((t, ncols), d) for d in out_dtypes],
        grid=(t // tm, ncols // tn),
        in_specs=in_specs,
        out_specs=[pl.BlockSpec((tm, tn), lambda i, j: (i, j)) for _ in out_dtypes],
        compiler_params=_cparams(("parallel", "parallel")),
        name="proj_" + mode,
    )(x, w, *tabs)
    return outs


def _rope_tables(pos):
    posf = pos.astype(F32)[:, None]
    h128 = HEAD_DIM // 2
    inv = ROPE_THETA ** (-jnp.arange(h128, dtype=F32) / h128)
    c, s = jnp.cos(posf * inv), jnp.sin(posf * inv)
    t128 = (jnp.concatenate([c, c], 1), jnp.concatenate([-s, s], 1))
    h64 = IDX_DIM // 2
    inv = ROPE_THETA ** (-jnp.arange(h64, dtype=F32) / h64)
    c, s = jnp.cos(posf * inv), jnp.sin(posf * inv)
    z = jnp.zeros_like(s)
    c64 = jnp.concatenate([c, c], 1)
    sa64 = jnp.concatenate([-s, z], 1)
    sb64 = jnp.concatenate([z, s], 1)
    t64 = tuple(jnp.concatenate([a, a], 1) for a in (c64, sa64, sb64))
    idx_scale = (IDX_HEADS * IDX_DIM) ** -0.5
    n = pos.shape[0]
    ck = jnp.concatenate([c64, jnp.full((n, IDX_HEADS), idx_scale, F32),
                          jnp.zeros((n, LANES - IDX_DIM - IDX_HEADS), F32)], 1)
    z64 = jnp.zeros((n, LANES - IDX_DIM), F32)
    tkw = (ck, jnp.concatenate([sa64, z64], 1), jnp.concatenate([sb64, z64], 1))
    return t128, t64, tkw


def _sigmoid(x):
    return 0.5 * jnp.tanh(0.5 * x) + 0.5


def _softplus_neg(lam):
    return jnp.maximum(-lam, 0.0) + jnp.log1p(jnp.exp(-jnp.abs(lam)))


def _gelu_tanh(x):
    return 0.5 * x * (1.0 + jnp.tanh(0.7978845608028654 * (x + 0.044715 * (x * x * x))))


def _rglru_gates(xc, wa_ref, wx_ref, ba, bx, lam):
    nblk = xc.shape[1] // RNN_BW
    rs, gs = [], []
    for j in range(nblk):
        xj = xc[:, j * RNN_BW:(j + 1) * RNN_BW].astype(BF16)
        rs.append(jnp.dot(xj, wa_ref[j], preferred_element_type=F32))
        gs.append(jnp.dot(xj, wx_ref[j], preferred_element_type=F32))
    r = _sigmoid(jnp.concatenate(rs, 1) + ba)
    g = _sigmoid(jnp.concatenate(gs, 1) + bx)
    log_a = -RG_C * r * _softplus_neg(lam)
    a = jnp.exp(log_a)
    mult = jnp.sqrt(jnp.tanh(-log_a) * (1.0 + a * a))
    return a, mult, g * xc


def _rnn_prompt_kernel(xr_ref, gr_ref, cw_ref, cb_ref, wa_ref, wx_ref, ba_ref, bx_ref, lam_ref,
                       y_ref, hl_ref, xbuf, hcar, a_scr, b_scr, h_scr):
    t = pl.program_id(1)
    tm, cw = xr_ref.shape

    @pl.when(t == 0)
    def _():
        xbuf[0:8, :] = jnp.zeros((8, cw), F32)
        hcar[...] = jnp.zeros_like(hcar)

    x = xr_ref[...]
    xbuf[8:8 + tm, :] = x
    w = cw_ref[...]
    xc = (cb_ref[...] + w[3:4] * x + w[2:3] * xbuf[7:7 + tm, :]
          + w[1:2] * xbuf[6:6 + tm, :] + w[0:1] * xbuf[5:5 + tm, :])
    xbuf[0:8, :] = x[tm - 8:tm, :]
    a, mult, gx = _rglru_gates(xc, wa_ref, wx_ref, ba_ref[...], bx_ref[...], lam_ref[...])
    pos = t * tm + lax.broadcasted_iota(I32, (tm, cw), 0)
    mult = jnp.where(pos == 0, 1.0, mult)
    a_scr[...] = a
    b_scr[...] = mult * gx
    row8 = lax.broadcasted_iota(I32, (8, cw), 0)

    def group(g, carry):
        r0 = pl.multiple_of(g * 8, 8)
        av = a_scr[pl.ds(r0, 8), :]
        bv = b_scr[pl.ds(r0, 8), :]
        for d in (1, 2, 4):
            a_s = pltpu.roll(av, d, 0)
            b_s = pltpu.roll(bv, d, 0)
            m = row8 >= d
            bv = jnp.where(m, av * b_s + bv, bv)
            av = jnp.where(m, av * a_s, av)
        h = av * carry + bv
        h_scr[pl.ds(r0, 8), :] = h
        return h[7:8, :]

    carry = lax.fori_loop(0, tm // 8, group, hcar[0:1, :])
    hcar[0:1, :] = carry
    y_ref[...] = (_gelu_tanh(gr_ref[...]) * h_scr[...]).astype(y_ref.dtype)

    @pl.when(t == pl.num_programs(1) - 1)
    def _():
        hl_ref[...] = carry


def _rnn_prompt(zr, s, conv_w, conv_b, wa, wx, ba, bx, lam):
    d = conv_w.shape[1]
    cw = 512
    tm = _pick(s, (256, 128))
    ncb = d // cw
    nb = cw // RNN_BW
    vec = lambda: pl.BlockSpec((1, cw), lambda c, t: (0, c))
    y, hl = pl.pallas_call(
        _rnn_prompt_kernel,
        out_shape=[jax.ShapeDtypeStruct((s, d), BF16), jax.ShapeDtypeStruct((1, d), F32)],
        grid=(ncb, s // tm),
        in_specs=[pl.BlockSpec((tm, cw), lambda c, t: (t, c)),
                  pl.BlockSpec((tm, cw), lambda c, t: (t, ncb + c)),
                  pl.BlockSpec((CONV_W, cw), lambda c, t: (0, c)),
                  vec(),
                  pl.BlockSpec((nb, RNN_BW, RNN_BW), lambda c, t: (c, 0, 0)),
                  pl.BlockSpec((nb, RNN_BW, RNN_BW), lambda c, t: (c, 0, 0)),
                  vec(), vec(), vec()],
        out_specs=[pl.BlockSpec((tm, cw), lambda c, t: (t, c)),
                   pl.BlockSpec((1, cw), lambda c, t: (0, c))],
        scratch_shapes=[pltpu.VMEM((tm + 8, cw), F32), pltpu.VMEM((8, cw), F32),
                        pltpu.VMEM((tm, cw), F32), pltpu.VMEM((tm, cw), F32),
                        pltpu.VMEM((tm, cw), F32)],
        compiler_params=_cparams(("parallel", "arbitrary")),
        name="rnn_prompt",
    )(zr, zr, conv_w, conv_b, wa, wx, ba, bx, lam)
    return y, hl


def _rnn_sample_kernel(xr_ref, gr_ref, c0_ref, c1_ref, c2_ref, h0_ref, cw_ref, cb_ref,
                       wa_ref, wx_ref, ba_ref, bx_ref, lam_ref, y_ref, h_ref):
    w = cw_ref[...]
    xc = (cb_ref[...] + w[3:4] * xr_ref[...] + w[2:3] * c2_ref[...]
          + w[1:2] * c1_ref[...] + w[0:1] * c0_ref[...])
    a, mult, gx = _rglru_gates(xc, wa_ref, wx_ref, ba_ref[...], bx_ref[...], lam_ref[...])
    h = a * h0_ref[...] + mult * gx
    h_ref[...] = h
    y_ref[...] = (_gelu_tanh(gr_ref[...]) * h).astype(y_ref.dtype)


def _rnn_sample(zr, conv_state, h0, conv_w, conv_b, wa, wx, ba, bx, lam):
    n, d = h0.shape
    cw = 512
    ncb = d // cw
    nb = cw // RNN_BW
    blk = lambda off: pl.BlockSpec((n, cw), lambda c: (0, off + c))
    vec = lambda: pl.BlockSpec((1, cw), lambda c: (0, c))
    wsp = lambda: pl.BlockSpec((nb, RNN_BW, RNN_BW), lambda c: (c, 0, 0))
    return pl.pallas_call(
        _rnn_sample_kernel,
        out_shape=[jax.ShapeDtypeStruct((n, d), BF16), jax.ShapeDtypeStruct((n, d), F32)],
        grid=(ncb,),
        in_specs=[blk(0), blk(ncb), blk(0), blk(0), blk(0), blk(0),
                  pl.BlockSpec((CONV_W, cw), lambda c: (0, c)), vec(), wsp(), wsp(),
                  vec(), vec(), vec()],
        out_specs=[blk(0), blk(0)],
        compiler_params=_cparams(("parallel",)),
        name="rnn_sample",
    )(zr, zr, conv_state[:, 0], conv_state[:, 1], conv_state[:, 2], h0,
      conv_w, conv_b, wa, wx, ba, bx, lam)


def _layer_norm(x, g, b):
    mu = jnp.mean(x, axis=-1, keepdims=True)
    xc = x - mu
    var = jnp.mean(xc * xc, axis=-1, keepdims=True)
    return xc * lax.rsqrt(var + LN_EPS) * g + b


def _merge_kernel(yr_ref, ya_ref, wr_ref, wa_ref, ga_ref, gb_ref, o_ref):
    a = jnp.dot(yr_ref[...], wr_ref[...], preferred_element_type=F32)
    b = jnp.dot(ya_ref[...], wa_ref[...], preferred_element_type=F32)
    m = _sigmoid(ga_ref[...]) * a + _sigmoid(gb_ref[...]) * b
    o_ref[...] = m.astype(o_ref.dtype)


def _merge(y_rnn, y_att, w_o_rnn, w_o_att, zr):
    t, d = y_rnn.shape
    tm = _pick(t, (512, 256, 128))
    tn = 512
    nj = d // tn
    return pl.pallas_call(
        _merge_kernel,
        out_shape=jax.ShapeDtypeStruct((t, d), BF16),
        grid=(t // tm, nj),
        in_specs=[pl.BlockSpec((tm, d), lambda i, j: (i, 0)),
                  pl.BlockSpec((tm, d), lambda i, j: (i, 0)),
                  pl.BlockSpec((d, tn), lambda i, j: (0, j)),
                  pl.BlockSpec((d, tn), lambda i, j: (0, j)),
                  pl.BlockSpec((tm, tn), lambda i, j: (i, 2 * nj + j)),
                  pl.BlockSpec((tm, tn), lambda i, j: (i, 3 * nj + j))],
        out_specs=pl.BlockSpec((tm, tn), lambda i, j: (i, j)),
        compiler_params=_cparams(("parallel", "parallel")),
        name="merge",
    )(y_rnn, y_att, w_o_rnn, w_o_att, zr, zr)


def _ln1_router_kernel(*refs, alpha, n_skip):
    m_ref, x_ref, w_ref, g_ref, b_ref, wr_ref, br_ref, x1_ref, sel_ref, gate_ref = refs[n_skip:]
    y = alpha * x_ref[...] + jnp.dot(m_ref[...], w_ref[...], preferred_element_type=F32)
    x1 = _layer_norm(y, g_ref[...], b_ref[...])
    x1_ref[...] = x1
    x_hi = x1.astype(BF16)
    x_lo = (x1 - x_hi.astype(F32)).astype(BF16)
    part = (jnp.dot(x_hi, wr_ref[...], preferred_element_type=F32)
            + jnp.dot(x_lo, wr_ref[...], preferred_element_type=F32))
    logits = part[:, :LANES] + part[:, LANES:] + br_ref[...]
    lane = lax.broadcasted_iota(I32, logits.shape, 1)
    live = lane < N_EXPERTS
    cur = jnp.where(live, logits, -jnp.inf)
    top = jnp.max(cur, axis=1, keepdims=True)
    sel = jnp.zeros(logits.shape, jnp.bool_)
    for _ in range(TOP_K):
        mx = jnp.max(cur, axis=1, keepdims=True)
        first = jnp.min(jnp.where(cur == mx, lane, LANES), axis=1, keepdims=True)
        pick = lane == first
        sel = jnp.logical_or(sel, pick)
        cur = jnp.where(pick, -jnp.inf, cur)
    e = jnp.where(sel, jnp.exp(logits - top), 0.0)
    sel_ref[...] = jnp.where(sel, 1.0, 0.0)
    gate_ref[...] = e / jnp.sum(e, axis=1, keepdims=True)


def _ln1_router(m, x, w_out, g, b, w_router, b_router, alpha, row0, t_all, prev=None):
    t, d = x.shape
    tm = _pick(t, (256, 128))
    r0 = row0 // tm
    in_specs = [pl.BlockSpec((tm, d), lambda i: (i, 0)),
                pl.BlockSpec((tm, d), lambda i: (i, 0)),
                _const_spec((d, d)), _const_spec((1, d)), _const_spec((1, d)),
                _const_spec((d, 2 * LANES)), _const_spec((1, LANES))]
    args = [m, x, w_out, g, b, w_router, b_router]
    aliases = {}
    if prev is not None:
        in_specs = [pl.BlockSpec(memory_space=pl.ANY)] * 3 + in_specs
        args = list(prev) + args
        aliases = {0: 0, 1: 1, 2: 2}

    return pl.pallas_call(
        functools.partial(_ln1_router_kernel, alpha=alpha, n_skip=len(aliases)),
        out_shape=[jax.ShapeDtypeStruct((t_all, d), F32),
                   jax.ShapeDtypeStruct((t_all, LANES), F32),
                   jax.ShapeDtypeStruct((t_all, LANES), F32)],
        grid=(t // tm,),
        in_specs=in_specs,
        out_specs=[pl.BlockSpec((tm, d), lambda i: (r0 + i, 0)),
                   pl.BlockSpec((tm, LANES), lambda i: (r0 + i, 0)),
                   pl.BlockSpec((tm, LANES), lambda i: (r0 + i, 0))],
        input_output_aliases=aliases,
        compiler_params=_cparams(("parallel",)),
        name="ln1_router",
    )(*args)


def _sort_key(x):
    bits = lax.bitcast_convert_type(x, I32)
    return bits ^ (jnp.right_shift(bits, 31) & 0x7FFFFFFF)


def _count(keys_ref, nch, cw, tvec, strict):
    rows = keys_ref.shape[0]
    tb = jnp.broadcast_to(tvec, (rows, LANES))

    def body(c, cnt):
        off = c * cw
        for s in range(cw // LANES):
            k = keys_ref[:, pl.ds(pl.multiple_of(off + s * LANES, LANES), LANES)]
            hit = (k > tb) if strict else (k >= tb)
            cnt = cnt + jnp.where(hit, 1.0, 0.0)
        return cnt

    cnt = lax.fori_loop(0, nch, body, jnp.zeros((rows, LANES), F32))
    return jnp.sum(cnt, axis=1, keepdims=True)


def _select_threshold(keys_ref, nch, cw, n_sel, few):
    rows = keys_ref.shape[0]
    settled = few > 0.5

    def cond(st):
        b, _, cnt_cur = st
        open_rows = jnp.where(jnp.logical_or(settled, cnt_cur == n_sel), 0.0, 1.0)
        return jnp.logical_and(b < 32, jnp.max(open_rows) > 0.0)

    def bit_step(st):
        b, cur, cnt_cur = st
        cand = cur | lax.shift_left(jnp.int32(1), 31 - b)
        cnt = _count(keys_ref, nch, cw, cand ^ INT_MIN, False)
        take = cnt >= n_sel
        return b + 1, jnp.where(take, cand, cur), jnp.where(take, cnt, cnt_cur)

    everything = jnp.zeros((rows, 1), F32) + jnp.asarray(nch * cw, F32)
    _, cur, n_ge = lax.while_loop(cond, bit_step, (jnp.int32(0), jnp.zeros((rows, 1), I32), everything))
    t = cur ^ INT_MIN
    tied = jnp.where(jnp.logical_or(settled, n_ge <= n_sel), 0.0, 1.0)

    @pl.when(jnp.max(tied) > 0.0)
    def _():
        n_gt = _count(keys_ref, nch, cw, t, True)
        tb = jnp.broadcast_to(t, (rows, LANES))
        needb = jnp.broadcast_to(jnp.where(tied > 0.5, n_sel - n_gt, 1e9), (rows, LANES))
        r = lax.broadcasted_iota(I32, (LANES, LANES), 0)
        c = lax.broadcasted_iota(I32, (LANES, LANES), 1)
        upper = jnp.where(r < c, 1.0, 0.0).astype(BF16)

        def fix(j, run):
            sl = pl.ds(pl.multiple_of(j * LANES, LANES), LANES)
            k = keys_ref[:, sl]
            eq = k == tb
            eqf = jnp.where(eq, 1.0, 0.0)
            before = jnp.dot(eqf.astype(BF16), upper, preferred_element_type=F32) + run
            drop = jnp.logical_and(eq, before >= needb)
            keys_ref[:, sl] = jnp.where(drop, INT_MIN, k)
            return run + jnp.sum(eqf, axis=1, keepdims=True)

        lax.fori_loop(0, nch * (cw // LANES), fix, jnp.zeros((rows, 1), F32))

    return jnp.maximum(t, INT_MIN + 1)


def _attn_prompt_kernel(qi_ref, kw_ref, q_ref, kit_ref, kt_ref, vx_ref, o_ref,
                        keys_scr, lhs_scr, wb_scr, tb_scr, qg_scr, s_scr, p_scr, m_scr, acc_scr,
                        *, n_sel, kc):
    i = pl.program_id(0)
    qb = Q_BLOCK
    nch = (i * qb + qb + kc - 1) // kc
    kw = kw_ref[...]
    lane = lax.broadcasted_iota(I32, (qb, LANES), 1)
    for h in range(IDX_HEADS):
        wb_scr[h] = jnp.broadcast_to(kw[:, IDX_DIM + h:IDX_DIM + h + 1], (qb, LANES))
        blk = qi_ref[:, (h // 2) * LANES:(h // 2 + 1) * LANES]
        keep = (lane < IDX_DIM) if h % 2 == 0 else (lane >= IDX_DIM)
        lhs_scr[h] = jnp.where(keep, blk, jnp.zeros_like(blk))

    sw = min(256, kc)

    def score_chunk(c, carry):
        for s in range(kc // sw):
            o2 = pl.multiple_of(c * kc + s * sw, sw)
            kt = kit_ref[:, pl.ds(o2, sw)]
            acc = jnp.zeros((qb, sw), F32)
            for h in range(IDX_HEADS):
                sc = jnp.dot(lhs_scr[h], kt, preferred_element_type=F32)
                wb = wb_scr[h]
                acc = acc + jnp.maximum(sc, 0.0) * jnp.concatenate([wb] * (sw // LANES), axis=1)
            kpos = o2 + lax.broadcasted_iota(I32, (qb, sw), 1)
            qpos = i * qb + lax.broadcasted_iota(I32, (qb, sw), 0)
            keys_scr[:, pl.ds(o2, sw)] = jnp.where(kpos <= qpos, _sort_key(acc), INT_MIN)
        return carry

    lax.fori_loop(0, nch, score_chunk, 0)
    n_causal = i * qb + lax.broadcasted_iota(I32, (qb, 1), 0) + 1
    t = _select_threshold(keys_scr, nch, kc, n_sel, jnp.where(n_causal <= n_sel, 1.0, 0.0))
    rows = Q_PER_KV * qb
    strip = 32
    nrep = kc // LANES
    tb_scr[...] = jnp.broadcast_to(t, (qb, LANES))

    for g in range(N_KV):
        for j in range(Q_PER_KV):
            h = g * Q_PER_KV + j
            qg_scr[g, j * qb:(j + 1) * qb, :] = q_ref[:, h * HEAD_DIM:(h + 1) * HEAD_DIM]
    m_scr[...] = jnp.full(m_scr.shape, NEG_BIG, F32)
    acc_scr[...] = jnp.zeros_like(acc_scr)

    def body(c, carry):
        off = pl.multiple_of(c * kc, kc)
        for g in range(N_KV):
            s_scr[g] = jnp.dot(qg_scr[g], kt_ref[g * HEAD_DIM:(g + 1) * HEAD_DIM, pl.ds(off, kc)],
                               preferred_element_type=F32)
            for rq in range(0, qb, strip):
                tb = jnp.concatenate([tb_scr[rq:rq + strip, :]] * nrep, axis=1)
                for j in range(Q_PER_KV):
                    r = j * qb + rq
                    km = keys_scr[rq:rq + strip, pl.ds(off, kc)] >= tb
                    s = jnp.where(km, s_scr[g, r:r + strip, :], NEG_BIG)
                    s_scr[g, r:r + strip, :] = s
                    m_old = m_scr[g, r:r + strip, :]
                    m_new = jnp.maximum(m_old, jnp.max(s, axis=1, keepdims=True))
                    alpha = jnp.exp2(m_old - m_new)
                    m_scr[g, r:r + strip, :] = m_new
                    acc_scr[g, r:r + strip, :] = (acc_scr[g, r:r + strip, :]
                                                  * jnp.concatenate([alpha] * 2, axis=1))
            for r in range(0, rows, strip):
                m_new = m_scr[g, r:r + strip, :]
                p = jnp.exp2(s_scr[g, r:r + strip, :] - jnp.concatenate([m_new] * nrep, axis=1))
                p_scr[g, r:r + strip, :] = p.astype(BF16)
            acc_scr[g] += jnp.dot(p_scr[g], vx_ref[pl.ds(off, kc), g * 2 * HEAD_DIM:(g + 1) * 2 * HEAD_DIM],
                                  preferred_element_type=F32)
        return carry

    lax.fori_loop(0, nch, body, 0)
    for g in range(N_KV):
        acc = acc_scr[g]
        out = acc[:, :HEAD_DIM] / acc[:, HEAD_DIM:HEAD_DIM + 1]
        for j in range(Q_PER_KV):
            h = g * Q_PER_KV + j
            o_ref[:, h * HEAD_DIM:(h + 1) * HEAD_DIM] = out[j * qb:(j + 1) * qb].astype(o_ref.dtype)


def _attn_prompt(qi, kw, q, kit2, kt, vx, s, n_sel):
    kc = min(512, s)
    d = N_HEADS * HEAD_DIM
    rows = Q_PER_KV * Q_BLOCK
    return pl.pallas_call(
        functools.partial(_attn_prompt_kernel, n_sel=n_sel, kc=kc),
        out_shape=jax.ShapeDtypeStruct((s, d), BF16),
        grid=(s // Q_BLOCK,),
        in_specs=[pl.BlockSpec((Q_BLOCK, IDX_HEADS * IDX_DIM), lambda i: (i, 0)),
                  pl.BlockSpec((Q_BLOCK, LANES), lambda i: (i, 0)),
                  pl.BlockSpec((Q_BLOCK, d), lambda i: (i, 0)),
                  _const_spec((LANES, s)), _const_spec((N_KV * HEAD_DIM, s)),
                  _const_spec((s, N_KV * 2 * HEAD_DIM))],
        out_specs=pl.BlockSpec((Q_BLOCK, d), lambda i: (i, 0)),
        scratch_shapes=[pltpu.VMEM((Q_BLOCK, s), I32),
                        pltpu.VMEM((IDX_HEADS, Q_BLOCK, LANES), BF16),
                        pltpu.VMEM((IDX_HEADS, Q_BLOCK, LANES), F32),
                        pltpu.VMEM((Q_BLOCK, LANES), I32),
                        pltpu.VMEM((N_KV, rows, HEAD_DIM), BF16),
                        pltpu.VMEM((N_KV, rows, kc), F32),
                        pltpu.VMEM((N_KV, rows, kc), BF16),
                        pltpu.VMEM((N_KV, rows, LANES), F32),
                        pltpu.VMEM((N_KV, rows, 2 * HEAD_DIM), F32)],
        compiler_params=_cparams(("parallel",)),
        name="attn_prompt",
    )(qi, kw, q, kit2, kt, vx)


def _idx_sample_kernel(pt_ref, qi_ref, w_ref, kn_ref, *refs, npg, past):
    pages, o_ref = refs[:npg], refs[npg]
    qi = qi_ref[...]
    kp = jnp.concatenate([p[...] for p in pages], axis=0).astype(BF16)
    sc = lax.dot_general(qi, kp, (((1,), (1,)), ((), ())), preferred_element_type=F32)
    w = w_ref[...]
    o_ref[:, 0:past] = jnp.sum(jnp.maximum(sc, 0.0) * w, axis=0, keepdims=True)
    kn = kn_ref[...][:, :IDX_DIM].astype(BF16).astype(F32)
    sn = jnp.sum(qi.astype(F32) * kn, axis=1, keepdims=True)
    new = jnp.sum(jnp.maximum(sn, 0.0) * w, axis=0, keepdims=True)
    lane = lax.broadcasted_iota(I32, (1, LANES), 1)
    o_ref[:, past:past + LANES] = jnp.where(lane == 0, new, -jnp.inf)


def _idx_sample(page_table, qi3, w3, kn3, cache_kidx):
    n, npg = page_table.shape
    page = cache_kidx.shape[1]
    past = npg * page
    page_specs = [pl.BlockSpec((None, page, IDX_DIM), lambda b, pt, p=p: (pt[b * npg + p], 0, 0))
                  for p in range(npg)]
    return pl.pallas_call(
        functools.partial(_idx_sample_kernel, npg=npg, past=past),
        out_shape=jax.ShapeDtypeStruct((n, 1, past + LANES), F32),
        grid_spec=pltpu.PrefetchScalarGridSpec(
            num_scalar_prefetch=1, grid=(n,),
            in_specs=[pl.BlockSpec((None, IDX_HEADS, IDX_DIM), lambda b, pt: (b, 0, 0)),
                      pl.BlockSpec((None, IDX_HEADS, 1), lambda b, pt: (b, 0, 0)),
                      pl.BlockSpec((None, 1, LANES), lambda b, pt: (b, 0, 0))] + page_specs,
            out_specs=pl.BlockSpec((None, 1, past + LANES), lambda b, pt: (b, 0, 0))),
        compiler_params=_cparams(("parallel",)),
        name="idx_sample",
    )(page_table.reshape(-1), qi3, w3, kn3, *([cache_kidx] * npg))


def _mask_sample_kernel(sc_ref, m_ref, keys_scr, *, n_valid, n_sel):
    rows, width = sc_ref.shape
    pos = lax.broadcasted_iota(I32, (rows, width), 1)
    keys_scr[...] = jnp.where(pos < n_valid, _sort_key(sc_ref[...]), INT_MIN)
    few = jnp.full((rows, 1), 1.0 if n_valid <= n_sel else 0.0, F32)
    t = _select_threshold(keys_scr, width // LANES, LANES, n_sel, few)
    m_ref[...] = jnp.where(keys_scr[...] >= t, 1.0, 0.0)


def _mask_sample(sc, n_valid, n_sel):
    n, width = sc.shape
    return pl.pallas_call(
        functools.partial(_mask_sample_kernel, n_valid=n_valid, n_sel=n_sel),
        out_shape=jax.ShapeDtypeStruct((n, width), F32),
        scratch_shapes=[pltpu.VMEM((n, width), I32)],
        name="mask_sample",
    )(sc)


def _attn_sample_kernel(pt_ref, q_ref, m4_ref, mn_ref, kn_ref, vn_ref, *refs, npg):
    kpages, vpages, o_ref = refs[:npg], refs[npg:2 * npg], refs[2 * npg]
    q = q_ref[...]
    kp = jnp.concatenate([p[...] for p in kpages], axis=0).astype(BF16)
    s = lax.dot_general(q, kp, (((1,), (1,)), ((), ())), preferred_element_type=F32)
    grp = lax.broadcasted_iota(I32, s.shape, 0) // Q_PER_KV
    own = (lax.broadcasted_iota(I32, s.shape, 1) & (N_KV - 1)) == grp
    s = jnp.where(own, jnp.where(m4_ref[...] > 0.5, s, NEG_BIG), NEG_BIG)
    grp_h = lax.broadcasted_iota(I32, (N_HEADS, HEAD_DIM), 0) // Q_PER_KV
    kn = kn_ref[...].astype(F32)
    vn = vn_ref[...].astype(F32)
    kn_h = jnp.zeros((N_HEADS, HEAD_DIM), F32)
    vn_h = jnp.zeros((N_HEADS, HEAD_DIM), F32)
    for g in range(N_KV):
        kn_h = jnp.where(grp_h == g, kn[g:g + 1, :], kn_h)
        vn_h = jnp.where(grp_h == g, vn[g:g + 1, :], vn_h)
    sn = jnp.sum(q.astype(F32) * kn_h, axis=1, keepdims=True)
    sn = jnp.where(mn_ref[:, 0:1] > 0.5, sn, NEG_BIG)
    m = jnp.maximum(jnp.max(s, axis=1, keepdims=True), sn)
    p = jnp.exp2(s - m)
    pn = jnp.exp2(sn - m)
    l = jnp.sum(p, axis=1, keepdims=True) + pn
    vp = jnp.concatenate([r[...] for r in vpages], axis=0).astype(BF16)
    o = jnp.dot(p.astype(BF16), vp, preferred_element_type=F32)
    o_ref[...] = ((o + pn.astype(BF16).astype(F32) * vn_h) / l).astype(o_ref.dtype)


def _attn_sample(page_table, q3, mask4, mask_new, kn3, vn3, cache_k, cache_v):
    n, npg = page_table.shape
    prow = cache_k.shape[1]
    pspec = lambda p: pl.BlockSpec((None, prow, HEAD_DIM), lambda b, pt, p=p: (pt[b * npg + p], 0, 0))
    return pl.pallas_call(
        functools.partial(_attn_sample_kernel, npg=npg),
        out_shape=jax.ShapeDtypeStruct((n, N_HEADS, HEAD_DIM), BF16),
        grid_spec=pltpu.PrefetchScalarGridSpec(
            num_scalar_prefetch=1, grid=(n,),
            in_specs=[pl.BlockSpec((None, N_HEADS, HEAD_DIM), lambda b, pt: (b, 0, 0)),
                      pl.BlockSpec((None, 1, npg * prow), lambda b, pt: (b, 0, 0)),
                      pl.BlockSpec((None, 1, LANES), lambda b, pt: (b, 0, 0)),
                      pl.BlockSpec((None, N_KV, HEAD_DIM), lambda b, pt: (b, 0, 0)),
                      pl.BlockSpec((None, N_KV, HEAD_DIM), lambda b, pt: (b, 0, 0))]
            + [pspec(p) for p in range(npg)] + [pspec(p) for p in range(npg)],
            out_specs=pl.BlockSpec((None, N_HEADS, HEAD_DIM), lambda b, pt: (b, 0, 0))),
        compiler_params=_cparams(("parallel",)),
        name="attn_sample",
    )(page_table.reshape(-1), q3, mask4, mask_new, kn3, vn3, *([cache_k] * npg), *([cache_v] * npg))


MOE_RB = 256
MOE_RC = 1280
MOE_TF = 512
MOE_TN = 512


def _moe_sizes(n_tok):
    n_assign = n_tok * TOP_K
    n_slots = (n_assign // MOE_RB + N_EXPERTS) * MOE_RB
    n_units = N_EXPERTS + -(-n_slots // MOE_RC)
    return n_slots, n_units


def _moe_rank_kernel(sel_ref, rank_ref, cnt_ref, carry):
    i = pl.program_id(0)
    tp = sel_ref.shape[0]

    @pl.when(i == 0)
    def _():
        carry[...] = jnp.zeros_like(carry)

    a = sel_ref[...]
    r = lax.broadcasted_iota(I32, (tp, tp), 0)
    c = lax.broadcasted_iota(I32, (tp, tp), 1)
    lower = jnp.where(c < r, 1.0, 0.0).astype(BF16)
    rank_ref[...] = jnp.dot(lower, a.astype(BF16), preferred_element_type=F32) + carry[0:1, :]
    carry[...] = carry[...] + jnp.sum(a, axis=0, keepdims=True)
    cnt_ref[...] = carry[...]


def _moe_rank(sel):
    t = sel.shape[0]
    tp = _pick(t, (256, 128))
    return pl.pallas_call(
        _moe_rank_kernel,
        out_shape=[jax.ShapeDtypeStruct((t, LANES), F32), jax.ShapeDtypeStruct((8, LANES), F32)],
        grid=(t // tp,),
        in_specs=[pl.BlockSpec((tp, LANES), lambda i: (i, 0))],
        out_specs=[pl.BlockSpec((tp, LANES), lambda i: (i, 0)),
                   pl.BlockSpec((8, LANES), lambda i: (0, 0))],
        scratch_shapes=[pltpu.VMEM((8, LANES), F32)],
        compiler_params=_cparams(("arbitrary",)),
        name="moe_rank",
    )(sel)


def _moe_dest_kernel(sel_ref, gate_ref, rank_ref, cnt_ref, dest_ref, g4_ref, tab_ref):
    cnt = cnt_ref[...]
    lane8 = lax.broadcasted_iota(I32, cnt.shape, 1)
    padded = jnp.ceil(cnt * (1.0 / MOE_RB)) * MOE_RB
    incl = padded
    for d in (1, 2, 4, 8, 16, 32, 64):
        incl = incl + jnp.where(lane8 >= d, pltpu.roll(incl, d, 1), 0.0)
    start = incl - padded
    row8 = lax.broadcasted_iota(I32, cnt.shape, 0)
    tab_ref[...] = jnp.where(row8 == 0, start, jnp.where(row8 == 1, padded, 0.0)).astype(I32)
    dest = start[0:1, :] + rank_ref[...]
    gate = gate_ref[...]
    cur = sel_ref[...]
    lane = lax.broadcasted_iota(I32, cur.shape, 1)
    d4 = jnp.zeros(cur.shape, F32)
    g4 = jnp.zeros(cur.shape, F32)
    for k in range(TOP_K):
        first = jnp.min(jnp.where(cur > 0.5, lane, LANES), axis=1, keepdims=True)
        pick = lane == first
        dk = jnp.sum(jnp.where(pick, dest, 0.0), axis=1, keepdims=True)
        gk = jnp.sum(jnp.where(pick, gate, 0.0), axis=1, keepdims=True)
        d4 = jnp.where(lane == k, dk, d4)
        g4 = jnp.where(lane == k, gk, g4)
        cur = jnp.where(pick, 0.0, cur)
    dest_ref[...] = d4.astype(I32)
    g4_ref[...] = g4


def _moe_dest(sel, gate, rank, cnt):
    t = sel.shape[0]
    tp = _pick(t, (256, 128))
    row = lambda: pl.BlockSpec((tp, LANES), lambda i: (i, 0))
    one = lambda: pl.BlockSpec((8, LANES), lambda i: (0, 0))
    return pl.pallas_call(
        _moe_dest_kernel,
        out_shape=[jax.ShapeDtypeStruct((t, LANES), I32), jax.ShapeDtypeStruct((t, LANES), F32),
                   jax.ShapeDtypeStruct((8, LANES), I32)],
        grid=(t // tp,),
        in_specs=[row(), row(), row(), one()],
        out_specs=[row(), row(), one()],
        compiler_params=_cparams(("arbitrary",)),
        name="moe_dest",
    )(sel, gate, rank, cnt)


def _moe_units_kernel(tab_ref, unit_ref, *, n_units):
    rcb = MOE_RC // MOE_RB

    def per_expert(e, state):
        u0, _ = state
        nb = tab_ref[1, e] // MOE_RB
        b0 = tab_ref[0, e] // MOE_RB
        nu = (nb + rcb - 1) // rcb

        def per_unit(j, carry):
            unit_ref[0, u0 + j] = e
            unit_ref[1, u0 + j] = b0 + j * rcb
            unit_ref[2, u0 + j] = jnp.minimum(nb - j * rcb, rcb)
            return carry

        lax.fori_loop(0, nu, per_unit, 0)
        return u0 + nu, jnp.where(nu > 0, e, state[1])

    used, last = lax.fori_loop(0, N_EXPERTS, per_expert, (jnp.int32(0), jnp.int32(0)))

    def fill(u, carry):
        unit_ref[0, u] = last
        unit_ref[1, u] = 0
        unit_ref[2, u] = 0
        return carry

    lax.fori_loop(used, n_units, fill, 0)


def _moe_units(tab, n_units):
    return pl.pallas_call(
        functools.partial(_moe_units_kernel, n_units=n_units),
        out_shape=jax.ShapeDtypeStruct((3, n_units), I32),
        in_specs=[pl.BlockSpec(memory_space=pltpu.SMEM)],
        out_specs=pl.BlockSpec(memory_space=pltpu.SMEM),
        name="moe_units",
    )(tab)


def _moe_scatter_kernel(dest_ref, x_ref, xs_ref, sem):
    tm = x_ref.shape[0]

    def issue(r, carry):
        for k in range(TOP_K):
            pltpu.make_async_copy(x_ref.at[pl.ds(r, 1)], xs_ref.at[pl.ds(dest_ref[r * TOP_K + k], 1)],
                                  sem).start()
        return carry

    lax.fori_loop(0, tm, issue, 0)
    for _ in range(TOP_K):
        pltpu.make_async_copy(x_ref, xs_ref.at[pl.ds(0, tm)], sem).wait()


def _moe_scatter(dest_flat, x1, n_slots):
    t, d = x1.shape
    tm = _pick(t, (256, 128))
    return pl.pallas_call(
        _moe_scatter_kernel,
        out_shape=jax.ShapeDtypeStruct((n_slots, d), F32),
        grid=(t // tm,),
        in_specs=[pl.BlockSpec((tm * TOP_K,), lambda i: (i,), memory_space=pltpu.SMEM),
                  pl.BlockSpec((tm, d), lambda i: (i, 0))],
        out_specs=pl.BlockSpec(memory_space=pl.ANY),
        scratch_shapes=[pltpu.SemaphoreType.DMA(())],
        compiler_params=_cparams(("arbitrary",)),
        name="moe_scatter",
    )(dest_flat, x1)


def _moe_expert_kernel(unit_ref, xs_ref, wg_ref, wl_ref, bg_ref, bl_ref, wd_ref, bd_ref, ys_ref,
                       x_scr, act_scr, xst, yst, wg_bf, wl_bf, wd_bf, xsem, ysem, *, ju):
    u = pl.program_id(0)
    j = pl.program_id(1)
    b0 = unit_ref[1, u]
    ns = unit_ref[2, u]
    rb = MOE_RB

    def x_copy(s, slot):
        return pltpu.make_async_copy(xs_ref.at[pl.ds((b0 + s) * rb, rb)], xst.at[slot], xsem.at[slot])

    def up_block(s):
        xb = x_scr[pl.ds(pl.multiple_of(s * rb, rb), rb), :]
        hg = jnp.dot(xb, wg_bf[...], preferred_element_type=F32) + bg_ref[...]
        hl = jnp.dot(xb, wl_bf[...], preferred_element_type=F32) + bl_ref[...]
        glu = jnp.minimum(hg, SWIGLU_LIMIT)
        lin = jnp.clip(hl, -SWIGLU_LIMIT, SWIGLU_LIMIT)
        act = glu * _sigmoid(SWIGLU_ALPHA * glu) * (lin + 1.0)
        act_scr[pl.ds(pl.multiple_of(s * rb, rb), rb),
                pl.ds(pl.multiple_of(j * MOE_TF, MOE_TF), MOE_TF)] = act.astype(BF16)

    @pl.when(jnp.logical_and(ns > 0, j < ju))
    def _():
        wg_bf[...] = wg_ref[...].astype(BF16)
        wl_bf[...] = wl_ref[...].astype(BF16)

    @pl.when(jnp.logical_and(ns > 0, j == 0))
    def _():
        x_copy(0, 0).start()

        def body(s, carry):
            slot = s % 2

            @pl.when(s + 1 < ns)
            def _():
                x_copy(s + 1, 1 - slot).start()

            x_copy(s, slot).wait()
            x_scr[pl.ds(pl.multiple_of(s * rb, rb), rb), :] = xst[slot].astype(BF16)
            up_block(s)
            return carry

        lax.fori_loop(0, ns, body, 0)

    @pl.when(jnp.logical_and(ns > 0, jnp.logical_and(j > 0, j < ju)))
    def _():
        def body(s, carry):
            up_block(s)
            return carry

        lax.fori_loop(0, ns, body, 0)

    @pl.when(jnp.logical_and(ns > 0, j >= ju))
    def _():
        wd_bf[...] = wd_ref[...].astype(BF16)
        col = pl.multiple_of((j - ju) * MOE_TN, MOE_TN)

        def y_copy(s, slot):
            return pltpu.make_async_copy(
                yst.at[slot], ys_ref.at[pl.ds((b0 + s) * rb, rb), pl.ds(col, MOE_TN)], ysem.at[slot])

        def body(s, carry):
            slot = s % 2

            @pl.when(s >= 2)
            def _():
                y_copy(s - 2, slot).wait()

            ab = act_scr[pl.ds(pl.multiple_of(s * rb, rb), rb), :]
            yst[slot] = jnp.dot(ab, wd_bf[...], preferred_element_type=F32) + bd_ref[...]
            y_copy(s, slot).start()
            return carry

        lax.fori_loop(0, ns, body, 0)

        @pl.when(ns >= 2)
        def _():
            y_copy(ns - 2, ns % 2).wait()

        y_copy(ns - 1, (ns - 1) % 2).wait()


def _moe_experts(units, xs, w_up, b_up, w_down, b_down):
    n_slots, d = xs.shape
    n_units = units.shape[1]
    dff = w_down.shape[1]
    ju, jd = dff // MOE_TF, d // MOE_TN

    def up_idx(off):
        return lambda u, j, un: (un[0, u], 0, off + jnp.where(un[2, u] > 0, jnp.minimum(j, ju - 1), ju - 1))

    def dn_idx(u, j, un):
        return (un[0, u], 0, jnp.where(un[2, u] > 0, jnp.maximum(j - ju, 0), jd - 1))

    return pl.pallas_call(
        functools.partial(_moe_expert_kernel, ju=ju),
        out_shape=jax.ShapeDtypeStruct((n_slots, d), F32),
        grid_spec=pltpu.PrefetchScalarGridSpec(
            num_scalar_prefetch=1, grid=(n_units, ju + jd),
            in_specs=[pl.BlockSpec(memory_space=pl.ANY),
                      pl.BlockSpec((None, d, MOE_TF), up_idx(0)),
                      pl.BlockSpec((None, d, MOE_TF), up_idx(ju)),
                      pl.BlockSpec((None, 1, MOE_TF), up_idx(0)),
                      pl.BlockSpec((None, 1, MOE_TF), up_idx(ju)),
                      pl.BlockSpec((None, dff, MOE_TN), dn_idx),
                      pl.BlockSpec((None, 1, MOE_TN), dn_idx)],
            out_specs=pl.BlockSpec(memory_space=pl.ANY),
            scratch_shapes=[pltpu.VMEM((MOE_RC, d), BF16), pltpu.VMEM((MOE_RC, dff), BF16),
                            pltpu.VMEM((2, MOE_RB, d), F32), pltpu.VMEM((2, MOE_RB, MOE_TN), F32),
                            pltpu.VMEM((d, MOE_TF), BF16), pltpu.VMEM((d, MOE_TF), BF16),
                            pltpu.VMEM((dff, MOE_TN), BF16),
                            pltpu.SemaphoreType.DMA((2,)), pltpu.SemaphoreType.DMA((2,))]),
        compiler_params=_cparams(("arbitrary", "arbitrary")),
        name="moe_experts",
    )(units, xs, w_up, w_up, b_up, b_up, w_down, b_down)


def _moe_combine_kernel(dest_ref, g4_ref, x1_ref, ys_ref, p_ref, wg_ref, wp_ref, g_ref, b_ref,
                        y_ref, gbuf, sem, *, alpha):
    tm = x1_ref.shape[0]

    def issue(r, carry):
        for k in range(TOP_K):
            pltpu.make_async_copy(ys_ref.at[pl.ds(dest_ref[r * TOP_K + k], 1)],
                                  gbuf.at[k, pl.ds(r, 1)], sem).start()
        return carry

    lax.fori_loop(0, tm, issue, 0)
    for k in range(TOP_K):
        pltpu.make_async_copy(ys_ref.at[pl.ds(0, tm)], gbuf.at[k], sem).wait()
    g4 = g4_ref[...]
    moe = g4[:, 0:1] * gbuf[0]
    for k in range(1, TOP_K):
        moe = moe + g4[:, k:k + 1] * gbuf[k]
    x2 = _layer_norm(alpha * x1_ref[...] + moe, g_ref[...], b_ref[...])
    gate = _sigmoid(jnp.dot(x2.astype(BF16), wg_ref[...], preferred_element_type=F32))
    y_ref[...] = x2 + gate * jnp.dot(p_ref[...].astype(BF16), wp_ref[...], preferred_element_type=F32)


def _moe_combine(dest_flat, g4, x1, ys, p, w_gate, w_proj, g, b, alpha, row0, t):
    d = x1.shape[1]
    tm = _pick(t, (128,))
    r0 = row0 // tm
    pd = p.shape[1]
    return pl.pallas_call(
        functools.partial(_moe_combine_kernel, alpha=alpha),
        out_shape=jax.ShapeDtypeStruct((t, d), F32),
        grid=(t // tm,),
        in_specs=[pl.BlockSpec((tm * TOP_K,), lambda i: (r0 + i,), memory_space=pltpu.SMEM),
                  pl.BlockSpec((tm, LANES), lambda i: (r0 + i, 0)),
                  pl.BlockSpec((tm, d), lambda i: (r0 + i, 0)),
                  pl.BlockSpec(memory_space=pl.ANY),
                  pl.BlockSpec((tm, pd), lambda i: (i, 0)),
                  _const_spec((d, d)), _const_spec((pd, d)), _const_spec((1, d)), _const_spec((1, d))],
        out_specs=pl.BlockSpec((tm, d), lambda i: (i, 0)),
        scratch_shapes=[pltpu.VMEM((TOP_K, tm, d), F32), pltpu.SemaphoreType.DMA(())],
        compiler_params=_cparams(("arbitrary",)),
        name="moe_combine",
    )(dest_flat, g4, x1, ys, p, w_gate, w_proj, g, b)


def _project_all(x, w_p, pos):
    t128, t64, tkw = _rope_tables(pos)
    xb = x.astype(BF16)
    dq = N_HEADS * HEAD_DIM
    dkv = N_KV * HEAD_DIM
    dqi = IDX_HEADS * IDX_DIM
    d = x.shape[1]
    (q,) = _proj(xb, w_p, 0, dq, t128, "rope128", (BF16,), scale=LOG2_E * HEAD_DIM ** -0.5)
    k32, kb = _proj(xb, w_p, dq, dkv, t128, "rope128", (F32, BF16))
    v32, vb = _proj(xb, w_p, dq + dkv, dkv, (), "plain", (F32, BF16))
    (qi,) = _proj(xb, w_p, dq + 2 * dkv, dqi, t64, "rope64", (BF16,))
    c0 = dq + 2 * dkv + dqi
    (zr,) = _proj(xb, w_p, c0, 4 * d, (), "plain", (F32,))
    (kw,) = _proj(xb, w_p, c0 + 4 * d, LANES, tkw, "rope64", (F32,))
    return q, k32, kb, v32, vb, qi, zr, kw


def kernel(x_prompt, x_sample, cache_k, cache_v, cache_kidx, state_h, state_conv, page_table, p_prompt, p_sample, w_in, conv_w, conv_b, rg_wa, rg_ba, rg_wx, rg_bx, rg_lambda, w_o_rnn, w_o_att, w_out, ln1_g, ln1_b, w_router, b_router, w_up, b_up, w_down, b_down, ln2_g, ln2_b, w_ple_gate, w_ple_proj):
    bsz, s, d = x_prompt.shape
    n, dec_t, _ = x_sample.shape
    assert bsz == 1 and dec_t == 1 and s % LANES == 0 and n % LANES == 0
    depth = w_in.shape[0]
    npg = page_table.shape[1]
    n_phys, page = cache_k.shape[1], cache_k.shape[2]
    past = npg * page
    alpha = (2 * depth) ** 0.25
    dq, dkv, dqi = N_HEADS * HEAD_DIM, N_KV * HEAD_DIM, IDX_HEADS * IDX_DIM
    c_ki = dq + 2 * dkv + dqi
    c_r = c_ki + IDX_DIM + IDX_HEADS
    t_all = s + n
    n_slots, n_units = _moe_sizes(t_all)
    pos_p = jnp.arange(s, dtype=I32)
    pos_s = jnp.full((n,), past, I32)
    row = lambda a: a[None, :]

    hp, hs = x_prompt[0], x_sample[:, 0]
    st_p = [[], [], [], [], []]
    st_s = [[], [], [], [], []]
    for i in range(depth):
        w = w_in[i]
        w_p = jnp.concatenate(
            [w[:, :c_ki], w[:, c_r:], w[:, c_ki:c_r],
             jnp.zeros((d, LANES - IDX_DIM - IDX_HEADS), w.dtype)], axis=1).astype(BF16)
        wa, wx = rg_wa[i].astype(BF16), rg_wx[i].astype(BF16)
        rnn_w = (conv_w[i], row(conv_b[i]), wa, wx, row(rg_ba[i]), row(rg_bx[i]), row(rg_lambda[i]))
        wor, woa, wo = w_o_rnn[i].astype(BF16), w_o_att[i].astype(BF16), w_out[i].astype(BF16)
        wr = jnp.pad(w_router[i], ((0, 0), (0, LANES - N_EXPERTS)))
        wr_hi = wr.astype(BF16)
        wr = jnp.concatenate([wr_hi, (wr - wr_hi.astype(F32)).astype(BF16)], axis=1)
        br = jnp.pad(b_router[i], (0, LANES - N_EXPERTS))[None, :]
        ln1 = (row(ln1_g[i]), row(ln1_b[i]))

        q, k32, kb, v32, vb, qi, zr, kw = _project_all(hp, w_p, pos_p)
        y_rnn, h_p = _rnn_prompt(zr, s, *rnn_w)
        kit = kw[:, :IDX_DIM].T.astype(BF16)
        ones_pad = jnp.zeros((s, HEAD_DIM), BF16).at[:, 0].set(1.0)
        vx = jnp.concatenate(
            [a for g in range(N_KV) for a in (vb[:, g * HEAD_DIM:(g + 1) * HEAD_DIM], ones_pad)], axis=1)
        y_att = _attn_prompt(qi, kw, q, jnp.concatenate([kit, kit], 0), kb.T, vx, s,
                             min(TOPK_MAX, s // 4))
        m = _merge(y_rnn, y_att, wor, woa, zr)
        bufs = _ln1_router(m, hp, wo, *ln1, wr, br, alpha, 0, t_all)
        st = (k32.reshape(1, s, N_KV, HEAD_DIM), v32.reshape(1, s, N_KV, HEAD_DIM),
              kw[None, :, :IDX_DIM], h_p, zr[None, s - (CONV_W - 1):s, :d])
        for lst, a in zip(st_p, st):
            lst.append(a)

        q, k32, kb, v32, vb, qi, zr, kw = _project_all(hs, w_p, pos_s)
        y_rnn, h_s = _rnn_sample(zr, state_conv[i], state_h[i], *rnn_w)
        sc = _idx_sample(page_table, qi.reshape(n, IDX_HEADS, IDX_DIM),
                         kw[:, IDX_DIM:IDX_DIM + IDX_HEADS, None], kw[:, None, :], cache_kidx[i])
        mask = _mask_sample(sc[:, 0], past + 1, min(TOPK_MAX, (past + 1) // 4))
        y_att = _attn_sample(page_table, q.reshape(n, N_HEADS, HEAD_DIM),
                             jnp.repeat(mask[:, :past], N_KV, axis=1)[:, None, :],
                             mask[:, None, past:past + LANES],
                             kb.reshape(n, N_KV, HEAD_DIM), vb.reshape(n, N_KV, HEAD_DIM),
                             cache_k[i].reshape(n_phys, page * N_KV, HEAD_DIM),
                             cache_v[i].reshape(n_phys, page * N_KV, HEAD_DIM)).reshape(n, dq)
        m = _merge(y_rnn, y_att, wor, woa, zr)
        x1, sel, gate = _ln1_router(m, hs, wo, *ln1, wr, br, alpha, s, t_all, prev=bufs)
        st = (k32.reshape(n, 1, N_KV, HEAD_DIM), v32.reshape(n, 1, N_KV, HEAD_DIM),
              kw[:, None, :IDX_DIM], h_s,
              jnp.concatenate([state_conv[i][:, 1:], zr[:, None, :d]], axis=1))
        for lst, a in zip(st_s, st):
            lst.append(a)

        rank, cnt = _moe_rank(sel)
        dest4, g4, tab = _moe_dest(sel, gate, rank, cnt)
        units = _moe_units(tab, n_units)
        dest_flat = dest4[:, :TOP_K].reshape(-1)
        xs = _moe_scatter(dest_flat, x1, n_slots)
        ys = _moe_experts(units, xs, w_up[i], b_up[i][:, None, :], w_down[i], b_down[i][:, None, :])
        tail = (w_ple_gate[i].astype(BF16), w_ple_proj[i].astype(BF16), row(ln2_g[i]), row(ln2_b[i]),
                alpha)
        hp = _moe_combine(dest_flat, g4, x1, ys, p_prompt[i][0], *tail, 0, s)
        hs = _moe_combine(dest_flat, g4, x1, ys, p_sample[i][:, 0], *tail, s, n)

    outs_p = [jnp.stack(l) for l in st_p]
    outs_s = [jnp.stack(l) for l in st_s]
    return (hp[None], hs[:, None], *outs_p, *outs_s)
```

```python
import functools

import jax
import jax.numpy as jnp
from jax import lax
from jax.experimental import pallas as pl
from jax.experimental.pallas import tpu as pltpu

F32 = jnp.float32
BF16 = jnp.bfloat16
I32 = jnp.int32

N_HEADS = 16
HEAD_DIM = 128
N_KV = 4
Q_PER_KV = N_HEADS // N_KV
IDX_HEADS = 16
IDX_DIM = 64
TOPK_MAX = 256
Q_BLOCK = 128
ROPE_THETA = 10000.0
RNN_BW = 128
CONV_W = 4
RG_C = 8.0
N_EXPERTS = 32
TOP_K = 4
SWIGLU_LIMIT = 7.0
SWIGLU_ALPHA = 1.702
LN_EPS = 1e-5
LANES = 128
INT_MIN = -(2 ** 31)
NEG_BIG = -1e30
LOG2_E = 1.4426950408889634
VMEM_LIMIT = 56 * 1024 * 1024


def _pick(n, cands):
    for c in cands:
        if n % c == 0:
            return c
    return n


def _cparams(sem):
    return pltpu.CompilerParams(dimension_semantics=sem, vmem_limit_bytes=VMEM_LIMIT)


def _const_spec(shape):
    nd = len(shape)
    return pl.BlockSpec(shape, lambda *a: (0,) * nd, pipeline_mode=pl.Buffered(1))


def _proj_kernel(*refs, mode, kinds, shift, n_tab, scale):
    n_w = 2 if shift else 1
    x_ref, w_refs = refs[0], refs[1:1 + n_w]
    tabs = refs[1 + n_w:1 + n_w + n_tab]
    outs = refs[1 + n_w + n_tab:1 + n_w + n_tab + len(kinds)]
    w_bf = refs[-1]
    tn = w_bf.shape[1]

    @pl.when(pl.program_id(1) == 0)
    def _():
        if shift:
            wide = jnp.concatenate([w_refs[0][...], w_refs[1][...]], axis=1)
            w_bf[...] = wide[:, shift:shift + tn].astype(BF16)
        else:
            w_bf[...] = w_refs[0][...].astype(BF16)

    z = jnp.dot(x_ref[...], w_bf[...], preferred_element_type=F32)
    for h in range(tn // LANES):
        zh = z[:, h * LANES:(h + 1) * LANES]
        if mode == "plain":
            r = zh
        elif mode == "rope128":
            r = zh * tabs[0][...] + pltpu.roll(zh, 64, 1) * tabs[1][...]
        else:
            r = (zh * tabs[0][...] + pltpu.roll(zh, 96, 1) * tabs[1][...]
                 + pltpu.roll(zh, 32, 1) * tabs[2][...])
        if scale != 1.0:
            r = r * scale
        for o, kind in zip(outs, kinds):
            if kind == "transposed":
                o[h * LANES:(h + 1) * LANES, :] = r.T.astype(o.dtype)
            elif kind == "with_ones":
                lane = lax.broadcasted_iota(I32, r.shape, 1)
                o[:, 2 * h * LANES:(2 * h + 1) * LANES] = r.astype(o.dtype)
                o[:, (2 * h + 1) * LANES:(2 * h + 2) * LANES] = jnp.where(lane == 0, 1.0, 0.0).astype(o.dtype)
            else:
                o[:, h * LANES:(h + 1) * LANES] = r.astype(o.dtype)


def _proj(x, w, col0, ncols, tabs, mode, outs, scale=1.0, shift=0):
    t, k = x.shape
    tm = _pick(t, (1024, 512, 256, 128))
    tn = _pick(ncols, (512, 256, 128))
    c0 = col0 // tn
    in_specs = [pl.BlockSpec((tm, k), lambda j, i: (i, 0)),
                pl.BlockSpec((k, tn), lambda j, i: (0, c0 + j))]
    ws = [w]
    if shift:
        in_specs.append(pl.BlockSpec((k, LANES), lambda j, i: (0, (c0 + j + 1) * (tn // LANES))))
        ws.append(w)
    in_specs += [pl.BlockSpec((tm, LANES), lambda j, i: (i, 0)) for _ in tabs]
    shapes, specs = [], []
    for dt, kind in outs:
        if kind == "transposed":
            shapes.append(jax.ShapeDtypeStruct((ncols, t), dt))
            specs.append(pl.BlockSpec((tn, tm), lambda j, i: (j, i)))
        elif kind == "with_ones":
            shapes.append(jax.ShapeDtypeStruct((t, 2 * ncols), dt))
            specs.append(pl.BlockSpec((tm, 2 * tn), lambda j, i: (i, j)))
        else:
            shapes.append(jax.ShapeDtypeStruct((t, ncols), dt))
            specs.append(pl.BlockSpec((tm, tn), lambda j, i: (i, j)))
    return pl.pallas_call(
        functools.partial(_proj_kernel, mode=mode, kinds=tuple(kd for _, kd in outs), shift=shift,
                          n_tab=len(tabs), scale=scale),
        out_shape=shapes,
        grid=(ncols // tn, t // tm),
        in_specs=in_specs,
        out_specs=specs,
        scratch_shapes=[pltpu.VMEM((k, tn), BF16)],
        compiler_params=_cparams(("parallel", "arbitrary")),
        name="proj_" + mode,
    )(x, *ws, *tabs)


def _rope_tables(pos):
    posf = pos.astype(F32)[:, None]
    h128 = HEAD_DIM // 2
    inv = ROPE_THETA ** (-jnp.arange(h128, dtype=F32) / h128)
    c, s = jnp.cos(posf * inv), jnp.sin(posf * inv)
    t128 = (jnp.concatenate([c, c], 1), jnp.concatenate([-s, s], 1))
    h64 = IDX_DIM // 2
    inv = ROPE_THETA ** (-jnp.arange(h64, dtype=F32) / h64)
    c, s = jnp.cos(posf * inv), jnp.sin(posf * inv)
    z = jnp.zeros_like(s)
    c64 = jnp.concatenate([c, c], 1)
    sa64 = jnp.concatenate([-s, z], 1)
    sb64 = jnp.concatenate([z, s], 1)
    t64 = tuple(jnp.concatenate([a, a], 1) for a in (c64, sa64, sb64))
    idx_scale = (IDX_HEADS * IDX_DIM) ** -0.5
    n = pos.shape[0]
    ck = jnp.concatenate([c64, jnp.full((n, IDX_HEADS), idx_scale, F32),
                          jnp.zeros((n, LANES - IDX_DIM - IDX_HEADS), F32)], 1)
    z64 = jnp.zeros((n, LANES - IDX_DIM), F32)
    tkw = (ck, jnp.concatenate([sa64, z64], 1), jnp.concatenate([sb64, z64], 1))
    return t128, t64, tkw


def _sigmoid(x):
    return 0.5 * jnp.tanh(0.5 * x) + 0.5


def _softplus_neg(lam):
    return jnp.maximum(-lam, 0.0) + jnp.log1p(jnp.exp(-jnp.abs(lam)))


def _gelu_tanh(x):
    return 0.5 * x * (1.0 + jnp.tanh(0.7978845608028654 * (x + 0.044715 * (x * x * x))))


def _rglru_gates(xc, wa_ref, wx_ref, ba, bx, lam):
    nblk = xc.shape[1] // RNN_BW
    rs, gs = [], []
    for j in range(nblk):
        xj = xc[:, j * RNN_BW:(j + 1) * RNN_BW].astype(BF16)
        rs.append(jnp.dot(xj, wa_ref[j], preferred_element_type=F32))
        gs.append(jnp.dot(xj, wx_ref[j], preferred_element_type=F32))
    r = _sigmoid(jnp.concatenate(rs, 1) + ba)
    g = _sigmoid(jnp.concatenate(gs, 1) + bx)
    log_a = -RG_C * r * _softplus_neg(lam)
    a = jnp.exp(log_a)
    mult = jnp.sqrt(jnp.tanh(-log_a) * (1.0 + a * a))
    return a, mult, g * xc


def _rnn_prompt_kernel(xr_ref, gr_ref, cw_ref, cb_ref, wa_ref, wx_ref, ba_ref, bx_ref, lam_ref,
                       y_ref, hl_ref, xbuf, hcar, a_scr, b_scr, h_scr):
    t = pl.program_id(1)
    tm, cw = xr_ref.shape

    @pl.when(t == 0)
    def _():
        xbuf[0:8, :] = jnp.zeros((8, cw), F32)
        hcar[...] = jnp.zeros_like(hcar)

    x = xr_ref[...]
    xbuf[8:8 + tm, :] = x
    w = cw_ref[...]
    xc = (cb_ref[...] + w[3:4] * x + w[2:3] * xbuf[7:7 + tm, :]
          + w[1:2] * xbuf[6:6 + tm, :] + w[0:1] * xbuf[5:5 + tm, :])
    xbuf[0:8, :] = x[tm - 8:tm, :]
    a, mult, gx = _rglru_gates(xc, wa_ref, wx_ref, ba_ref[...], bx_ref[...], lam_ref[...])
    pos = t * tm + lax.broadcasted_iota(I32, (tm, cw), 0)
    mult = jnp.where(pos == 0, 1.0, mult)
    a_scr[...] = a
    b_scr[...] = mult * gx
    row8 = lax.broadcasted_iota(I32, (8, cw), 0)

    def group(g, carry):
        r0 = pl.multiple_of(g * 8, 8)
        av = a_scr[pl.ds(r0, 8), :]
        bv = b_scr[pl.ds(r0, 8), :]
        for d in (1, 2, 4):
            a_s = pltpu.roll(av, d, 0)
            b_s = pltpu.roll(bv, d, 0)
            m = row8 >= d
            bv = jnp.where(m, av * b_s + bv, bv)
            av = jnp.where(m, av * a_s, av)
        h = av * carry + bv
        h_scr[pl.ds(r0, 8), :] = h
        return h[7:8, :]

    carry = lax.fori_loop(0, tm // 8, group, hcar[0:1, :])
    hcar[0:1, :] = carry
    y_ref[...] = (_gelu_tanh(gr_ref[...]) * h_scr[...]).astype(y_ref.dtype)

    @pl.when(t == pl.num_programs(1) - 1)
    def _():
        hl_ref[...] = carry


def _rnn_prompt(zr, s, conv_w, conv_b, wa, wx, ba, bx, lam):
    d = conv_w.shape[1]
    cw = 512
    tm = _pick(s, (256, 128))
    ncb = d // cw
    nb = cw // RNN_BW
    vec = lambda: pl.BlockSpec((1, cw), lambda c, t: (0, c))
    y, hl = pl.pallas_call(
        _rnn_prompt_kernel,
        out_shape=[jax.ShapeDtypeStruct((s, d), BF16), jax.ShapeDtypeStruct((1, d), F32)],
        grid=(ncb, s // tm),
        in_specs=[pl.BlockSpec((tm, cw), lambda c, t: (t, c)),
                  pl.BlockSpec((tm, cw), lambda c, t: (t, ncb + c)),
                  pl.BlockSpec((CONV_W, cw), lambda c, t: (0, c)),
                  vec(),
                  pl.BlockSpec((nb, RNN_BW, RNN_BW), lambda c, t: (c, 0, 0)),
                  pl.BlockSpec((nb, RNN_BW, RNN_BW), lambda c, t: (c, 0, 0)),
                  vec(), vec(), vec()],
        out_specs=[pl.BlockSpec((tm, cw), lambda c, t: (t, c)),
                   pl.BlockSpec((1, cw), lambda c, t: (0, c))],
        scratch_shapes=[pltpu.VMEM((tm + 8, cw), F32), pltpu.VMEM((8, cw), F32),
                        pltpu.VMEM((tm, cw), F32), pltpu.VMEM((tm, cw), F32),
                        pltpu.VMEM((tm, cw), F32)],
        compiler_params=_cparams(("parallel", "arbitrary")),
        name="rnn_prompt",
    )(zr, zr, conv_w, conv_b, wa, wx, ba, bx, lam)
    return y, hl


def _rnn_sample_kernel(xr_ref, gr_ref, c0_ref, c1_ref, c2_ref, h0_ref, cw_ref, cb_ref,
                       wa_ref, wx_ref, ba_ref, bx_ref, lam_ref, y_ref, h_ref):
    w = cw_ref[...]
    xc = (cb_ref[...] + w[3:4] * xr_ref[...] + w[2:3] * c2_ref[...]
          + w[1:2] * c1_ref[...] + w[0:1] * c0_ref[...])
    a, mult, gx = _rglru_gates(xc, wa_ref, wx_ref, ba_ref[...], bx_ref[...], lam_ref[...])
    h = a * h0_ref[...] + mult * gx
    h_ref[...] = h
    y_ref[...] = (_gelu_tanh(gr_ref[...]) * h).astype(y_ref.dtype)


def _rnn_sample(zr, conv_state, h0, conv_w, conv_b, wa, wx, ba, bx, lam):
    n, d = h0.shape
    cw = 512
    ncb = d // cw
    nb = cw // RNN_BW
    blk = lambda off: pl.BlockSpec((n, cw), lambda c: (0, off + c))
    vec = lambda: pl.BlockSpec((1, cw), lambda c: (0, c))
    wsp = lambda: pl.BlockSpec((nb, RNN_BW, RNN_BW), lambda c: (c, 0, 0))
    return pl.pallas_call(
        _rnn_sample_kernel,
        out_shape=[jax.ShapeDtypeStruct((n, d), BF16), jax.ShapeDtypeStruct((n, d), F32)],
        grid=(ncb,),
        in_specs=[blk(0), blk(ncb), blk(0), blk(0), blk(0), blk(0),
                  pl.BlockSpec((CONV_W, cw), lambda c: (0, c)), vec(), wsp(), wsp(),
                  vec(), vec(), vec()],
        out_specs=[blk(0), blk(0)],
        compiler_params=_cparams(("parallel",)),
        name="rnn_sample",
    )(zr, zr, conv_state[:, 0], conv_state[:, 1], conv_state[:, 2], h0,
      conv_w, conv_b, wa, wx, ba, bx, lam)


def _layer_norm(x, g, b):
    mu = jnp.mean(x, axis=-1, keepdims=True)
    xc = x - mu
    var = jnp.mean(xc * xc, axis=-1, keepdims=True)
    return xc * lax.rsqrt(var + LN_EPS) * g + b


def _merge_kernel(yr_ref, ya_ref, wr_ref, wa_ref, ga_ref, gb_ref, o_ref):
    a = jnp.dot(yr_ref[...], wr_ref[...], preferred_element_type=F32)
    b = jnp.dot(ya_ref[...], wa_ref[...], preferred_element_type=F32)
    m = _sigmoid(ga_ref[...]) * a + _sigmoid(gb_ref[...]) * b
    o_ref[...] = m.astype(o_ref.dtype)


def _merge(y_rnn, y_att, w_o_rnn, w_o_att, zr):
    t, d = y_rnn.shape
    tm = _pick(t, (512, 256, 128))
    tn = 512
    nj = d // tn
    return pl.pallas_call(
        _merge_kernel,
        out_shape=jax.ShapeDtypeStruct((t, d), BF16),
        grid=(t // tm, nj),
        in_specs=[pl.BlockSpec((tm, d), lambda i, j: (i, 0)),
                  pl.BlockSpec((tm, d), lambda i, j: (i, 0)),
                  pl.BlockSpec((d, tn), lambda i, j: (0, j)),
                  pl.BlockSpec((d, tn), lambda i, j: (0, j)),
                  pl.BlockSpec((tm, tn), lambda i, j: (i, 2 * nj + j)),
                  pl.BlockSpec((tm, tn), lambda i, j: (i, 3 * nj + j))],
        out_specs=pl.BlockSpec((tm, tn), lambda i, j: (i, j)),
        compiler_params=_cparams(("parallel", "parallel")),
        name="merge",
    )(y_rnn, y_att, w_o_rnn, w_o_att, zr, zr)


def _ln1_router_kernel(*refs, alpha, n_skip):
    m_ref, x_ref, w_ref, g_ref, b_ref, wr_ref, br_ref, x1_ref, sel_ref, gate_ref = refs[n_skip:]
    y = alpha * x_ref[...] + jnp.dot(m_ref[...], w_ref[...], preferred_element_type=F32)
    x1 = _layer_norm(y, g_ref[...], b_ref[...])
    x1_ref[...] = x1
    x_hi = x1.astype(BF16)
    x_lo = (x1 - x_hi.astype(F32)).astype(BF16)
    part = (jnp.dot(x_hi, wr_ref[...], preferred_element_type=F32)
            + jnp.dot(x_lo, wr_ref[...], preferred_element_type=F32))
    logits = part[:, :LANES] + part[:, LANES:] + br_ref[...]
    lane = lax.broadcasted_iota(I32, logits.shape, 1)
    live = lane < N_EXPERTS
    cur = jnp.where(live, logits, -jnp.inf)
    top = jnp.max(cur, axis=1, keepdims=True)
    sel = jnp.zeros(logits.shape, jnp.bool_)
    for _ in range(TOP_K):
        mx = jnp.max(cur, axis=1, keepdims=True)
        first = jnp.min(jnp.where(cur == mx, lane, LANES), axis=1, keepdims=True)
        pick = lane == first
        sel = jnp.logical_or(sel, pick)
        cur = jnp.where(pick, -jnp.inf, cur)
    e = jnp.where(sel, jnp.exp(logits - top), 0.0)
    sel_ref[...] = jnp.where(sel, 1.0, 0.0)
    gate_ref[...] = e / jnp.sum(e, axis=1, keepdims=True)


def _ln1_router(m, x, w_out, g, b, w_router, b_router, alpha, row0, t_all, prev=None):
    t, d = x.shape
    tm = _pick(t, (256, 128))
    r0 = row0 // tm
    in_specs = [pl.BlockSpec((tm, d), lambda i: (i, 0)),
                pl.BlockSpec((tm, d), lambda i: (i, 0)),
                _const_spec((d, d)), _const_spec((1, d)), _const_spec((1, d)),
                _const_spec((d, 2 * LANES)), _const_spec((1, LANES))]
    args = [m, x, w_out, g, b, w_router, b_router]
    aliases = {}
    if prev is not None:
        in_specs = [pl.BlockSpec(memory_space=pl.ANY)] * 3 + in_specs
        args = list(prev) + args
        aliases = {0: 0, 1: 1, 2: 2}

    return pl.pallas_call(
        functools.partial(_ln1_router_kernel, alpha=alpha, n_skip=len(aliases)),
        out_shape=[jax.ShapeDtypeStruct((t_all, d), F32),
                   jax.ShapeDtypeStruct((t_all, LANES), F32),
                   jax.ShapeDtypeStruct((t_all, LANES), F32)],
        grid=(t // tm,),
        in_specs=in_specs,
        out_specs=[pl.BlockSpec((tm, d), lambda i: (r0 + i, 0)),
                   pl.BlockSpec((tm, LANES), lambda i: (r0 + i, 0)),
                   pl.BlockSpec((tm, LANES), lambda i: (r0 + i, 0))],
        input_output_aliases=aliases,
        compiler_params=_cparams(("parallel",)),
        name="ln1_router",
    )(*args)


def _sort_key(x):
    bits = lax.bitcast_convert_type(x, I32)
    return bits ^ (jnp.right_shift(bits, 31) & 0x7FFFFFFF)


def _count(keys_ref, nch, cw, tvec, strict):
    rows = keys_ref.shape[0]
    tb = jnp.broadcast_to(tvec, (rows, LANES))

    def body(c, cnt):
        off = c * cw
        for s in range(cw // LANES):
            k = keys_ref[:, pl.ds(pl.multiple_of(off + s * LANES, LANES), LANES)]
            hit = (k > tb) if strict else (k >= tb)
            cnt = cnt + jnp.where(hit, 1.0, 0.0)
        return cnt

    cnt = lax.fori_loop(0, nch, body, jnp.zeros((rows, LANES), F32))
    return jnp.sum(cnt, axis=1, keepdims=True)


def _select_threshold(keys_ref, nch, cw, n_sel, few):
    rows = keys_ref.shape[0]
    settled = few > 0.5

    def cond(st):
        b, _, cnt_cur = st
        open_rows = jnp.where(jnp.logical_or(settled, cnt_cur == n_sel), 0.0, 1.0)
        return jnp.logical_and(b < 32, jnp.max(open_rows) > 0.0)

    def bit_step(st):
        b, cur, cnt_cur = st
        cand = cur | lax.shift_left(jnp.int32(1), 31 - b)
        cnt = _count(keys_ref, nch, cw, cand ^ INT_MIN, False)
        take = cnt >= n_sel
        return b + 1, jnp.where(take, cand, cur), jnp.where(take, cnt, cnt_cur)

    everything = jnp.zeros((rows, 1), F32) + jnp.asarray(nch * cw, F32)
    _, cur, n_ge = lax.while_loop(cond, bit_step, (jnp.int32(0), jnp.zeros((rows, 1), I32), everything))
    t = cur ^ INT_MIN
    tied = jnp.where(jnp.logical_or(settled, n_ge <= n_sel), 0.0, 1.0)

    @pl.when(jnp.max(tied) > 0.0)
    def _():
        n_gt = _count(keys_ref, nch, cw, t, True)
        tb = jnp.broadcast_to(t, (rows, LANES))
        needb = jnp.broadcast_to(jnp.where(tied > 0.5, n_sel - n_gt, 1e9), (rows, LANES))
        r = lax.broadcasted_iota(I32, (LANES, LANES), 0)
        c = lax.broadcasted_iota(I32, (LANES, LANES), 1)
        upper = jnp.where(r < c, 1.0, 0.0).astype(BF16)

        def fix(j, run):
            sl = pl.ds(pl.multiple_of(j * LANES, LANES), LANES)
            k = keys_ref[:, sl]
            eq = k == tb
            eqf = jnp.where(eq, 1.0, 0.0)
            before = jnp.dot(eqf.astype(BF16), upper, preferred_element_type=F32) + run
            drop = jnp.logical_and(eq, before >= needb)
            keys_ref[:, sl] = jnp.where(drop, INT_MIN, k)
            return run + jnp.sum(eqf, axis=1, keepdims=True)

        lax.fori_loop(0, nch * (cw // LANES), fix, jnp.zeros((rows, 1), F32))

    return jnp.maximum(t, INT_MIN + 1)


def _attn_prompt_kernel(qi_ref, kw_ref, q_ref, kit_ref, kt_ref, vx_ref, o_ref,
                        keys_scr, lhs_scr, wb_scr, tb_scr, qg_scr, s_scr, p_scr, m_scr, acc_scr,
                        *, n_sel, kc):
    i = pl.program_id(0)
    qb = Q_BLOCK
    nch = (i * qb + qb + kc - 1) // kc
    kw = kw_ref[...]
    lane = lax.broadcasted_iota(I32, (qb, LANES), 1)
    for h in range(IDX_HEADS):
        wb_scr[h] = jnp.broadcast_to(kw[:, IDX_DIM + h:IDX_DIM + h + 1], (qb, LANES))
        blk = qi_ref[:, (h // 2) * LANES:(h // 2 + 1) * LANES]
        keep = (lane < IDX_DIM) if h % 2 == 0 else (lane >= IDX_DIM)
        lhs_scr[h] = jnp.where(keep, blk, jnp.zeros_like(blk))

    sw = min(256, kc)

    def score_chunk(c, carry):
        for s in range(kc // sw):
            o2 = pl.multiple_of(c * kc + s * sw, sw)
            kt = kit_ref[:, pl.ds(o2, sw)]
            acc = jnp.zeros((qb, sw), F32)
            for h in range(IDX_HEADS):
                sc = jnp.dot(lhs_scr[h], kt, preferred_element_type=F32)
                wb = wb_scr[h]
                acc = acc + jnp.maximum(sc, 0.0) * jnp.concatenate([wb] * (sw // LANES), axis=1)
            kpos = o2 + lax.broadcasted_iota(I32, (qb, sw), 1)
            qpos = i * qb + lax.broadcasted_iota(I32, (qb, sw), 0)
            keys_scr[:, pl.ds(o2, sw)] = jnp.where(kpos <= qpos, _sort_key(acc), INT_MIN)
        return carry

    lax.fori_loop(0, nch, score_chunk, 0)
    n_causal = i * qb + lax.broadcasted_iota(I32, (qb, 1), 0) + 1
    t = _select_threshold(keys_scr, nch, kc, n_sel, jnp.where(n_causal <= n_sel, 1.0, 0.0))
    rows = Q_PER_KV * qb
    strip = 32
    nrep = kc // LANES
    tb_scr[...] = jnp.broadcast_to(t, (qb, LANES))

    for g in range(N_KV):
        for j in range(Q_PER_KV):
            h = g * Q_PER_KV + j
            qg_scr[g, j * qb:(j + 1) * qb, :] = q_ref[:, h * HEAD_DIM:(h + 1) * HEAD_DIM]
    m_scr[...] = jnp.full(m_scr.shape, NEG_BIG, F32)
    acc_scr[...] = jnp.zeros_like(acc_scr)

    def body(c, carry):
        off = pl.multiple_of(c * kc, kc)
        for g in range(N_KV):
            s_scr[g] = jnp.dot(qg_scr[g], kt_ref[g * HEAD_DIM:(g + 1) * HEAD_DIM, pl.ds(off, kc)],
                               preferred_element_type=F32)
            for rq in range(0, qb, strip):
                tb = jnp.concatenate([tb_scr[rq:rq + strip, :]] * nrep, axis=1)
                for j in range(Q_PER_KV):
                    r = j * qb + rq
                    km = keys_scr[rq:rq + strip, pl.ds(off, kc)] >= tb
                    s = jnp.where(km, s_scr[g, r:r + strip, :], NEG_BIG)
                    s_scr[g, r:r + strip, :] = s
                    m_old = m_scr[g, r:r + strip, :]
                    m_new = jnp.maximum(m_old, jnp.max(s, axis=1, keepdims=True))
                    alpha = jnp.exp2(m_old - m_new)
                    m_scr[g, r:r + strip, :] = m_new
                    acc_scr[g, r:r + strip, :] = (acc_scr[g, r:r + strip, :]
                                                  * jnp.concatenate([alpha] * 2, axis=1))
            for r in range(0, rows, strip):
                m_new = m_scr[g, r:r + strip, :]
                p = jnp.exp2(s_scr[g, r:r + strip, :] - jnp.concatenate([m_new] * nrep, axis=1))
                p_scr[g, r:r + strip, :] = p.astype(BF16)
            acc_scr[g] += jnp.dot(p_scr[g], vx_ref[pl.ds(off, kc), g * 2 * HEAD_DIM:(g + 1) * 2 * HEAD_DIM],
                                  preferred_element_type=F32)
        return carry

    def body_pair(c2, carry):
        body(2 * c2, carry)
        return body(2 * c2 + 1, carry)

    lax.fori_loop(0, nch // 2, body_pair, 0)

    @pl.when(nch % 2 == 1)
    def _():
        body(nch - 1, 0)

    for g in range(N_KV):
        acc = acc_scr[g]
        out = acc[:, :HEAD_DIM] / acc[:, HEAD_DIM:HEAD_DIM + 1]
        for j in range(Q_PER_KV):
            h = g * Q_PER_KV + j
            o_ref[:, h * HEAD_DIM:(h + 1) * HEAD_DIM] = out[j * qb:(j + 1) * qb].astype(o_ref.dtype)


def _attn_prompt(qi, kw, q, kit2, kt, vx, s, n_sel):
    kc = min(512, s)
    d = N_HEADS * HEAD_DIM
    rows = Q_PER_KV * Q_BLOCK
    return pl.pallas_call(
        functools.partial(_attn_prompt_kernel, n_sel=n_sel, kc=kc),
        out_shape=jax.ShapeDtypeStruct((s, d), BF16),
        grid=(s // Q_BLOCK,),
        in_specs=[pl.BlockSpec((Q_BLOCK, IDX_HEADS * IDX_DIM), lambda i: (i, 0)),
                  pl.BlockSpec((Q_BLOCK, LANES), lambda i: (i, 0)),
                  pl.BlockSpec((Q_BLOCK, d), lambda i: (i, 0)),
                  _const_spec((LANES, s)), _const_spec((N_KV * HEAD_DIM, s)),
                  _const_spec((s, N_KV * 2 * HEAD_DIM))],
        out_specs=pl.BlockSpec((Q_BLOCK, d), lambda i: (i, 0)),
        scratch_shapes=[pltpu.VMEM((Q_BLOCK, s), I32),
                        pltpu.VMEM((IDX_HEADS, Q_BLOCK, LANES), BF16),
                        pltpu.VMEM((IDX_HEADS, Q_BLOCK, LANES), F32),
                        pltpu.VMEM((Q_BLOCK, LANES), I32),
                        pltpu.VMEM((N_KV, rows, HEAD_DIM), BF16),
                        pltpu.VMEM((N_KV, rows, kc), F32),
                        pltpu.VMEM((N_KV, rows, kc), BF16),
                        pltpu.VMEM((N_KV, rows, LANES), F32),
                        pltpu.VMEM((N_KV, rows, 2 * HEAD_DIM), F32)],
        compiler_params=_cparams(("parallel",)),
        name="attn_prompt",
    )(qi, kw, q, kit2, kt, vx)


IDX_SEQS_PER_STEP = 4


def _idx_sample_kernel(pt_ref, qi_ref, w_ref, kn_ref, *refs, npg, past, group):
    pages, o_ref = refs[:group * npg], refs[group * npg]
    lane = lax.broadcasted_iota(I32, (1, LANES), 1)
    for g in range(group):
        qi = qi_ref[g]
        kp = jnp.concatenate([p[...] for p in pages[g * npg:(g + 1) * npg]], axis=0).astype(BF16)
        sc = lax.dot_general(qi, kp, (((1,), (1,)), ((), ())), preferred_element_type=F32)
        w = w_ref[g]
        o_ref[g, :, 0:past] = jnp.sum(jnp.maximum(sc, 0.0) * w, axis=0, keepdims=True)
        kn = kn_ref[g][:, :IDX_DIM].astype(BF16).astype(F32)
        sn = jnp.sum(qi.astype(F32) * kn, axis=1, keepdims=True)
        new = jnp.sum(jnp.maximum(sn, 0.0) * w, axis=0, keepdims=True)
        o_ref[g, :, past:past + LANES] = jnp.where(lane == 0, new, -jnp.inf)


def _idx_sample(page_table, qi3, w3, kn3, cache_kidx):
    n, npg = page_table.shape
    page = cache_kidx.shape[1]
    past = npg * page
    group = IDX_SEQS_PER_STEP if n % IDX_SEQS_PER_STEP == 0 else 1
    page_specs = [pl.BlockSpec((None, page, IDX_DIM),
                               lambda b, pt, g=g, p=p: (pt[(b * group + g) * npg + p], 0, 0))
                  for g in range(group) for p in range(npg)]
    return pl.pallas_call(
        functools.partial(_idx_sample_kernel, npg=npg, past=past, group=group),
        out_shape=jax.ShapeDtypeStruct((n, 1, past + LANES), F32),
        grid_spec=pltpu.PrefetchScalarGridSpec(
            num_scalar_prefetch=1, grid=(n // group,),
            in_specs=[pl.BlockSpec((group, IDX_HEADS, IDX_DIM), lambda b, pt: (b, 0, 0)),
                      pl.BlockSpec((group, IDX_HEADS, 1), lambda b, pt: (b, 0, 0)),
                      pl.BlockSpec((group, 1, LANES), lambda b, pt: (b, 0, 0))] + page_specs,
            out_specs=pl.BlockSpec((group, 1, past + LANES), lambda b, pt: (b, 0, 0))),
        compiler_params=_cparams(("parallel",)),
        name="idx_sample",
    )(page_table.reshape(-1), qi3, w3, kn3, *([cache_kidx] * (group * npg)))


def _mask_sample_kernel(sc_ref, m_ref, keys_scr, *, n_valid, n_sel):
    rows, width = sc_ref.shape
    pos = lax.broadcasted_iota(I32, (rows, width), 1)
    keys_scr[...] = jnp.where(pos < n_valid, _sort_key(sc_ref[...]), INT_MIN)
    few = jnp.full((rows, 1), 1.0 if n_valid <= n_sel else 0.0, F32)
    t = _select_threshold(keys_scr, width // LANES, LANES, n_sel, few)
    m_ref[...] = jnp.where(keys_scr[...] >= t, 1.0, 0.0)


def _mask_sample(sc, n_valid, n_sel):
    n, width = sc.shape
    return pl.pallas_call(
        functools.partial(_mask_sample_kernel, n_valid=n_valid, n_sel=n_sel),
        out_shape=jax.ShapeDtypeStruct((n, width), F32),
        scratch_shapes=[pltpu.VMEM((n, width), I32)],
        name="mask_sample",
    )(sc)


def _attn_sample_kernel(pt_ref, q_ref, m4_ref, mn_ref, kn_ref, vn_ref, *refs, npg):
    kpages, vpages, o_ref = refs[:npg], refs[npg:2 * npg], refs[2 * npg]
    q = q_ref[...]
    kp = jnp.concatenate([p[...] for p in kpages], axis=0).astype(BF16)
    s = lax.dot_general(q, kp, (((1,), (1,)), ((), ())), preferred_element_type=F32)
    grp = lax.broadcasted_iota(I32, s.shape, 0) // Q_PER_KV
    own = (lax.broadcasted_iota(I32, s.shape, 1) & (N_KV - 1)) == grp
    s = jnp.where(own, jnp.where(m4_ref[...] > 0.5, s, NEG_BIG), NEG_BIG)
    grp_h = lax.broadcasted_iota(I32, (N_HEADS, HEAD_DIM), 0) // Q_PER_KV
    kn = kn_ref[...].astype(F32)
    vn = vn_ref[...].astype(F32)
    kn_h = jnp.zeros((N_HEADS, HEAD_DIM), F32)
    vn_h = jnp.zeros((N_HEADS, HEAD_DIM), F32)
    for g in range(N_KV):
        kn_h = jnp.where(grp_h == g, kn[g:g + 1, :], kn_h)
        vn_h = jnp.where(grp_h == g, vn[g:g + 1, :], vn_h)
    sn = jnp.sum(q.astype(F32) * kn_h, axis=1, keepdims=True)
    sn = jnp.where(mn_ref[:, 0:1] > 0.5, sn, NEG_BIG)
    m = jnp.maximum(jnp.max(s, axis=1, keepdims=True), sn)
    p = jnp.exp2(s - m)
    pn = jnp.exp2(sn - m)
    l = jnp.sum(p, axis=1, keepdims=True) + pn
    vp = jnp.concatenate([r[...] for r in vpages], axis=0).astype(BF16)
    o = jnp.dot(p.astype(BF16), vp, preferred_element_type=F32)
    o_ref[...] = ((o + pn.astype(BF16).astype(F32) * vn_h) / l).astype(o_ref.dtype)


def _attn_sample(page_table, q3, mask4, mask_new, kn3, vn3, cache_k, cache_v):
    n, npg = page_table.shape
    prow = cache_k.shape[1]
    pspec = lambda p: pl.BlockSpec((None, prow, HEAD_DIM), lambda b, pt, p=p: (pt[b * npg + p], 0, 0))
    return pl.pallas_call(
        functools.partial(_attn_sample_kernel, npg=npg),
        out_shape=jax.ShapeDtypeStruct((n, N_HEADS, HEAD_DIM), BF16),
        grid_spec=pltpu.PrefetchScalarGridSpec(
            num_scalar_prefetch=1, grid=(n,),
            in_specs=[pl.BlockSpec((None, N_HEADS, HEAD_DIM), lambda b, pt: (b, 0, 0)),
                      pl.BlockSpec((None, 1, npg * prow), lambda b, pt: (b, 0, 0)),
                      pl.BlockSpec((None, 1, LANES), lambda b, pt: (b, 0, 0)),
                      pl.BlockSpec((None, N_KV, HEAD_DIM), lambda b, pt: (b, 0, 0)),
                      pl.BlockSpec((None, N_KV, HEAD_DIM), lambda b, pt: (b, 0, 0))]
            + [pspec(p) for p in range(npg)] + [pspec(p) for p in range(npg)],
            out_specs=pl.BlockSpec((None, N_HEADS, HEAD_DIM), lambda b, pt: (b, 0, 0))),
        compiler_params=_cparams(("parallel",)),
        name="attn_sample",
    )(page_table.reshape(-1), q3, mask4, mask_new, kn3, vn3, *([cache_k] * npg), *([cache_v] * npg))


MOE_RB = 256
MOE_RC = 1280
MOE_TF = 512
MOE_TN = 512


def _moe_sizes(n_tok):
    n_assign = n_tok * TOP_K
    n_slots = (n_assign // MOE_RB + N_EXPERTS) * MOE_RB
    n_units = N_EXPERTS + n_assign // MOE_RC
    return n_slots, n_units


def _moe_rank_kernel(sel_ref, rank_ref, cnt_ref, carry):
    i = pl.program_id(0)
    tp = sel_ref.shape[0]

    @pl.when(i == 0)
    def _():
        carry[...] = jnp.zeros_like(carry)

    a = sel_ref[...]
    r = lax.broadcasted_iota(I32, (tp, tp), 0)
    c = lax.broadcasted_iota(I32, (tp, tp), 1)
    lower = jnp.where(c < r, 1.0, 0.0).astype(BF16)
    rank_ref[...] = jnp.dot(lower, a.astype(BF16), preferred_element_type=F32) + carry[0:1, :]
    carry[...] = carry[...] + jnp.sum(a, axis=0, keepdims=True)
    cnt_ref[...] = carry[...]


def _moe_rank(sel):
    t = sel.shape[0]
    tp = _pick(t, (256, 128))
    return pl.pallas_call(
        _moe_rank_kernel,
        out_shape=[jax.ShapeDtypeStruct((t, LANES), F32), jax.ShapeDtypeStruct((8, LANES), F32)],
        grid=(t // tp,),
        in_specs=[pl.BlockSpec((tp, LANES), lambda i: (i, 0))],
        out_specs=[pl.BlockSpec((tp, LANES), lambda i: (i, 0)),
                   pl.BlockSpec((8, LANES), lambda i: (0, 0))],
        scratch_shapes=[pltpu.VMEM((8, LANES), F32)],
        compiler_params=_cparams(("arbitrary",)),
        name="moe_rank",
    )(sel)


def _moe_dest_kernel(sel_ref, gate_ref, rank_ref, cnt_ref, dest_ref, g4_ref, tab_ref):
    cnt = cnt_ref[...]
    lane8 = lax.broadcasted_iota(I32, cnt.shape, 1)
    padded = jnp.ceil(cnt * (1.0 / MOE_RB)) * MOE_RB
    incl = padded
    for d in (1, 2, 4, 8, 16, 32, 64):
        incl = incl + jnp.where(lane8 >= d, pltpu.roll(incl, d, 1), 0.0)
    start = incl - padded
    row8 = lax.broadcasted_iota(I32, cnt.shape, 0)
    tab_ref[...] = jnp.where(row8 == 0, start, jnp.where(row8 == 1, padded, 0.0)).astype(I32)
    dest = start[0:1, :] + rank_ref[...]
    gate = gate_ref[...]
    cur = sel_ref[...]
    lane = lax.broadcasted_iota(I32, cur.shape, 1)
    d4 = jnp.zeros(cur.shape, F32)
    g4 = jnp.zeros(cur.shape, F32)
    for k in range(TOP_K):
        first = jnp.min(jnp.where(cur > 0.5, lane, LANES), axis=1, keepdims=True)
        pick = lane == first
        dk = jnp.sum(jnp.where(pick, dest, 0.0), axis=1, keepdims=True)
        gk = jnp.sum(jnp.where(pick, gate, 0.0), axis=1, keepdims=True)
        d4 = jnp.where(lane == k, dk, d4)
        g4 = jnp.where(lane == k, gk, g4)
        cur = jnp.where(pick, 0.0, cur)
    dest_ref[...] = d4.astype(I32)
    g4_ref[...] = g4


def _moe_dest(sel, gate, rank, cnt):
    t = sel.shape[0]
    tp = _pick(t, (256, 128))
    row = lambda: pl.BlockSpec((tp, LANES), lambda i: (i, 0))
    one = lambda: pl.BlockSpec((8, LANES), lambda i: (0, 0))
    return pl.pallas_call(
        _moe_dest_kernel,
        out_shape=[jax.ShapeDtypeStruct((t, LANES), I32), jax.ShapeDtypeStruct((t, LANES), F32),
                   jax.ShapeDtypeStruct((8, LANES), I32)],
        grid=(t // tp,),
        in_specs=[row(), row(), row(), one()],
        out_specs=[row(), row(), one()],
        compiler_params=_cparams(("arbitrary",)),
        name="moe_dest",
    )(sel, gate, rank, cnt)


def _moe_units_kernel(tab_ref, unit_ref, *, n_units):
    rcb = MOE_RC // MOE_RB

    def per_expert(e, state):
        u0, _ = state
        nb = tab_ref[1, e] // MOE_RB
        b0 = tab_ref[0, e] // MOE_RB
        nu = (nb + rcb - 1) // rcb

        def per_unit(j, carry):
            unit_ref[0, u0 + j] = e
            unit_ref[1, u0 + j] = b0 + j * rcb
            unit_ref[2, u0 + j] = jnp.minimum(nb - j * rcb, rcb)
            return carry

        lax.fori_loop(0, nu, per_unit, 0)
        return u0 + nu, jnp.where(nu > 0, e, state[1])

    used, last = lax.fori_loop(0, N_EXPERTS, per_expert, (jnp.int32(0), jnp.int32(0)))

    def fill(u, carry):
        unit_ref[0, u] = last
        unit_ref[1, u] = 0
        unit_ref[2, u] = 0
        return carry

    lax.fori_loop(used, n_units, fill, 0)


def _moe_units(tab, n_units):
    return pl.pallas_call(
        functools.partial(_moe_units_kernel, n_units=n_units),
        out_shape=jax.ShapeDtypeStruct((3, n_units), I32),
        in_specs=[pl.BlockSpec(memory_space=pltpu.SMEM)],
        out_specs=pl.BlockSpec(memory_space=pltpu.SMEM),
        name="moe_units",
    )(tab)


def _moe_scatter_kernel(dest_ref, x_ref, xs_ref, sem):
    tm = x_ref.shape[0]

    def issue(r, carry):
        for k in range(TOP_K):
            pltpu.make_async_copy(x_ref.at[pl.ds(r, 1)], xs_ref.at[pl.ds(dest_ref[r * TOP_K + k], 1)],
                                  sem).start()
        return carry

    lax.fori_loop(0, tm, issue, 0)
    for _ in range(TOP_K):
        pltpu.make_async_copy(x_ref, xs_ref.at[pl.ds(0, tm)], sem).wait()


def _moe_scatter(dest_flat, x1, n_slots):
    t, d = x1.shape
    tm = _pick(t, (256, 128))
    return pl.pallas_call(
        _moe_scatter_kernel,
        out_shape=jax.ShapeDtypeStruct((n_slots, d), F32),
        grid=(t // tm,),
        in_specs=[pl.BlockSpec((tm * TOP_K,), lambda i: (i,), memory_space=pltpu.SMEM),
                  pl.BlockSpec((tm, d), lambda i: (i, 0))],
        out_specs=pl.BlockSpec(memory_space=pl.ANY),
        scratch_shapes=[pltpu.SemaphoreType.DMA(())],
        compiler_params=_cparams(("arbitrary",)),
        name="moe_scatter",
    )(dest_flat, x1)


def _moe_expert_kernel(unit_ref, xs_ref, wg_ref, wl_ref, bg_ref, bl_ref, wd_ref, bd_ref, ys_ref,
                       x_scr, act_scr, xst, yst1, yst2, wg_bf, wl_bf, wd_bf, xsem, ysem1, ysem2, *, ju):
    u = pl.program_id(0)
    j = pl.program_id(1)
    b0 = unit_ref[1, u]
    ns = unit_ref[2, u]
    rb = MOE_RB

    def x_copy(s, slot):
        return pltpu.make_async_copy(xs_ref.at[pl.ds((b0 + s) * rb, rb)], xst.at[slot], xsem.at[slot])

    def rows_of(s, n=1):
        return pl.ds(s * rb, n * rb) if isinstance(s, int) else pl.ds(pl.multiple_of(s * rb, rb), n * rb)

    def up_block(s, n=1):
        xb = x_scr[rows_of(s, n), :]
        hg = jnp.dot(xb, wg_bf[...], preferred_element_type=F32) + bg_ref[...]
        hl = jnp.dot(xb, wl_bf[...], preferred_element_type=F32) + bl_ref[...]
        glu = jnp.minimum(hg, SWIGLU_LIMIT)
        lin = jnp.clip(hl, -SWIGLU_LIMIT, SWIGLU_LIMIT)
        act = glu * _sigmoid(SWIGLU_ALPHA * glu) * (lin + 1.0)
        act_scr[rows_of(s, n), pl.ds(pl.multiple_of(j * MOE_TF, MOE_TF), MOE_TF)] = act.astype(BF16)

    n_pair = (ns - 1) // 2
    has_tail = (ns - 1) % 2 == 1

    def cast_up_weights():
        wg_bf[...] = wg_ref[...].astype(BF16)
        wl_bf[...] = wl_ref[...].astype(BF16)

    @pl.when(jnp.logical_and(ns > 0, j == 0))
    def _():
        x_copy(0, 0).start()
        cast_up_weights()

        @pl.when(ns > 1)
        def _():
            x_copy(1, 1).start()

        x_copy(0, 0).wait()
        x_scr[rows_of(0), :] = xst[0].astype(BF16)
        up_block(0)

        def body(s, carry):
            slot = s % 2

            @pl.when(s + 1 < ns)
            def _():
                x_copy(s + 1, 1 - slot).start()

            x_copy(s, slot).wait()
            x_scr[rows_of(s), :] = xst[slot].astype(BF16)
            up_block(s)
            return carry

        lax.fori_loop(1, ns, body, 0)

    @pl.when(jnp.logical_and(ns > 0, jnp.logical_and(j > 0, j < ju)))
    def _():
        cast_up_weights()
        up_block(0)

        def body(p, carry):
            up_block(1 + 2 * p, 2)
            return carry

        lax.fori_loop(0, n_pair, body, 0)

        @pl.when(has_tail)
        def _():
            up_block(ns - 1)

    @pl.when(jnp.logical_and(ns > 0, j >= ju))
    def _():
        col = pl.multiple_of((j - ju) * MOE_TN, MOE_TN)

        def y1_copy(s, slot):
            return pltpu.make_async_copy(
                yst1.at[slot], ys_ref.at[pl.ds((b0 + s) * rb, rb), pl.ds(col, MOE_TN)], ysem1.at[slot])

        def y2_copy(p, slot):
            return pltpu.make_async_copy(
                yst2.at[slot], ys_ref.at[pl.ds((b0 + 1 + 2 * p) * rb, 2 * rb), pl.ds(col, MOE_TN)],
                ysem2.at[slot])

        def down(s, n=1):
            return jnp.dot(act_scr[rows_of(s, n), :], wd_bf[...], preferred_element_type=F32) + bd_ref[...]

        wd_bf[...] = wd_ref[...].astype(BF16)
        yst1[0] = down(0)
        y1_copy(0, 0).start()

        def body(p, carry):
            slot = p % 2

            @pl.when(p >= 2)
            def _():
                y2_copy(p - 2, slot).wait()

            yst2[slot] = down(1 + 2 * p, 2)
            y2_copy(p, slot).start()
            return carry

        lax.fori_loop(0, n_pair, body, 0)

        @pl.when(has_tail)
        def _():
            yst1[1] = down(ns - 1)
            y1_copy(ns - 1, 1).start()
            y1_copy(ns - 1, 1).wait()

        @pl.when(n_pair >= 2)
        def _():
            y2_copy(n_pair - 2, n_pair % 2).wait()

        @pl.when(n_pair >= 1)
        def _():
            y2_copy(n_pair - 1, (n_pair - 1) % 2).wait()

        y1_copy(0, 0).wait()


def _moe_experts(units, xs, w_up, b_up, w_down, b_down):
    n_slots, d = xs.shape
    n_units = units.shape[1]
    dff = w_down.shape[1]
    ju, jd = dff // MOE_TF, d // MOE_TN

    def up_idx(off):
        return lambda u, j, un: (un[0, u], 0, off + jnp.where(un[2, u] > 0, jnp.minimum(j, ju - 1), ju - 1))

    def dn_idx(u, j, un):
        return (un[0, u], 0, jnp.where(un[2, u] > 0, jnp.maximum(j - ju, 0), jd - 1))

    return pl.pallas_call(
        functools.partial(_moe_expert_kernel, ju=ju),
        out_shape=jax.ShapeDtypeStruct((n_slots, d), F32),
        grid_spec=pltpu.PrefetchScalarGridSpec(
            num_scalar_prefetch=1, grid=(n_units, ju + jd),
            in_specs=[pl.BlockSpec(memory_space=pl.ANY),
                      pl.BlockSpec((None, d, MOE_TF), up_idx(0)),
                      pl.BlockSpec((None, d, MOE_TF), up_idx(ju)),
                      pl.BlockSpec((None, 1, MOE_TF), up_idx(0)),
                      pl.BlockSpec((None, 1, MOE_TF), up_idx(ju)),
                      pl.BlockSpec((None, dff, MOE_TN), dn_idx),
                      pl.BlockSpec((None, 1, MOE_TN), dn_idx)],
            out_specs=pl.BlockSpec(memory_space=pl.ANY),
            scratch_shapes=[pltpu.VMEM((MOE_RC, d), BF16), pltpu.VMEM((MOE_RC, dff), BF16),
                            pltpu.VMEM((2, MOE_RB, d), F32), pltpu.VMEM((2, MOE_RB, MOE_TN), F32),
                            pltpu.VMEM((2, 2 * MOE_RB, MOE_TN), F32),
                            pltpu.VMEM((d, MOE_TF), BF16), pltpu.VMEM((d, MOE_TF), BF16),
                            pltpu.VMEM((dff, MOE_TN), BF16),
                            pltpu.SemaphoreType.DMA((2,)), pltpu.SemaphoreType.DMA((2,)),
                            pltpu.SemaphoreType.DMA((2,))]),
        compiler_params=_cparams(("arbitrary", "arbitrary")),
        name="moe_experts",
    )(units, xs, w_up, w_up, b_up, b_up, w_down, b_down)


def _moe_combine_kernel(dest_ref, g4_ref, x1_ref, ys_ref, p_ref, wg_ref, wp_ref, g_ref, b_ref,
                        y_ref, gbuf, sem, *, alpha):
    tm = x1_ref.shape[0]

    def issue(r, carry):
        for k in range(TOP_K):
            pltpu.make_async_copy(ys_ref.at[pl.ds(dest_ref[r * TOP_K + k], 1)],
                                  gbuf.at[k, pl.ds(r, 1)], sem).start()
        return carry

    lax.fori_loop(0, tm, issue, 0)
    for k in range(TOP_K):
        pltpu.make_async_copy(ys_ref.at[pl.ds(0, tm)], gbuf.at[k], sem).wait()
    g4 = g4_ref[...]
    moe = g4[:, 0:1] * gbuf[0]
    for k in range(1, TOP_K):
        moe = moe + g4[:, k:k + 1] * gbuf[k]
    x2 = _layer_norm(alpha * x1_ref[...] + moe, g_ref[...], b_ref[...])
    gate = _sigmoid(jnp.dot(x2.astype(BF16), wg_ref[...], preferred_element_type=F32))
    y_ref[...] = x2 + gate * jnp.dot(p_ref[...].astype(BF16), wp_ref[...], preferred_element_type=F32)


def _moe_combine(dest_flat, g4, x1, ys, p, w_gate, w_proj, g, b, alpha, row0, t):
    d = x1.shape[1]
    tm = _pick(t, (128,))
    r0 = row0 // tm
    pd = p.shape[1]
    return pl.pallas_call(
        functools.partial(_moe_combine_kernel, alpha=alpha),
        out_shape=jax.ShapeDtypeStruct((t, d), F32),
        grid=(t // tm,),
        in_specs=[pl.BlockSpec((tm * TOP_K,), lambda i: (r0 + i,), memory_space=pltpu.SMEM),
                  pl.BlockSpec((tm, LANES), lambda i: (r0 + i, 0)),
                  pl.BlockSpec((tm, d), lambda i: (r0 + i, 0)),
                  pl.BlockSpec(memory_space=pl.ANY),
                  pl.BlockSpec((tm, pd), lambda i: (i, 0)),
                  _const_spec((d, d)), _const_spec((pd, d)), _const_spec((1, d)), _const_spec((1, d))],
        out_specs=pl.BlockSpec((tm, d), lambda i: (i, 0)),
        scratch_shapes=[pltpu.VMEM((TOP_K, tm, d), F32), pltpu.SemaphoreType.DMA(())],
        compiler_params=_cparams(("arbitrary",)),
        name="moe_combine",
    )(dest_flat, g4, x1, ys, p, w_gate, w_proj, g, b)


def _project_all(x, w, pos, prompt):
    t128, t64, tkw = _rope_tables(pos)
    xb = x.astype(BF16)
    dq = N_HEADS * HEAD_DIM
    dkv = N_KV * HEAD_DIM
    dqi = IDX_HEADS * IDX_DIM
    d = x.shape[1]
    (q,) = _proj(xb, w, 0, dq, t128, "rope128", ((BF16, None),), scale=LOG2_E * HEAD_DIM ** -0.5)
    k32, kb = _proj(xb, w, dq, dkv, t128, "rope128",
                    ((F32, None), (BF16, "transposed" if prompt else None)))
    v32, vb = _proj(xb, w, dq + dkv, dkv, (), "plain",
                    ((F32, None), (BF16, "with_ones" if prompt else None)))
    (qi,) = _proj(xb, w, dq + 2 * dkv, dqi, t64, "rope64", ((BF16, None),))
    c0 = dq + 2 * dkv + dqi
    (kw,) = _proj(xb, w, c0, LANES, tkw, "rope64", ((F32, None),))
    (zr,) = _proj(xb, w, c0, 4 * d, (), "plain", ((F32, None),), shift=IDX_DIM + IDX_HEADS)
    return q, k32, kb, v32, vb, qi, zr, kw


def kernel(x_prompt, x_sample, cache_k, cache_v, cache_kidx, state_h, state_conv, page_table, p_prompt, p_sample, w_in, conv_w, conv_b, rg_wa, rg_ba, rg_wx, rg_bx, rg_lambda, w_o_rnn, w_o_att, w_out, ln1_g, ln1_b, w_router, b_router, w_up, b_up, w_down, b_down, ln2_g, ln2_b, w_ple_gate, w_ple_proj):
    bsz, s, d = x_prompt.shape
    n, dec_t, _ = x_sample.shape
    assert bsz == 1 and dec_t == 1 and s % LANES == 0 and n % LANES == 0
    depth = w_in.shape[0]
    npg = page_table.shape[1]
    n_phys, page = cache_k.shape[1], cache_k.shape[2]
    past = npg * page
    alpha = (2 * depth) ** 0.25
    dq, dkv, dqi = N_HEADS * HEAD_DIM, N_KV * HEAD_DIM, IDX_HEADS * IDX_DIM
    t_all = s + n
    n_slots, n_units = _moe_sizes(t_all)
    pos_p = jnp.arange(s, dtype=I32)
    pos_s = jnp.full((n,), past, I32)
    row = lambda a: a[None, :]

    hp, hs = x_prompt[0], x_sample[:, 0]
    st_p = [[], [], [], [], []]
    st_s = [[], [], [], [], []]
    for i in range(depth):
        wa, wx = rg_wa[i].astype(BF16), rg_wx[i].astype(BF16)
        rnn_w = (conv_w[i], row(conv_b[i]), wa, wx, row(rg_ba[i]), row(rg_bx[i]), row(rg_lambda[i]))
        wor, woa, wo = w_o_rnn[i].astype(BF16), w_o_att[i].astype(BF16), w_out[i].astype(BF16)
        wr = jnp.pad(w_router[i], ((0, 0), (0, LANES - N_EXPERTS)))
        wr_hi = wr.astype(BF16)
        wr = jnp.concatenate([wr_hi, (wr - wr_hi.astype(F32)).astype(BF16)], axis=1)
        br = jnp.pad(b_router[i], (0, LANES - N_EXPERTS))[None, :]
        ln1 = (row(ln1_g[i]), row(ln1_b[i]))

        q, k32, kt, v32, vx, qi, zr, kw = _project_all(hp, w_in[i], pos_p, True)
        y_rnn, h_p = _rnn_prompt(zr, s, *rnn_w)
        kit = kw[:, :IDX_DIM].T.astype(BF16)
        y_att = _attn_prompt(qi, kw, q, jnp.concatenate([kit, kit], 0), kt, vx, s,
                             min(TOPK_MAX, s // 4))
        m = _merge(y_rnn, y_att, wor, woa, zr)
        bufs = _ln1_router(m, hp, wo, *ln1, wr, br, alpha, 0, t_all)
        st = (k32.reshape(1, s, N_KV, HEAD_DIM), v32.reshape(1, s, N_KV, HEAD_DIM),
              kw[None, :, :IDX_DIM], h_p, zr[None, s - (CONV_W - 1):s, :d])
        for lst, a in zip(st_p, st):
            lst.append(a)

        q, k32, kb, v32, vb, qi, zr, kw = _project_all(hs, w_in[i], pos_s, False)
        y_rnn, h_s = _rnn_sample(zr, state_conv[i], state_h[i], *rnn_w)
        sc = _idx_sample(page_table, qi.reshape(n, IDX_HEADS, IDX_DIM),
                         kw[:, IDX_DIM:IDX_DIM + IDX_HEADS, None], kw[:, None, :], cache_kidx[i])
        mask = _mask_sample(sc[:, 0], past + 1, min(TOPK_MAX, (past + 1) // 4))
        y_att = _attn_sample(page_table, q.reshape(n, N_HEADS, HEAD_DIM),
                             jnp.repeat(mask[:, :past], N_KV, axis=1)[:, None, :],
                             mask[:, None, past:past + LANES],
                             kb.reshape(n, N_KV, HEAD_DIM), vb.reshape(n, N_KV, HEAD_DIM),
                             cache_k[i].reshape(n_phys, page * N_KV, HEAD_DIM),
                             cache_v[i].reshape(n_phys, page * N_KV, HEAD_DIM)).reshape(n, dq)
        m = _merge(y_rnn, y_att, wor, woa, zr)
        x1, sel, gate = _ln1_router(m, hs, wo, *ln1, wr, br, alpha, s, t_all, prev=bufs)
        st = (k32.reshape(n, 1, N_KV, HEAD_DIM), v32.reshape(n, 1, N_KV, HEAD_DIM),
              kw[:, None, :IDX_DIM], h_s,
              jnp.concatenate([state_conv[i][:, 1:], zr[:, None, :d]], axis=1))
        for lst, a in zip(st_s, st):
            lst.append(a)

        rank, cnt = _moe_rank(sel)
        dest4, g4, tab = _moe_dest(sel, gate, rank, cnt)
        units = _moe_units(tab, n_units)
        dest_flat = dest4[:, :TOP_K].reshape(-1)
        xs = _moe_scatter(dest_flat, x1, n_slots)
        ys = _moe_experts(units, xs, w_up[i], b_up[i][:, None, :], w_down[i], b_down[i][:, None, :])
        tail = (w_ple_gate[i].astype(BF16), w_ple_proj[i].astype(BF16), row(ln2_g[i]), row(ln2_b[i]),
                alpha)
        hp = _moe_combine(dest_flat, g4, x1, ys, p_prompt[i][0], *tail, 0, s)
        hs = _moe_combine(dest_flat, g4, x1, ys, p_sample[i][:, 0], *tail, s, n)

    outs_p = [jnp.stack(l) for l in st_p]
    outs_s = [jnp.stack(l) for l in st_s]
    return (hp[None], hs[:, None], *outs_p, *outs_s)
```

```python
import functools

import jax
import jax.numpy as jnp
from jax import lax
from jax.experimental import pallas as pl
from jax.experimental.pallas import tpu as pltpu

F32 = jnp.float32
BF16 = jnp.bfloat16
I32 = jnp.int32

N_HEADS = 16
HEAD_DIM = 128
N_KV = 4
Q_PER_KV = N_HEADS // N_KV
IDX_HEADS = 16
IDX_DIM = 64
TOPK_MAX = 256
Q_BLOCK = 128
ROPE_THETA = 10000.0
RNN_BW = 128
CONV_W = 4
RG_C = 8.0
N_EXPERTS = 32
TOP_K = 4
SWIGLU_LIMIT = 7.0
SWIGLU_ALPHA = 1.702
LN_EPS = 1e-5
LANES = 128
INT_MIN = -(2 ** 31)
NEG_BIG = -1e30
LOG2_E = 1.4426950408889634
VMEM_LIMIT = 56 * 1024 * 1024


def _pick(n, cands):
    for c in cands:
        if n % c == 0:
            return c
    return n


def _cparams(sem):
    return pltpu.CompilerParams(dimension_semantics=sem, vmem_limit_bytes=VMEM_LIMIT)


def _const_spec(shape):
    nd = len(shape)
    return pl.BlockSpec(shape, lambda *a: (0,) * nd, pipeline_mode=pl.Buffered(1))


def _proj_kernel(*refs, mode, kinds, shift, n_tab, scale):
    n_w = 2 if shift else 1
    x_ref, w_refs = refs[0], refs[1:1 + n_w]
    tabs = refs[1 + n_w:1 + n_w + n_tab]
    outs = refs[1 + n_w + n_tab:1 + n_w + n_tab + len(kinds)]
    w_bf = refs[-1]
    tn = w_bf.shape[1]

    @pl.when(pl.program_id(1) == 0)
    def _():
        for c in range(0, tn, LANES):
            lo = c + shift
            if lo + LANES <= tn:
                rows = w_refs[0][lo:lo + LANES, :]
            elif lo >= tn:
                rows = w_refs[1][lo - tn:lo - tn + LANES, :]
            else:
                rows = jnp.concatenate([w_refs[0][lo:tn, :], w_refs[1][0:lo + LANES - tn, :]], axis=0)
            w_bf[:, c:c + LANES] = rows.T.astype(BF16)

    z = jnp.dot(x_ref[...], w_bf[...], preferred_element_type=F32)
    for h in range(tn // LANES):
        zh = z[:, h * LANES:(h + 1) * LANES]
        if mode == "plain":
            r = zh
        elif mode == "rope128":
            r = zh * tabs[0][...] + pltpu.roll(zh, 64, 1) * tabs[1][...]
        else:
            r = (zh * tabs[0][...] + pltpu.roll(zh, 96, 1) * tabs[1][...]
                 + pltpu.roll(zh, 32, 1) * tabs[2][...])
        if scale != 1.0:
            r = r * scale
        for o, kind in zip(outs, kinds):
            if kind == "transposed":
                o[h * LANES:(h + 1) * LANES, :] = r.T.astype(o.dtype)
            elif kind == "with_ones":
                lane = lax.broadcasted_iota(I32, r.shape, 1)
                o[:, 2 * h * LANES:(2 * h + 1) * LANES] = r.astype(o.dtype)
                o[:, (2 * h + 1) * LANES:(2 * h + 2) * LANES] = jnp.where(lane == 0, 1.0, 0.0).astype(o.dtype)
            else:
                o[:, h * LANES:(h + 1) * LANES] = r.astype(o.dtype)


def _proj(x, wt, col0, ncols, tabs, mode, outs, scale=1.0, shift=0):
    t, k = x.shape
    tm = _pick(t, (1024, 512, 256, 128))
    tn = _pick(ncols, (1024, 512, 256, 128))
    assert col0 % tn == 0 and shift % 8 == 0 and shift < LANES
    c0 = col0 // tn
    in_specs = [pl.BlockSpec((tm, k), lambda j, i: (i, 0)),
                pl.BlockSpec((tn, k), lambda j, i: (c0 + j, 0))]
    ws = [wt]
    if shift:
        in_specs.append(pl.BlockSpec((LANES, k), lambda j, i: ((c0 + j + 1) * (tn // LANES), 0)))
        ws.append(wt)
    in_specs += [pl.BlockSpec((tm, LANES), lambda j, i: (i, 0)) for _ in tabs]
    shapes, specs = [], []
    for dt, kind in outs:
        if kind == "transposed":
            shapes.append(jax.ShapeDtypeStruct((ncols, t), dt))
            specs.append(pl.BlockSpec((tn, tm), lambda j, i: (j, i)))
        elif kind == "with_ones":
            shapes.append(jax.ShapeDtypeStruct((t, 2 * ncols), dt))
            specs.append(pl.BlockSpec((tm, 2 * tn), lambda j, i: (i, j)))
        else:
            shapes.append(jax.ShapeDtypeStruct((t, ncols), dt))
            specs.append(pl.BlockSpec((tm, tn), lambda j, i: (i, j)))
    return pl.pallas_call(
        functools.partial(_proj_kernel, mode=mode, kinds=tuple(kd for _, kd in outs), shift=shift,
                          n_tab=len(tabs), scale=scale),
        out_shape=shapes,
        grid=(ncols // tn, t // tm),
        in_specs=in_specs,
        out_specs=specs,
        scratch_shapes=[pltpu.VMEM((k, tn), BF16)],
        compiler_params=_cparams(("parallel", "arbitrary")),
        name="proj_" + mode,
    )(x, *ws, *tabs)


def _rope_tables(pos):
    posf = pos.astype(F32)[:, None]
    h128 = HEAD_DIM // 2
    inv = ROPE_THETA ** (-jnp.arange(h128, dtype=F32) / h128)
    c, s = jnp.cos(posf * inv), jnp.sin(posf * inv)
    t128 = (jnp.concatenate([c, c], 1), jnp.concatenate([-s, s], 1))
    h64 = IDX_DIM // 2
    inv = ROPE_THETA ** (-jnp.arange(h64, dtype=F32) / h64)
    c, s = jnp.cos(posf * inv), jnp.sin(posf * inv)
    z = jnp.zeros_like(s)
    c64 = jnp.concatenate([c, c], 1)
    sa64 = jnp.concatenate([-s, z], 1)
    sb64 = jnp.concatenate([z, s], 1)
    t64 = tuple(jnp.concatenate([a, a], 1) for a in (c64, sa64, sb64))
    idx_scale = (IDX_HEADS * IDX_DIM) ** -0.5
    n = pos.shape[0]
    ck = jnp.concatenate([c64, jnp.full((n, IDX_HEADS), idx_scale, F32),
                          jnp.zeros((n, LANES - IDX_DIM - IDX_HEADS), F32)], 1)
    z64 = jnp.zeros((n, LANES - IDX_DIM), F32)
    tkw = (ck, jnp.concatenate([sa64, z64], 1), jnp.concatenate([sb64, z64], 1))
    return t128, t64, tkw


def _sigmoid(x):
    return 0.5 * jnp.tanh(0.5 * x) + 0.5


def _softplus_neg(lam):
    return jnp.maximum(-lam, 0.0) + jnp.log1p(jnp.exp(-jnp.abs(lam)))


def _gelu_tanh(x):
    return 0.5 * x * (1.0 + jnp.tanh(0.7978845608028654 * (x + 0.044715 * (x * x * x))))


def _rglru_gates(xc, wa_ref, wx_ref, ba, bx, lam):
    nblk = xc.shape[1] // RNN_BW
    rs, gs = [], []
    for j in range(nblk):
        xj = xc[:, j * RNN_BW:(j + 1) * RNN_BW].astype(BF16)
        rs.append(jnp.dot(xj, wa_ref[j], preferred_element_type=F32))
        gs.append(jnp.dot(xj, wx_ref[j], preferred_element_type=F32))
    r = _sigmoid(jnp.concatenate(rs, 1) + ba)
    g = _sigmoid(jnp.concatenate(gs, 1) + bx)
    log_a = -RG_C * r * _softplus_neg(lam)
    a = jnp.exp(log_a)
    mult = jnp.sqrt(jnp.tanh(-log_a) * (1.0 + a * a))
    return a, mult, g * xc


def _rnn_prompt_kernel(xr_ref, gr_ref, cw_ref, cb_ref, wa_ref, wx_ref, ba_ref, bx_ref, lam_ref,
                       y_ref, hl_ref, xbuf, hcar, a_scr, b_scr, h_scr):
    t = pl.program_id(1)
    tm, cw = xr_ref.shape

    @pl.when(t == 0)
    def _():
        xbuf[0:8, :] = jnp.zeros((8, cw), F32)
        hcar[...] = jnp.zeros_like(hcar)

    x = xr_ref[...]
    xbuf[8:8 + tm, :] = x
    w = cw_ref[...]
    xc = (cb_ref[...] + w[3:4] * x + w[2:3] * xbuf[7:7 + tm, :]
          + w[1:2] * xbuf[6:6 + tm, :] + w[0:1] * xbuf[5:5 + tm, :])
    xbuf[0:8, :] = x[tm - 8:tm, :]
    a, mult, gx = _rglru_gates(xc, wa_ref, wx_ref, ba_ref[...], bx_ref[...], lam_ref[...])
    pos = t * tm + lax.broadcasted_iota(I32, (tm, cw), 0)
    mult = jnp.where(pos == 0, 1.0, mult)
    a_scr[...] = a
    b_scr[...] = mult * gx
    row8 = lax.broadcasted_iota(I32, (8, cw), 0)

    def group(g, carry):
        r0 = pl.multiple_of(g * 8, 8)
        av = a_scr[pl.ds(r0, 8), :]
        bv = b_scr[pl.ds(r0, 8), :]
        for d in (1, 2, 4):
            a_s = pltpu.roll(av, d, 0)
            b_s = pltpu.roll(bv, d, 0)
            m = row8 >= d
            bv = jnp.where(m, av * b_s + bv, bv)
            av = jnp.where(m, av * a_s, av)
        h = av * carry + bv
        h_scr[pl.ds(r0, 8), :] = h
        return h[7:8, :]

    carry = lax.fori_loop(0, tm // 8, group, hcar[0:1, :])
    hcar[0:1, :] = carry
    y_ref[...] = (_gelu_tanh(gr_ref[...]) * h_scr[...]).astype(y_ref.dtype)

    @pl.when(t == pl.num_programs(1) - 1)
    def _():
        hl_ref[...] = carry


def _rnn_prompt(zr, s, conv_w, conv_b, wa, wx, ba, bx, lam):
    d = conv_w.shape[1]
    cw = 512
    tm = _pick(s, (256, 128))
    ncb = d // cw
    nb = cw // RNN_BW
    vec = lambda: pl.BlockSpec((1, cw), lambda c, t: (0, c))
    y, hl = pl.pallas_call(
        _rnn_prompt_kernel,
        out_shape=[jax.ShapeDtypeStruct((s, d), BF16), jax.ShapeDtypeStruct((1, d), F32)],
        grid=(ncb, s // tm),
        in_specs=[pl.BlockSpec((tm, cw), lambda c, t: (t, c)),
                  pl.BlockSpec((tm, cw), lambda c, t: (t, ncb + c)),
                  pl.BlockSpec((CONV_W, cw), lambda c, t: (0, c)),
                  vec(),
                  pl.BlockSpec((nb, RNN_BW, RNN_BW), lambda c, t: (c, 0, 0)),
                  pl.BlockSpec((nb, RNN_BW, RNN_BW), lambda c, t: (c, 0, 0)),
                  vec(), vec(), vec()],
        out_specs=[pl.BlockSpec((tm, cw), lambda c, t: (t, c)),
                   pl.BlockSpec((1, cw), lambda c, t: (0, c))],
        scratch_shapes=[pltpu.VMEM((tm + 8, cw), F32), pltpu.VMEM((8, cw), F32),
                        pltpu.VMEM((tm, cw), F32), pltpu.VMEM((tm, cw), F32),
                        pltpu.VMEM((tm, cw), F32)],
        compiler_params=_cparams(("parallel", "arbitrary")),
        name="rnn_prompt",
    )(zr, zr, conv_w, conv_b, wa, wx, ba, bx, lam)
    return y, hl


def _rnn_sample_kernel(xr_ref, gr_ref, c0_ref, c1_ref, c2_ref, h0_ref, cw_ref, cb_ref,
                       wa_ref, wx_ref, ba_ref, bx_ref, lam_ref, y_ref, h_ref):
    w = cw_ref[...]
    xc = (cb_ref[...] + w[3:4] * xr_ref[...] + w[2:3] * c2_ref[...]
          + w[1:2] * c1_ref[...] + w[0:1] * c0_ref[...])
    a, mult, gx = _rglru_gates(xc, wa_ref, wx_ref, ba_ref[...], bx_ref[...], lam_ref[...])
    h = a * h0_ref[...] + mult * gx
    h_ref[...] = h
    y_ref[...] = (_gelu_tanh(gr_ref[...]) * h).astype(y_ref.dtype)


def _rnn_sample(zr, conv_state, h0, conv_w, conv_b, wa, wx, ba, bx, lam):
    n, d = h0.shape
    cw = 512
    ncb = d // cw
    nb = cw // RNN_BW
    blk = lambda off: pl.BlockSpec((n, cw), lambda c: (0, off + c))
    vec = lambda: pl.BlockSpec((1, cw), lambda c: (0, c))
    wsp = lambda: pl.BlockSpec((nb, RNN_BW, RNN_BW), lambda c: (c, 0, 0))
    return pl.pallas_call(
        _rnn_sample_kernel,
        out_shape=[jax.ShapeDtypeStruct((n, d), BF16), jax.ShapeDtypeStruct((n, d), F32)],
        grid=(ncb,),
        in_specs=[blk(0), blk(ncb), blk(0), blk(0), blk(0), blk(0),
                  pl.BlockSpec((CONV_W, cw), lambda c: (0, c)), vec(), wsp(), wsp(),
                  vec(), vec(), vec()],
        out_specs=[blk(0), blk(0)],
        compiler_params=_cparams(("parallel",)),
        name="rnn_sample",
    )(zr, zr, conv_state[:, 0], conv_state[:, 1], conv_state[:, 2], h0,
      conv_w, conv_b, wa, wx, ba, bx, lam)


def _layer_norm(x, g, b):
    mu = jnp.mean(x, axis=-1, keepdims=True)
    xc = x - mu
    var = jnp.mean(xc * xc, axis=-1, keepdims=True)
    return xc * lax.rsqrt(var + LN_EPS) * g + b


def _merge_kernel(yr_ref, ya_ref, wr_ref, wa_ref, ga_ref, gb_ref, o_ref):
    a = jnp.dot(yr_ref[...], wr_ref[...], preferred_element_type=F32)
    b = jnp.dot(ya_ref[...], wa_ref[...], preferred_element_type=F32)
    m = _sigmoid(ga_ref[...]) * a + _sigmoid(gb_ref[...]) * b
    o_ref[...] = m.astype(o_ref.dtype)


def _merge(y_rnn, y_att, w_o_rnn, w_o_att, zr):
    t, d = y_rnn.shape
    tm = _pick(t, (512, 256, 128))
    tn = 512
    nj = d // tn
    return pl.pallas_call(
        _merge_kernel,
        out_shape=jax.ShapeDtypeStruct((t, d), BF16),
        grid=(t // tm, nj),
        in_specs=[pl.BlockSpec((tm, d), lambda i, j: (i, 0)),
                  pl.BlockSpec((tm, d), lambda i, j: (i, 0)),
                  pl.BlockSpec((d, tn), lambda i, j: (0, j)),
                  pl.BlockSpec((d, tn), lambda i, j: (0, j)),
                  pl.BlockSpec((tm, tn), lambda i, j: (i, 2 * nj + j)),
                  pl.BlockSpec((tm, tn), lambda i, j: (i, 3 * nj + j))],
        out_specs=pl.BlockSpec((tm, tn), lambda i, j: (i, j)),
        compiler_params=_cparams(("parallel", "parallel")),
        name="merge",
    )(y_rnn, y_att, w_o_rnn, w_o_att, zr, zr)


LN1_SUB_ROWS = 256


def _ln1_router_kernel(*refs, alpha, n_skip):
    m_ref, x_ref, w_ref, g_ref, b_ref, wr_ref, br_ref, x1_ref, sel_ref, gate_ref = refs[n_skip:]
    tm = x_ref.shape[0]
    sub = min(tm, LN1_SUB_ROWS)
    for r in range(0, tm, sub):
        rows = slice(r, r + sub)
        y = alpha * x_ref[rows, :] + jnp.dot(m_ref[rows, :], w_ref[...], preferred_element_type=F32)
        x1 = _layer_norm(y, g_ref[...], b_ref[...])
        x1_ref[rows, :] = x1
        x_hi = x1.astype(BF16)
        x_lo = (x1 - x_hi.astype(F32)).astype(BF16)
        part = (jnp.dot(x_hi, wr_ref[...], preferred_element_type=F32)
                + jnp.dot(x_lo, wr_ref[...], preferred_element_type=F32))
        logits = part[:, :LANES] + part[:, LANES:] + br_ref[...]
        lane = lax.broadcasted_iota(I32, logits.shape, 1)
        live = lane < N_EXPERTS
        cur = jnp.where(live, logits, -jnp.inf)
        top = jnp.max(cur, axis=1, keepdims=True)
        sel = jnp.zeros(logits.shape, jnp.bool_)
        for _ in range(TOP_K):
            mx = jnp.max(cur, axis=1, keepdims=True)
            first = jnp.min(jnp.where(cur == mx, lane, LANES), axis=1, keepdims=True)
            pick = lane == first
            sel = jnp.logical_or(sel, pick)
            cur = jnp.where(pick, -jnp.inf, cur)
        e = jnp.where(sel, jnp.exp(logits - top), 0.0)
        sel_ref[rows, :] = jnp.where(sel, 1.0, 0.0)
        gate_ref[rows, :] = e / jnp.sum(e, axis=1, keepdims=True)


def _ln1_router(m, x, w_out, g, b, w_router, b_router, alpha, row0, t_all, prev=None):
    t, d = x.shape
    tm = _pick(t, (2 * LN1_SUB_ROWS, LN1_SUB_ROWS, 128))
    assert row0 % tm == 0
    r0 = row0 // tm
    in_specs = [pl.BlockSpec((tm, d), lambda i: (i, 0)),
                pl.BlockSpec((tm, d), lambda i: (i, 0)),
                _const_spec((d, d)), _const_spec((1, d)), _const_spec((1, d)),
                _const_spec((d, 2 * LANES)), _const_spec((1, LANES))]
    args = [m, x, w_out, g, b, w_router, b_router]
    aliases = {}
    if prev is not None:
        in_specs = [pl.BlockSpec(memory_space=pl.ANY)] * 3 + in_specs
        args = list(prev) + args
        aliases = {0: 0, 1: 1, 2: 2}

    return pl.pallas_call(
        functools.partial(_ln1_router_kernel, alpha=alpha, n_skip=len(aliases)),
        out_shape=[jax.ShapeDtypeStruct((t_all, d), F32),
                   jax.ShapeDtypeStruct((t_all, LANES), F32),
                   jax.ShapeDtypeStruct((t_all, LANES), F32)],
        grid=(t // tm,),
        in_specs=in_specs,
        out_specs=[pl.BlockSpec((tm, d), lambda i: (r0 + i, 0)),
                   pl.BlockSpec((tm, LANES), lambda i: (r0 + i, 0)),
                   pl.BlockSpec((tm, LANES), lambda i: (r0 + i, 0))],
        input_output_aliases=aliases,
        compiler_params=_cparams(("parallel",)),
        name="ln1_router",
    )(*args)


def _sort_key(x):
    bits = lax.bitcast_convert_type(x, I32)
    return bits ^ (jnp.right_shift(bits, 31) & 0x7FFFFFFF)


def _count(keys_ref, nch, cw, tvec, strict):
    rows = keys_ref.shape[0]
    tb = jnp.broadcast_to(tvec, (rows, LANES))

    def body(c, cnt):
        off = c * cw
        for s in range(cw // LANES):
            k = keys_ref[:, pl.ds(pl.multiple_of(off + s * LANES, LANES), LANES)]
            hit = (k > tb) if strict else (k >= tb)
            cnt = cnt + jnp.where(hit, 1.0, 0.0)
        return cnt

    cnt = lax.fori_loop(0, nch, body, jnp.zeros((rows, LANES), F32))
    return jnp.sum(cnt, axis=1, keepdims=True)


def _select_threshold(keys_ref, nch, cw, n_sel, few):
    rows = keys_ref.shape[0]
    settled = few > 0.5

    def cond(st):
        b, _, cnt_cur = st
        open_rows = jnp.where(jnp.logical_or(settled, cnt_cur == n_sel), 0.0, 1.0)
        return jnp.logical_and(b < 32, jnp.max(open_rows) > 0.0)

    def bit_step(st):
        b, cur, cnt_cur = st
        cand = cur | lax.shift_left(jnp.int32(1), 31 - b)
        cnt = _count(keys_ref, nch, cw, cand ^ INT_MIN, False)
        take = cnt >= n_sel
        return b + 1, jnp.where(take, cand, cur), jnp.where(take, cnt, cnt_cur)

    everything = jnp.zeros((rows, 1), F32) + jnp.asarray(nch * cw, F32)
    _, cur, n_ge = lax.while_loop(cond, bit_step, (jnp.int32(0), jnp.zeros((rows, 1), I32), everything))
    t = cur ^ INT_MIN
    tied = jnp.where(jnp.logical_or(settled, n_ge <= n_sel), 0.0, 1.0)

    @pl.when(jnp.max(tied) > 0.0)
    def _():
        n_gt = _count(keys_ref, nch, cw, t, True)
        tb = jnp.broadcast_to(t, (rows, LANES))
        needb = jnp.broadcast_to(jnp.where(tied > 0.5, n_sel - n_gt, 1e9), (rows, LANES))
        r = lax.broadcasted_iota(I32, (LANES, LANES), 0)
        c = lax.broadcasted_iota(I32, (LANES, LANES), 1)
        upper = jnp.where(r < c, 1.0, 0.0).astype(BF16)

        def fix(j, run):
            sl = pl.ds(pl.multiple_of(j * LANES, LANES), LANES)
            k = keys_ref[:, sl]
            eq = k == tb
            eqf = jnp.where(eq, 1.0, 0.0)
            before = jnp.dot(eqf.astype(BF16), upper, preferred_element_type=F32) + run
            drop = jnp.logical_and(eq, before >= needb)
            keys_ref[:, sl] = jnp.where(drop, INT_MIN, k)
            return run + jnp.sum(eqf, axis=1, keepdims=True)

        lax.fori_loop(0, nch * (cw // LANES), fix, jnp.zeros((rows, 1), F32))

    return jnp.maximum(t, INT_MIN + 1)


def _attn_prompt_kernel(qi_ref, kw_ref, q_ref, kit_ref, kt_ref, vx_ref, o_ref,
                        keys_scr, lhs_scr, wb_scr, tb_scr, qg_scr, s_scr, p_scr, m_scr, acc_scr,
                        *, n_sel, kc):
    i = pl.program_id(0)
    qb = Q_BLOCK
    nch = (i * qb + qb + kc - 1) // kc
    kw = kw_ref[...]
    lane = lax.broadcasted_iota(I32, (qb, LANES), 1)
    for h in range(IDX_HEADS):
        wb_scr[h] = jnp.broadcast_to(kw[:, IDX_DIM + h:IDX_DIM + h + 1], (qb, LANES))
        blk = qi_ref[:, (h // 2) * LANES:(h // 2 + 1) * LANES]
        keep = (lane < IDX_DIM) if h % 2 == 0 else (lane >= IDX_DIM)
        lhs_scr[h] = jnp.where(keep, blk, jnp.zeros_like(blk))

    sw = min(256, kc)

    def score_chunk(c, carry):
        for s in range(kc // sw):
            o2 = pl.multiple_of(c * kc + s * sw, sw)
            kt = kit_ref[:, pl.ds(o2, sw)]
            acc = jnp.zeros((qb, sw), F32)
            for h in range(IDX_HEADS):
                sc = jnp.dot(lhs_scr[h], kt, preferred_element_type=F32)
                wb = wb_scr[h]
                acc = acc + jnp.maximum(sc, 0.0) * jnp.concatenate([wb] * (sw // LANES), axis=1)
            kpos = o2 + lax.broadcasted_iota(I32, (qb, sw), 1)
            qpos = i * qb + lax.broadcasted_iota(I32, (qb, sw), 0)
            keys_scr[:, pl.ds(o2, sw)] = jnp.where(kpos <= qpos, _sort_key(acc), INT_MIN)
        return carry

    lax.fori_loop(0, nch, score_chunk, 0)
    n_causal = i * qb + lax.broadcasted_iota(I32, (qb, 1), 0) + 1
    t = _select_threshold(keys_scr, nch, kc, n_sel, jnp.where(n_causal <= n_sel, 1.0, 0.0))
    rows = Q_PER_KV * qb
    strip = 32
    nrep = kc // LANES
    tb_scr[...] = jnp.broadcast_to(t, (qb, LANES))

    for g in range(N_KV):
        for j in range(Q_PER_KV):
            h = g * Q_PER_KV + j
            qg_scr[g, j * qb:(j + 1) * qb, :] = q_ref[:, h * HEAD_DIM:(h + 1) * HEAD_DIM]
    m_scr[...] = jnp.full(m_scr.shape, NEG_BIG, F32)
    acc_scr[...] = jnp.zeros_like(acc_scr)

    def body(c, carry):
        off = pl.multiple_of(c * kc, kc)
        for g in range(N_KV):
            s_scr[g] = jnp.dot(qg_scr[g], kt_ref[g * HEAD_DIM:(g + 1) * HEAD_DIM, pl.ds(off, kc)],
                               preferred_element_type=F32)
            for rq in range(0, qb, strip):
                tb = jnp.concatenate([tb_scr[rq:rq + strip, :]] * nrep, axis=1)
                for j in range(Q_PER_KV):
                    r = j * qb + rq
                    km = keys_scr[rq:rq + strip, pl.ds(off, kc)] >= tb
                    s = jnp.where(km, s_scr[g, r:r + strip, :], NEG_BIG)
                    s_scr[g, r:r + strip, :] = s
                    m_old = m_scr[g, r:r + strip, :]
                    m_new = jnp.maximum(m_old, jnp.max(s, axis=1, keepdims=True))
                    alpha = jnp.exp2(m_old - m_new)
                    m_scr[g, r:r + strip, :] = m_new
                    acc_scr[g, r:r + strip, :] = (acc_scr[g, r:r + strip, :]
                                                  * jnp.concatenate([alpha] * 2, axis=1))
            for r in range(0, rows, strip):
                m_new = m_scr[g, r:r + strip, :]
                p = jnp.exp2(s_scr[g, r:r + strip, :] - jnp.concatenate([m_new] * nrep, axis=1))
                p_scr[g, r:r + strip, :] = p.astype(BF16)
            acc_scr[g] += jnp.dot(p_scr[g], vx_ref[pl.ds(off, kc), g * 2 * HEAD_DIM:(g + 1) * 2 * HEAD_DIM],
                                  preferred_element_type=F32)
        return carry

    def body_pair(c2, carry):
        body(2 * c2, carry)
        return body(2 * c2 + 1, carry)

    lax.fori_loop(0, nch // 2, body_pair, 0)

    @pl.when(nch % 2 == 1)
    def _():
        body(nch - 1, 0)

    for g in range(N_KV):
        acc = acc_scr[g]
        out = acc[:, :HEAD_DIM] / acc[:, HEAD_DIM:HEAD_DIM + 1]
        for j in range(Q_PER_KV):
            h = g * Q_PER_KV + j
            o_ref[:, h * HEAD_DIM:(h + 1) * HEAD_DIM] = out[j * qb:(j + 1) * qb].astype(o_ref.dtype)


def _attn_prompt(qi, kw, q, kit2, kt, vx, s, n_sel):
    kc = min(512, s)
    d = N_HEADS * HEAD_DIM
    rows = Q_PER_KV * Q_BLOCK
    return pl.pallas_call(
        functools.partial(_attn_prompt_kernel, n_sel=n_sel, kc=kc),
        out_shape=jax.ShapeDtypeStruct((s, d), BF16),
        grid=(s // Q_BLOCK,),
        in_specs=[pl.BlockSpec((Q_BLOCK, IDX_HEADS * IDX_DIM), lambda i: (i, 0)),
                  pl.BlockSpec((Q_BLOCK, LANES), lambda i: (i, 0)),
                  pl.BlockSpec((Q_BLOCK, d), lambda i: (i, 0)),
                  _const_spec((LANES, s)), _const_spec((N_KV * HEAD_DIM, s)),
                  _const_spec((s, N_KV * 2 * HEAD_DIM))],
        out_specs=pl.BlockSpec((Q_BLOCK, d), lambda i: (i, 0)),
        scratch_shapes=[pltpu.VMEM((Q_BLOCK, s), I32),
                        pltpu.VMEM((IDX_HEADS, Q_BLOCK, LANES), BF16),
                        pltpu.VMEM((IDX_HEADS, Q_BLOCK, LANES), F32),
                        pltpu.VMEM((Q_BLOCK, LANES), I32),
                        pltpu.VMEM((N_KV, rows, HEAD_DIM), BF16),
                        pltpu.VMEM((N_KV, rows, kc), F32),
                        pltpu.VMEM((N_KV, rows, kc), BF16),
                        pltpu.VMEM((N_KV, rows, LANES), F32),
                        pltpu.VMEM((N_KV, rows, 2 * HEAD_DIM), F32)],
        compiler_params=_cparams(("parallel",)),
        name="attn_prompt",
    )(qi, kw, q, kit2, kt, vx)


IDX_SEQS_PER_STEP = 4


def _idx_sample_kernel(pt_ref, qi_ref, w_ref, kn_ref, *refs, npg, past, group):
    pages, o_ref = refs[:group * npg], refs[group * npg]
    lane = lax.broadcasted_iota(I32, (1, LANES), 1)
    for g in range(group):
        qi = qi_ref[g]
        kp = jnp.concatenate([p[...] for p in pages[g * npg:(g + 1) * npg]], axis=1).astype(BF16)
        sc = jnp.dot(qi, kp, preferred_element_type=F32)
        w = w_ref[g]
        o_ref[g, :, 0:past] = jnp.sum(jnp.maximum(sc, 0.0) * w, axis=0, keepdims=True)
        kn = kn_ref[g][:, :IDX_DIM].astype(BF16).astype(F32)
        sn = jnp.sum(qi.astype(F32) * kn, axis=1, keepdims=True)
        new = jnp.sum(jnp.maximum(sn, 0.0) * w, axis=0, keepdims=True)
        o_ref[g, :, past:past + LANES] = jnp.where(lane == 0, new, -jnp.inf)


def _idx_sample(page_table, qi3, w3, kn3, cache_kidx_t):
    n, npg = page_table.shape
    page = cache_kidx_t.shape[2]
    past = npg * page
    group = IDX_SEQS_PER_STEP if n % IDX_SEQS_PER_STEP == 0 else 1
    page_specs = [pl.BlockSpec((None, IDX_DIM, page),
                               lambda b, pt, g=g, p=p: (pt[(b * group + g) * npg + p], 0, 0))
                  for g in range(group) for p in range(npg)]
    return pl.pallas_call(
        functools.partial(_idx_sample_kernel, npg=npg, past=past, group=group),
        out_shape=jax.ShapeDtypeStruct((n, 1, past + LANES), F32),
        grid_spec=pltpu.PrefetchScalarGridSpec(
            num_scalar_prefetch=1, grid=(n // group,),
            in_specs=[pl.BlockSpec((group, IDX_HEADS, IDX_DIM), lambda b, pt: (b, 0, 0)),
                      pl.BlockSpec((group, IDX_HEADS, 1), lambda b, pt: (b, 0, 0)),
                      pl.BlockSpec((group, 1, LANES), lambda b, pt: (b, 0, 0))] + page_specs,
            out_specs=pl.BlockSpec((group, 1, past + LANES), lambda b, pt: (b, 0, 0))),
        compiler_params=_cparams(("parallel",)),
        name="idx_sample",
    )(page_table.reshape(-1), qi3, w3, kn3, *([cache_kidx_t] * (group * npg)))


def _mask_sample_kernel(sc_ref, m_ref, keys_scr, *, n_valid, n_sel):
    rows, width = sc_ref.shape
    pos = lax.broadcasted_iota(I32, (rows, width), 1)
    keys_scr[...] = jnp.where(pos < n_valid, _sort_key(sc_ref[...]), INT_MIN)
    few = jnp.full((rows, 1), 1.0 if n_valid <= n_sel else 0.0, F32)
    t = _select_threshold(keys_scr, width // LANES, LANES, n_sel, few)
    m_ref[...] = jnp.where(keys_scr[...] >= t, 1.0, 0.0)


def _mask_sample(sc, n_valid, n_sel):
    n, width = sc.shape
    return pl.pallas_call(
        functools.partial(_mask_sample_kernel, n_valid=n_valid, n_sel=n_sel),
        out_shape=jax.ShapeDtypeStruct((n, width), F32),
        scratch_shapes=[pltpu.VMEM((n, width), I32)],
        name="mask_sample",
    )(sc)


def _attn_sample_kernel(pt_ref, q_ref, m4_ref, mn_ref, kn_ref, vn_ref, *refs, npg):
    kpages, vpages, o_ref = refs[:npg], refs[npg:2 * npg], refs[2 * npg]
    q = q_ref[...]
    kp = jnp.concatenate([p[...] for p in kpages], axis=0).astype(BF16)
    s = lax.dot_general(q, kp, (((1,), (1,)), ((), ())), preferred_element_type=F32)
    grp = lax.broadcasted_iota(I32, s.shape, 0) // Q_PER_KV
    own = (lax.broadcasted_iota(I32, s.shape, 1) & (N_KV - 1)) == grp
    s = jnp.where(own, jnp.where(m4_ref[...] > 0.5, s, NEG_BIG), NEG_BIG)
    grp_h = lax.broadcasted_iota(I32, (N_HEADS, HEAD_DIM), 0) // Q_PER_KV
    kn = kn_ref[...].astype(F32)
    vn = vn_ref[...].astype(F32)
    kn_h = jnp.zeros((N_HEADS, HEAD_DIM), F32)
    vn_h = jnp.zeros((N_HEADS, HEAD_DIM), F32)
    for g in range(N_KV):
        kn_h = jnp.where(grp_h == g, kn[g:g + 1, :], kn_h)
        vn_h = jnp.where(grp_h == g, vn[g:g + 1, :], vn_h)
    sn = jnp.sum(q.astype(F32) * kn_h, axis=1, keepdims=True)
    sn = jnp.where(mn_ref[:, 0:1] > 0.5, sn, NEG_BIG)
    m = jnp.maximum(jnp.max(s, axis=1, keepdims=True), sn)
    p = jnp.exp2(s - m)
    pn = jnp.exp2(sn - m)
    l = jnp.sum(p, axis=1, keepdims=True) + pn
    vp = jnp.concatenate([r[...] for r in vpages], axis=0).astype(BF16)
    o = jnp.dot(p.astype(BF16), vp, preferred_element_type=F32)
    o_ref[...] = ((o + pn.astype(BF16).astype(F32) * vn_h) / l).astype(o_ref.dtype)


def _attn_sample(page_table, q3, mask4, mask_new, kn3, vn3, cache_k, cache_v):
    n, npg = page_table.shape
    prow = cache_k.shape[1]
    pspec = lambda p: pl.BlockSpec((None, prow, HEAD_DIM), lambda b, pt, p=p: (pt[b * npg + p], 0, 0))
    return pl.pallas_call(
        functools.partial(_attn_sample_kernel, npg=npg),
        out_shape=jax.ShapeDtypeStruct((n, N_HEADS, HEAD_DIM), BF16),
        grid_spec=pltpu.PrefetchScalarGridSpec(
            num_scalar_prefetch=1, grid=(n,),
            in_specs=[pl.BlockSpec((None, N_HEADS, HEAD_DIM), lambda b, pt: (b, 0, 0)),
                      pl.BlockSpec((None, 1, npg * prow), lambda b, pt: (b, 0, 0)),
                      pl.BlockSpec((None, 1, LANES), lambda b, pt: (b, 0, 0)),
                      pl.BlockSpec((None, N_KV, HEAD_DIM), lambda b, pt: (b, 0, 0)),
                      pl.BlockSpec((None, N_KV, HEAD_DIM), lambda b, pt: (b, 0, 0))]
            + [pspec(p) for p in range(npg)] + [pspec(p) for p in range(npg)],
            out_specs=pl.BlockSpec((None, N_HEADS, HEAD_DIM), lambda b, pt: (b, 0, 0))),
        compiler_params=_cparams(("parallel",)),
        name="attn_sample",
    )(page_table.reshape(-1), q3, mask4, mask_new, kn3, vn3, *([cache_k] * npg), *([cache_v] * npg))


MOE_RB = 128
MOE_GROUP = 4
MOE_RC = 1280
MOE_TF = 512
MOE_TN = 512


def _moe_sizes(n_tok):
    n_assign = n_tok * TOP_K
    n_slots = (n_assign // MOE_RB + N_EXPERTS) * MOE_RB
    n_units = N_EXPERTS + n_assign // MOE_RC
    return n_slots, n_units


def _moe_rank_kernel(sel_ref, rank_ref, cnt_ref, carry):
    i = pl.program_id(0)
    tp = sel_ref.shape[0]

    @pl.when(i == 0)
    def _():
        carry[...] = jnp.zeros_like(carry)

    a = sel_ref[...]
    r = lax.broadcasted_iota(I32, (tp, tp), 0)
    c = lax.broadcasted_iota(I32, (tp, tp), 1)
    lower = jnp.where(c < r, 1.0, 0.0).astype(BF16)
    rank_ref[...] = jnp.dot(lower, a.astype(BF16), preferred_element_type=F32) + carry[0:1, :]
    carry[...] = carry[...] + jnp.sum(a, axis=0, keepdims=True)
    cnt_ref[...] = carry[...]


def _moe_rank(sel):
    t = sel.shape[0]
    tp = _pick(t, (256, 128))
    return pl.pallas_call(
        _moe_rank_kernel,
        out_shape=[jax.ShapeDtypeStruct((t, LANES), F32), jax.ShapeDtypeStruct((8, LANES), F32)],
        grid=(t // tp,),
        in_specs=[pl.BlockSpec((tp, LANES), lambda i: (i, 0))],
        out_specs=[pl.BlockSpec((tp, LANES), lambda i: (i, 0)),
                   pl.BlockSpec((8, LANES), lambda i: (0, 0))],
        scratch_shapes=[pltpu.VMEM((8, LANES), F32)],
        compiler_params=_cparams(("arbitrary",)),
        name="moe_rank",
    )(sel)


def _moe_dest_kernel(sel_ref, gate_ref, rank_ref, cnt_ref, dest_ref, g4_ref, tab_ref):
    cnt = cnt_ref[...]
    lane8 = lax.broadcasted_iota(I32, cnt.shape, 1)
    padded = jnp.ceil(cnt * (1.0 / MOE_RB)) * MOE_RB
    incl = padded
    for d in (1, 2, 4, 8, 16, 32, 64):
        incl = incl + jnp.where(lane8 >= d, pltpu.roll(incl, d, 1), 0.0)
    start = incl - padded
    row8 = lax.broadcasted_iota(I32, cnt.shape, 0)
    tab_ref[...] = jnp.where(row8 == 0, start, jnp.where(row8 == 1, padded, 0.0)).astype(I32)
    dest = start[0:1, :] + rank_ref[...]
    gate = gate_ref[...]
    cur = sel_ref[...]
    lane = lax.broadcasted_iota(I32, cur.shape, 1)
    d4 = jnp.zeros(cur.shape, F32)
    g4 = jnp.zeros(cur.shape, F32)
    for k in range(TOP_K):
        first = jnp.min(jnp.where(cur > 0.5, lane, LANES), axis=1, keepdims=True)
        pick = lane == first
        dk = jnp.sum(jnp.where(pick, dest, 0.0), axis=1, keepdims=True)
        gk = jnp.sum(jnp.where(pick, gate, 0.0), axis=1, keepdims=True)
        d4 = jnp.where(lane == k, dk, d4)
        g4 = jnp.where(lane == k, gk, g4)
        cur = jnp.where(pick, 0.0, cur)
    dest_ref[...] = d4.astype(I32)
    g4_ref[...] = g4


def _moe_dest(sel, gate, rank, cnt):
    t = sel.shape[0]
    tp = _pick(t, (256, 128))
    row = lambda: pl.BlockSpec((tp, LANES), lambda i: (i, 0))
    one = lambda: pl.BlockSpec((8, LANES), lambda i: (0, 0))
    return pl.pallas_call(
        _moe_dest_kernel,
        out_shape=[jax.ShapeDtypeStruct((t, LANES), I32), jax.ShapeDtypeStruct((t, LANES), F32),
                   jax.ShapeDtypeStruct((8, LANES), I32)],
        grid=(t // tp,),
        in_specs=[row(), row(), row(), one()],
        out_specs=[row(), row(), one()],
        compiler_params=_cparams(("arbitrary",)),
        name="moe_dest",
    )(sel, gate, rank, cnt)


def _moe_units_kernel(tab_ref, unit_ref, *, n_units):
    rcb = MOE_RC // MOE_RB

    def per_expert(e, state):
        u0, _ = state
        nb = tab_ref[1, e] // MOE_RB
        b0 = tab_ref[0, e] // MOE_RB
        nu = (nb + rcb - 1) // rcb

        def per_unit(j, carry):
            unit_ref[0, u0 + j] = e
            unit_ref[1, u0 + j] = b0 + j * rcb
            unit_ref[2, u0 + j] = jnp.minimum(nb - j * rcb, rcb)
            return carry

        lax.fori_loop(0, nu, per_unit, 0)
        return u0 + nu, jnp.where(nu > 0, e, state[1])

    used, last = lax.fori_loop(0, N_EXPERTS, per_expert, (jnp.int32(0), jnp.int32(0)))

    def fill(u, carry):
        unit_ref[0, u] = last
        unit_ref[1, u] = 0
        unit_ref[2, u] = 0
        return carry

    lax.fori_loop(used, n_units, fill, 0)


def _moe_units(tab, n_units):
    return pl.pallas_call(
        functools.partial(_moe_units_kernel, n_units=n_units),
        out_shape=jax.ShapeDtypeStruct((3, n_units), I32),
        in_specs=[pl.BlockSpec(memory_space=pltpu.SMEM)],
        out_specs=pl.BlockSpec(memory_space=pltpu.SMEM),
        name="moe_units",
    )(tab)


def _moe_scatter_kernel(dest_ref, x_ref, xs_ref, sem):
    tm = x_ref.shape[0]

    def issue(r, carry):
        for k in range(TOP_K):
            pltpu.make_async_copy(x_ref.at[pl.ds(r, 1)], xs_ref.at[pl.ds(dest_ref[r * TOP_K + k], 1)],
                                  sem).start()
        return carry

    lax.fori_loop(0, tm, issue, 0)
    for _ in range(TOP_K):
        pltpu.make_async_copy(x_ref, xs_ref.at[pl.ds(0, tm)], sem).wait()


def _moe_scatter(dest_flat, x1, n_slots):
    t, d = x1.shape
    tm = _pick(t, (256, 128))
    return pl.pallas_call(
        _moe_scatter_kernel,
        out_shape=jax.ShapeDtypeStruct((n_slots, d), F32),
        grid=(t // tm,),
        in_specs=[pl.BlockSpec((tm * TOP_K,), lambda i: (i,), memory_space=pltpu.SMEM),
                  pl.BlockSpec((tm, d), lambda i: (i, 0))],
        out_specs=pl.BlockSpec(memory_space=pl.ANY),
        scratch_shapes=[pltpu.SemaphoreType.DMA(())],
        compiler_params=_cparams(("arbitrary",)),
        name="moe_scatter",
    )(dest_flat, x1)


def _moe_expert_kernel(unit_ref, xs_ref, wg_ref, wl_ref, bg_ref, bl_ref, wd_ref, bd_ref, ys_ref,
                       x_scr, act_scr, xst, yst1, yst2, ystg, wg_bf, wl_bf, wd_bf,
                       xsem, ysem1, ysem2, ysemg, *, ju):
    u = pl.program_id(0)
    j = pl.program_id(1)
    b0 = unit_ref[1, u]
    ns = unit_ref[2, u]
    rb = MOE_RB

    def x_copy(s, slot):
        return pltpu.make_async_copy(xs_ref.at[pl.ds((b0 + s) * rb, rb)], xst.at[slot], xsem.at[slot])

    def rows_of(s, n=1):
        return pl.ds(s * rb, n * rb) if isinstance(s, int) else pl.ds(pl.multiple_of(s * rb, rb), n * rb)

    def up_block(s, n=1):
        xb = x_scr[rows_of(s, n), :]
        hg = jnp.dot(xb, wg_bf[...], preferred_element_type=F32) + bg_ref[...]
        hl = jnp.dot(xb, wl_bf[...], preferred_element_type=F32) + bl_ref[...]
        glu = jnp.minimum(hg, SWIGLU_LIMIT)
        lin = jnp.clip(hl, -SWIGLU_LIMIT, SWIGLU_LIMIT)
        act = glu * _sigmoid(SWIGLU_ALPHA * glu) * (lin + 1.0)
        act_scr[rows_of(s, n), pl.ds(pl.multiple_of(j * MOE_TF, MOE_TF), MOE_TF)] = act.astype(BF16)

    n_grp = (ns - 1) // MOE_GROUP
    rem = (ns - 1) % MOE_GROUP
    has2 = rem >= 2
    has1 = rem % 2 == 1
    tail2_at = 1 + MOE_GROUP * n_grp
    tail1_at = tail2_at + jnp.where(has2, 2, 0)

    def cast_up_weights():
        wg_bf[...] = wg_ref[...].astype(BF16)
        wl_bf[...] = wl_ref[...].astype(BF16)

    @pl.when(jnp.logical_and(ns > 0, j == 0))
    def _():
        x_copy(0, 0).start()
        cast_up_weights()

        @pl.when(ns > 1)
        def _():
            x_copy(1, 1).start()

        x_copy(0, 0).wait()
        x_scr[rows_of(0), :] = xst[0].astype(BF16)
        up_block(0)

        def body(s, carry):
            slot = s % 2

            @pl.when(s + 1 < ns)
            def _():
                x_copy(s + 1, 1 - slot).start()

            x_copy(s, slot).wait()
            x_scr[rows_of(s), :] = xst[slot].astype(BF16)
            up_block(s)
            return carry

        lax.fori_loop(1, ns, body, 0)

    @pl.when(jnp.logical_and(ns > 0, jnp.logical_and(j > 0, j < ju)))
    def _():
        cast_up_weights()
        up_block(0)

        def body(p, carry):
            up_block(1 + MOE_GROUP * p, MOE_GROUP)
            return carry

        lax.fori_loop(0, n_grp, body, 0)

        @pl.when(has2)
        def _():
            up_block(tail2_at, 2)

        @pl.when(has1)
        def _():
            up_block(tail1_at)

    @pl.when(jnp.logical_and(ns > 0, j >= ju))
    def _():
        col = pl.multiple_of((j - ju) * MOE_TN, MOE_TN)

        def y_copy(stage, sem, s, n):
            return pltpu.make_async_copy(
                stage, ys_ref.at[pl.ds((b0 + s) * rb, n * rb), pl.ds(col, MOE_TN)], sem)

        def yg_copy(p, slot):
            return y_copy(ystg.at[slot], ysemg.at[slot], 1 + MOE_GROUP * p, MOE_GROUP)

        def down(s, n=1):
            return jnp.dot(act_scr[rows_of(s, n), :], wd_bf[...], preferred_element_type=F32) + bd_ref[...]

        wd_bf[...] = wd_ref[...].astype(BF16)
        yst1[0] = down(0)
        y_copy(yst1.at[0], ysem1.at[0], 0, 1).start()

        def body(p, carry):
            slot = p % 2

            @pl.when(p >= 2)
            def _():
                yg_copy(p - 2, slot).wait()

            ystg[slot] = down(1 + MOE_GROUP * p, MOE_GROUP)
            yg_copy(p, slot).start()
            return carry

        lax.fori_loop(0, n_grp, body, 0)

        @pl.when(has2)
        def _():
            yst2[...] = down(tail2_at, 2)
            y_copy(yst2, ysem2.at[0], tail2_at, 2).start()
            y_copy(yst2, ysem2.at[0], tail2_at, 2).wait()

        @pl.when(has1)
        def _():
            yst1[1] = down(tail1_at)
            y_copy(yst1.at[1], ysem1.at[1], tail1_at, 1).start()
            y_copy(yst1.at[1], ysem1.at[1], tail1_at, 1).wait()

        @pl.when(n_grp >= 2)
        def _():
            yg_copy(n_grp - 2, n_grp % 2).wait()

        @pl.when(n_grp >= 1)
        def _():
            yg_copy(n_grp - 1, (n_grp - 1) % 2).wait()

        y_copy(yst1.at[0], ysem1.at[0], 0, 1).wait()


def _moe_experts(units, xs, w_up, b_up, w_down, b_down):
    n_slots, d = xs.shape
    n_units = units.shape[1]
    dff = w_down.shape[1]
    ju, jd = dff // MOE_TF, d // MOE_TN

    def up_idx(off):
        return lambda u, j, un: (un[0, u], 0, off + jnp.where(un[2, u] > 0, jnp.minimum(j, ju - 1), ju - 1))

    def dn_idx(u, j, un):
        return (un[0, u], 0, jnp.where(un[2, u] > 0, jnp.maximum(j - ju, 0), jd - 1))

    return pl.pallas_call(
        functools.partial(_moe_expert_kernel, ju=ju),
        out_shape=jax.ShapeDtypeStruct((n_slots, d), F32),
        grid_spec=pltpu.PrefetchScalarGridSpec(
            num_scalar_prefetch=1, grid=(n_units, ju + jd),
            in_specs=[pl.BlockSpec(memory_space=pl.ANY),
                      pl.BlockSpec((None, d, MOE_TF), up_idx(0)),
                      pl.BlockSpec((None, d, MOE_TF), up_idx(ju)),
                      pl.BlockSpec((None, 1, MOE_TF), up_idx(0)),
                      pl.BlockSpec((None, 1, MOE_TF), up_idx(ju)),
                      pl.BlockSpec((None, dff, MOE_TN), dn_idx),
                      pl.BlockSpec((None, 1, MOE_TN), dn_idx)],
            out_specs=pl.BlockSpec(memory_space=pl.ANY),
            scratch_shapes=[pltpu.VMEM((MOE_RC, d), BF16), pltpu.VMEM((MOE_RC, dff), BF16),
                            pltpu.VMEM((2, MOE_RB, d), F32), pltpu.VMEM((2, MOE_RB, MOE_TN), F32),
                            pltpu.VMEM((2 * MOE_RB, MOE_TN), F32),
                            pltpu.VMEM((2, MOE_GROUP * MOE_RB, MOE_TN), F32),
                            pltpu.VMEM((d, MOE_TF), BF16), pltpu.VMEM((d, MOE_TF), BF16),
                            pltpu.VMEM((dff, MOE_TN), BF16),
                            pltpu.SemaphoreType.DMA((2,)), pltpu.SemaphoreType.DMA((2,)),
                            pltpu.SemaphoreType.DMA((1,)), pltpu.SemaphoreType.DMA((2,))]),
        compiler_params=_cparams(("arbitrary", "arbitrary")),
        name="moe_experts",
    )(units, xs, w_up, w_up, b_up, b_up, w_down, b_down)


COMBINE_SUB_ROWS = 128


def _moe_combine_kernel(dest_ref, g4_ref, x1_ref, ys_ref, p_ref, wg_ref, wp_ref, g_ref, b_ref,
                        y_ref, gbuf, sem, *, alpha):
    tm = x1_ref.shape[0]

    def issue(r, carry):
        for k in range(TOP_K):
            pltpu.make_async_copy(ys_ref.at[pl.ds(dest_ref[r * TOP_K + k], 1)],
                                  gbuf.at[k, pl.ds(r, 1)], sem).start()
        return carry

    lax.fori_loop(0, tm, issue, 0)
    for k in range(TOP_K):
        pltpu.make_async_copy(ys_ref.at[pl.ds(0, tm)], gbuf.at[k], sem).wait()
    sub = min(tm, COMBINE_SUB_ROWS)
    for r in range(0, tm, sub):
        rows = slice(r, r + sub)
        g4 = g4_ref[rows, :]
        moe = g4[:, 0:1] * gbuf[0, rows, :]
        for k in range(1, TOP_K):
            moe = moe + g4[:, k:k + 1] * gbuf[k, rows, :]
        x2 = _layer_norm(alpha * x1_ref[rows, :] + moe, g_ref[...], b_ref[...])
        gate = _sigmoid(jnp.dot(x2.astype(BF16), wg_ref[...], preferred_element_type=F32))
        y_ref[rows, :] = x2 + gate * jnp.dot(p_ref[rows, :].astype(BF16), wp_ref[...],
                                             preferred_element_type=F32)


def _moe_combine(dest_flat, g4, x1, ys, p, w_gate, w_proj, g, b, alpha, row0, t):
    d = x1.shape[1]
    tm = _pick(t, (2 * COMBINE_SUB_ROWS, COMBINE_SUB_ROWS))
    assert row0 % tm == 0
    r0 = row0 // tm
    pd = p.shape[1]
    return pl.pallas_call(
        functools.partial(_moe_combine_kernel, alpha=alpha),
        out_shape=jax.ShapeDtypeStruct((t, d), F32),
        grid=(t // tm,),
        in_specs=[pl.BlockSpec((tm * TOP_K,), lambda i: (r0 + i,), memory_space=pltpu.SMEM),
                  pl.BlockSpec((tm, LANES), lambda i: (r0 + i, 0)),
                  pl.BlockSpec((tm, d), lambda i: (r0 + i, 0)),
                  pl.BlockSpec(memory_space=pl.ANY),
                  pl.BlockSpec((tm, pd), lambda i: (i, 0)),
                  _const_spec((d, d)), _const_spec((pd, d)), _const_spec((1, d)), _const_spec((1, d))],
        out_specs=pl.BlockSpec((tm, d), lambda i: (i, 0)),
        scratch_shapes=[pltpu.VMEM((TOP_K, tm, d), F32), pltpu.SemaphoreType.DMA(())],
        compiler_params=_cparams(("arbitrary",)),
        name="moe_combine",
    )(dest_flat, g4, x1, ys, p, w_gate, w_proj, g, b)


def _project_all(x, w, pos, prompt):
    t128, t64, tkw = _rope_tables(pos)
    xb = x.astype(BF16)
    dq = N_HEADS * HEAD_DIM
    dkv = N_KV * HEAD_DIM
    dqi = IDX_HEADS * IDX_DIM
    d = x.shape[1]
    (q,) = _proj(xb, w, 0, dq, t128, "rope128", ((BF16, None),), scale=LOG2_E * HEAD_DIM ** -0.5)
    k32, kb = _proj(xb, w, dq, dkv, t128, "rope128",
                    ((F32, None), (BF16, "transposed" if prompt else None)))
    v32, vb = _proj(xb, w, dq + dkv, dkv, (), "plain",
                    ((F32, None), (BF16, "with_ones" if prompt else None)))
    (qi,) = _proj(xb, w, dq + 2 * dkv, dqi, t64, "rope64", ((BF16, None),))
    c0 = dq + 2 * dkv + dqi
    (kw,) = _proj(xb, w, c0, LANES, tkw, "rope64", ((F32, None),))
    (zr,) = _proj(xb, w, c0, 4 * d, (), "plain", ((F32, None),), shift=IDX_DIM + IDX_HEADS)
    return q, k32, kb, v32, vb, qi, zr, kw


def kernel(x_prompt, x_sample, cache_k, cache_v, cache_kidx, state_h, state_conv, page_table, p_prompt, p_sample, w_in, conv_w, conv_b, rg_wa, rg_ba, rg_wx, rg_bx, rg_lambda, w_o_rnn, w_o_att, w_out, ln1_g, ln1_b, w_router, b_router, w_up, b_up, w_down, b_down, ln2_g, ln2_b, w_ple_gate, w_ple_proj):
    bsz, s, d = x_prompt.shape
    n, dec_t, _ = x_sample.shape
    assert bsz == 1 and dec_t == 1 and s % LANES == 0 and n % LANES == 0
    depth = w_in.shape[0]
    npg = page_table.shape[1]
    n_phys, page = cache_k.shape[1], cache_k.shape[2]
    past = npg * page
    alpha = (2 * depth) ** 0.25
    dq, dkv, dqi = N_HEADS * HEAD_DIM, N_KV * HEAD_DIM, IDX_HEADS * IDX_DIM
    t_all = s + n
    n_slots, n_units = _moe_sizes(t_all)
    pos_p = jnp.arange(s, dtype=I32)
    pos_s = jnp.full((n,), past, I32)
    row = lambda a: a[None, :]

    hp, hs = x_prompt[0], x_sample[:, 0]
    st_p = [[], [], [], [], []]
    st_s = [[], [], [], [], []]
    for i in range(depth):
        wa, wx = rg_wa[i].astype(BF16), rg_wx[i].astype(BF16)
        rnn_w = (conv_w[i], row(conv_b[i]), wa, wx, row(rg_ba[i]), row(rg_bx[i]), row(rg_lambda[i]))
        wor, woa, wo = w_o_rnn[i].astype(BF16), w_o_att[i].astype(BF16), w_out[i].astype(BF16)
        wr = jnp.pad(w_router[i], ((0, 0), (0, LANES - N_EXPERTS)))
        wr_hi = wr.astype(BF16)
        wr = jnp.concatenate([wr_hi, (wr - wr_hi.astype(F32)).astype(BF16)], axis=1)
        br = jnp.pad(b_router[i], (0, LANES - N_EXPERTS))[None, :]
        ln1 = (row(ln1_g[i]), row(ln1_b[i]))

        w_in_t = jnp.swapaxes(w_in[i], 0, 1)
        q, k32, kt, v32, vx, qi, zr, kw = _project_all(hp, w_in_t, pos_p, True)
        y_rnn, h_p = _rnn_prompt(zr, s, *rnn_w)
        kit = kw[:, :IDX_DIM].T.astype(BF16)
        y_att = _attn_prompt(qi, kw, q, jnp.concatenate([kit, kit], 0), kt, vx, s,
                             min(TOPK_MAX, s // 4))
        m = _merge(y_rnn, y_att, wor, woa, zr)
        bufs = _ln1_router(m, hp, wo, *ln1, wr, br, alpha, 0, t_all)
        st = (k32.reshape(1, s, N_KV, HEAD_DIM), v32.reshape(1, s, N_KV, HEAD_DIM),
              kw[None, :, :IDX_DIM], h_p, zr[None, s - (CONV_W - 1):s, :d])
        for lst, a in zip(st_p, st):
            lst.append(a)

        q, k32, kb, v32, vb, qi, zr, kw = _project_all(hs, w_in_t, pos_s, False)
        y_rnn, h_s = _rnn_sample(zr, state_conv[i], state_h[i], *rnn_w)
        sc = _idx_sample(page_table, qi.reshape(n, IDX_HEADS, IDX_DIM),
                         kw[:, IDX_DIM:IDX_DIM + IDX_HEADS, None], kw[:, None, :],
                         jnp.swapaxes(cache_kidx[i], 1, 2))
        mask = _mask_sample(sc[:, 0], past + 1, min(TOPK_MAX, (past + 1) // 4))
        y_att = _attn_sample(page_table, q.reshape(n, N_HEADS, HEAD_DIM),
                             jnp.repeat(mask[:, :past], N_KV, axis=1)[:, None, :],
                             mask[:, None, past:past + LANES],
                             kb.reshape(n, N_KV, HEAD_DIM), vb.reshape(n, N_KV, HEAD_DIM),
                             cache_k[i].reshape(n_phys, page * N_KV, HEAD_DIM),
                             cache_v[i].reshape(n_phys, page * N_KV, HEAD_DIM)).reshape(n, dq)
        m = _merge(y_rnn, y_att, wor, woa, zr)
        x1, sel, gate = _ln1_router(m, hs, wo, *ln1, wr, br, alpha, s, t_all, prev=bufs)
        st = (k32.reshape(n, 1, N_KV, HEAD_DIM), v32.reshape(n, 1, N_KV, HEAD_DIM),
              kw[:, None, :IDX_DIM], h_s,
              jnp.concatenate([state_conv[i][:, 1:], zr[:, None, :d]], axis=1))
        for lst, a in zip(st_s, st):
            lst.append(a)

        rank, cnt = _moe_rank(sel)
        dest4, g4, tab = _moe_dest(sel, gate, rank, cnt)
        units = _moe_units(tab, n_units)
        dest_flat = dest4[:, :TOP_K].reshape(-1)
        xs = _moe_scatter(dest_flat, x1, n_slots)
        ys = _moe_experts(units, xs, w_up[i], b_up[i][:, None, :], w_down[i], b_down[i][:, None, :])
        tail = (w_ple_gate[i].astype(BF16), w_ple_proj[i].astype(BF16), row(ln2_g[i]), row(ln2_b[i]),
                alpha)
        hp = _moe_combine(dest_flat, g4, x1, ys, p_prompt[i][0], *tail, 0, s)
        hs = _moe_combine(dest_flat, g4, x1, ys, p_sample[i][:, 0], *tail, s, n)

    outs_p = [jnp.stack(l) for l in st_p]
    outs_s = [jnp.stack(l) for l in st_s]
    return (hp[None], hs[:, None], *outs_p, *outs_s)
```

```python
import functools

import jax
import jax.numpy as jnp
from jax import lax
from jax.experimental import pallas as pl
from jax.experimental.pallas import tpu as pltpu

F32 = jnp.float32
BF16 = jnp.bfloat16
I32 = jnp.int32

N_HEADS = 16
HEAD_DIM = 128
N_KV = 4
Q_PER_KV = N_HEADS // N_KV
IDX_HEADS = 16
IDX_DIM = 64
TOPK_MAX = 256
Q_BLOCK = 128
ROPE_THETA = 10000.0
RNN_BW = 128
CONV_W = 4
RG_C = 8.0
N_EXPERTS = 32
TOP_K = 4
SWIGLU_LIMIT = 7.0
SWIGLU_ALPHA = 1.702
LN_EPS = 1e-5
LANES = 128
INT_MIN = -(2 ** 31)
NEG_BIG = -1e30
LOG2_E = 1.4426950408889634
VMEM_LIMIT = 56 * 1024 * 1024


def _pick(n, cands):
    for c in cands:
        if n % c == 0:
            return c
    return n


def _cparams(sem):
    return pltpu.CompilerParams(dimension_semantics=sem, vmem_limit_bytes=VMEM_LIMIT)


def _const_spec(shape):
    nd = len(shape)
    return pl.BlockSpec(shape, lambda *a: (0,) * nd, pipeline_mode=pl.Buffered(1))


def _proj_kernel(*refs, mode, kinds, shift, n_tab, scale):
    n_w = 2 if shift else 1
    x_ref, w_refs = refs[0], refs[1:1 + n_w]
    tabs = refs[1 + n_w:1 + n_w + n_tab]
    outs = refs[1 + n_w + n_tab:1 + n_w + n_tab + len(kinds)]
    w_bf = refs[-1]
    tn = w_bf.shape[1]

    @pl.when(pl.program_id(1) == 0)
    def _():
        for c in range(0, tn, LANES):
            lo = c + shift
            if lo + LANES <= tn:
                rows = w_refs[0][lo:lo + LANES, :]
            elif lo >= tn:
                rows = w_refs[1][lo - tn:lo - tn + LANES, :]
            else:
                rows = jnp.concatenate([w_refs[0][lo:tn, :], w_refs[1][0:lo + LANES - tn, :]], axis=0)
            w_bf[:, c:c + LANES] = rows.T.astype(BF16)

    z = jnp.dot(x_ref[...], w_bf[...], preferred_element_type=F32)
    for h in range(tn // LANES):
        zh = z[:, h * LANES:(h + 1) * LANES]
        if mode == "plain":
            r = zh
        elif mode == "rope128":
            r = zh * tabs[0][...] + pltpu.roll(zh, 64, 1) * tabs[1][...]
        else:
            r = (zh * tabs[0][...] + pltpu.roll(zh, 96, 1) * tabs[1][...]
                 + pltpu.roll(zh, 32, 1) * tabs[2][...])
        if scale != 1.0:
            r = r * scale
        for o, kind in zip(outs, kinds):
            if kind == "transposed":
                o[h * LANES:(h + 1) * LANES, :] = r.T.astype(o.dtype)
            elif kind == "with_ones":
                lane = lax.broadcasted_iota(I32, r.shape, 1)
                o[:, 2 * h * LANES:(2 * h + 1) * LANES] = r.astype(o.dtype)
                o[:, (2 * h + 1) * LANES:(2 * h + 2) * LANES] = jnp.where(lane == 0, 1.0, 0.0).astype(o.dtype)
            else:
                o[:, h * LANES:(h + 1) * LANES] = r.astype(o.dtype)


def _proj(x, wt, col0, ncols, tabs, mode, outs, scale=1.0, shift=0):
    t, k = x.shape
    tm = _pick(t, (1024, 512, 256, 128))
    tn = _pick(ncols, (1024, 512, 256, 128))
    assert col0 % tn == 0 and shift % 8 == 0 and shift < LANES
    c0 = col0 // tn
    in_specs = [pl.BlockSpec((tm, k), lambda j, i: (i, 0)),
                pl.BlockSpec((tn, k), lambda j, i: (c0 + j, 0))]
    ws = [wt]
    if shift:
        in_specs.append(pl.BlockSpec((LANES, k), lambda j, i: ((c0 + j + 1) * (tn // LANES), 0)))
        ws.append(wt)
    in_specs += [pl.BlockSpec((tm, LANES), lambda j, i: (i, 0)) for _ in tabs]
    shapes, specs = [], []
    for dt, kind in outs:
        if kind == "transposed":
            shapes.append(jax.ShapeDtypeStruct((ncols, t), dt))
            specs.append(pl.BlockSpec((tn, tm), lambda j, i: (j, i)))
        elif kind == "with_ones":
            shapes.append(jax.ShapeDtypeStruct((t, 2 * ncols), dt))
            specs.append(pl.BlockSpec((tm, 2 * tn), lambda j, i: (i, j)))
        else:
            shapes.append(jax.ShapeDtypeStruct((t, ncols), dt))
            specs.append(pl.BlockSpec((tm, tn), lambda j, i: (i, j)))
    return pl.pallas_call(
        functools.partial(_proj_kernel, mode=mode, kinds=tuple(kd for _, kd in outs), shift=shift,
                          n_tab=len(tabs), scale=scale),
        out_shape=shapes,
        grid=(ncols // tn, t // tm),
        in_specs=in_specs,
        out_specs=specs,
        scratch_shapes=[pltpu.VMEM((k, tn), BF16)],
        compiler_params=_cparams(("parallel", "arbitrary")),
        name="proj_" + mode,
    )(x, *ws, *tabs)


def _rope_tables(pos):
    posf = pos.astype(F32)[:, None]
    h128 = HEAD_DIM // 2
    inv = ROPE_THETA ** (-jnp.arange(h128, dtype=F32) / h128)
    c, s = jnp.cos(posf * inv), jnp.sin(posf * inv)
    t128 = (jnp.concatenate([c, c], 1), jnp.concatenate([-s, s], 1))
    h64 = IDX_DIM // 2
    inv = ROPE_THETA ** (-jnp.arange(h64, dtype=F32) / h64)
    c, s = jnp.cos(posf * inv), jnp.sin(posf * inv)
    z = jnp.zeros_like(s)
    c64 = jnp.concatenate([c, c], 1)
    sa64 = jnp.concatenate([-s, z], 1)
    sb64 = jnp.concatenate([z, s], 1)
    t64 = tuple(jnp.concatenate([a, a], 1) for a in (c64, sa64, sb64))
    idx_scale = (IDX_HEADS * IDX_DIM) ** -0.5
    n = pos.shape[0]
    ck = jnp.concatenate([c64, jnp.full((n, IDX_HEADS), idx_scale, F32),
                          jnp.zeros((n, LANES - IDX_DIM - IDX_HEADS), F32)], 1)
    z64 = jnp.zeros((n, LANES - IDX_DIM), F32)
    tkw = (ck, jnp.concatenate([sa64, z64], 1), jnp.concatenate([sb64, z64], 1))
    return t128, t64, tkw


def _sigmoid(x):
    return 0.5 * jnp.tanh(0.5 * x) + 0.5


def _softplus_neg(lam):
    return jnp.maximum(-lam, 0.0) + jnp.log1p(jnp.exp(-jnp.abs(lam)))


def _gelu_tanh(x):
    return 0.5 * x * (1.0 + jnp.tanh(0.7978845608028654 * (x + 0.044715 * (x * x * x))))


def _rglru_gates(xc, wa_ref, wx_ref, ba, bx, lam):
    nblk = xc.shape[1] // RNN_BW
    rs, gs = [], []
    for j in range(nblk):
        xj = xc[:, j * RNN_BW:(j + 1) * RNN_BW].astype(BF16)
        rs.append(jnp.dot(xj, wa_ref[j], preferred_element_type=F32))
        gs.append(jnp.dot(xj, wx_ref[j], preferred_element_type=F32))
    r = _sigmoid(jnp.concatenate(rs, 1) + ba)
    g = _sigmoid(jnp.concatenate(gs, 1) + bx)
    log_a = -RG_C * r * _softplus_neg(lam)
    a = jnp.exp(log_a)
    mult = jnp.sqrt(jnp.tanh(-log_a) * (1.0 + a * a))
    return a, mult, g * xc


def _rnn_prompt_kernel(xr_ref, gr_ref, cw_ref, cb_ref, wa_ref, wx_ref, ba_ref, bx_ref, lam_ref,
                       y_ref, hl_ref, xbuf, hcar, a_scr, b_scr, h_scr):
    t = pl.program_id(1)
    tm, cw = xr_ref.shape

    @pl.when(t == 0)
    def _():
        xbuf[0:8, :] = jnp.zeros((8, cw), F32)
        hcar[...] = jnp.zeros_like(hcar)

    x = xr_ref[...]
    xbuf[8:8 + tm, :] = x
    w = cw_ref[...]
    xc = (cb_ref[...] + w[3:4] * x + w[2:3] * xbuf[7:7 + tm, :]
          + w[1:2] * xbuf[6:6 + tm, :] + w[0:1] * xbuf[5:5 + tm, :])
    xbuf[0:8, :] = x[tm - 8:tm, :]
    a, mult, gx = _rglru_gates(xc, wa_ref, wx_ref, ba_ref[...], bx_ref[...], lam_ref[...])
    pos = t * tm + lax.broadcasted_iota(I32, (tm, cw), 0)
    mult = jnp.where(pos == 0, 1.0, mult)
    a_scr[...] = a
    b_scr[...] = mult * gx
    row8 = lax.broadcasted_iota(I32, (8, cw), 0)

    def group(g, carry):
        r0 = pl.multiple_of(g * 8, 8)
        av = a_scr[pl.ds(r0, 8), :]
        bv = b_scr[pl.ds(r0, 8), :]
        for d in (1, 2, 4):
            a_s = pltpu.roll(av, d, 0)
            b_s = pltpu.roll(bv, d, 0)
            m = row8 >= d
            bv = jnp.where(m, av * b_s + bv, bv)
            av = jnp.where(m, av * a_s, av)
        h = av * carry + bv
        h_scr[pl.ds(r0, 8), :] = h
        return h[7:8, :]

    carry = lax.fori_loop(0, tm // 8, group, hcar[0:1, :])
    hcar[0:1, :] = carry
    y_ref[...] = (_gelu_tanh(gr_ref[...]) * h_scr[...]).astype(y_ref.dtype)

    @pl.when(t == pl.num_programs(1) - 1)
    def _():
        hl_ref[...] = carry


def _rnn_prompt(zr, s, conv_w, conv_b, wa, wx, ba, bx, lam):
    d = conv_w.shape[1]
    cw = 512
    tm = _pick(s, (256, 128))
    ncb = d // cw
    nb = cw // RNN_BW
    vec = lambda: pl.BlockSpec((1, cw), lambda c, t: (0, c))
    y, hl = pl.pallas_call(
        _rnn_prompt_kernel,
        out_shape=[jax.ShapeDtypeStruct((s, d), BF16), jax.ShapeDtypeStruct((1, d), F32)],
        grid=(ncb, s // tm),
        in_specs=[pl.BlockSpec((tm, cw), lambda c, t: (t, c)),
                  pl.BlockSpec((tm, cw), lambda c, t: (t, ncb + c)),
                  pl.BlockSpec((CONV_W, cw), lambda c, t: (0, c)),
                  vec(),
                  pl.BlockSpec((nb, RNN_BW, RNN_BW), lambda c, t: (c, 0, 0)),
                  pl.BlockSpec((nb, RNN_BW, RNN_BW), lambda c, t: (c, 0, 0)),
                  vec(), vec(), vec()],
        out_specs=[pl.BlockSpec((tm, cw), lambda c, t: (t, c)),
                   pl.BlockSpec((1, cw), lambda c, t: (0, c))],
        scratch_shapes=[pltpu.VMEM((tm + 8, cw), F32), pltpu.VMEM((8, cw), F32),
                        pltpu.VMEM((tm, cw), F32), pltpu.VMEM((tm, cw), F32),
                        pltpu.VMEM((tm, cw), F32)],
        compiler_params=_cparams(("parallel", "arbitrary")),
        name="rnn_prompt",
    )(zr, zr, conv_w, conv_b, wa, wx, ba, bx, lam)
    return y, hl


def _rnn_sample_kernel(xr_ref, gr_ref, c0_ref, c1_ref, c2_ref, h0_ref, cw_ref, cb_ref,
                       wa_ref, wx_ref, ba_ref, bx_ref, lam_ref, y_ref, h_ref):
    w = cw_ref[...]
    xc = (cb_ref[...] + w[3:4] * xr_ref[...] + w[2:3] * c2_ref[...]
          + w[1:2] * c1_ref[...] + w[0:1] * c0_ref[...])
    a, mult, gx = _rglru_gates(xc, wa_ref, wx_ref, ba_ref[...], bx_ref[...], lam_ref[...])
    h = a * h0_ref[...] + mult * gx
    h_ref[...] = h
    y_ref[...] = (_gelu_tanh(gr_ref[...]) * h).astype(y_ref.dtype)


def _rnn_sample(zr, conv_state, h0, conv_w, conv_b, wa, wx, ba, bx, lam):
    n, d = h0.shape
    cw = 512
    ncb = d // cw
    nb = cw // RNN_BW
    blk = lambda off: pl.BlockSpec((n, cw), lambda c: (0, off + c))
    vec = lambda: pl.BlockSpec((1, cw), lambda c: (0, c))
    wsp = lambda: pl.BlockSpec((nb, RNN_BW, RNN_BW), lambda c: (c, 0, 0))
    return pl.pallas_call(
        _rnn_sample_kernel,
        out_shape=[jax.ShapeDtypeStruct((n, d), BF16), jax.ShapeDtypeStruct((n, d), F32)],
        grid=(ncb,),
        in_specs=[blk(0), blk(ncb), blk(0), blk(0), blk(0), blk(0),
                  pl.BlockSpec((CONV_W, cw), lambda c: (0, c)), vec(), wsp(), wsp(),
                  vec(), vec(), vec()],
        out_specs=[blk(0), blk(0)],
        compiler_params=_cparams(("parallel",)),
        name="rnn_sample",
    )(zr, zr, conv_state[:, 0], conv_state[:, 1], conv_state[:, 2], h0,
      conv_w, conv_b, wa, wx, ba, bx, lam)


def _layer_norm(x, g, b):
    mu = jnp.mean(x, axis=-1, keepdims=True)
    xc = x - mu
    var = jnp.mean(xc * xc, axis=-1, keepdims=True)
    return xc * lax.rsqrt(var + LN_EPS) * g + b


def _merge_kernel(yr_ref, ya_ref, wr_ref, wa_ref, ga_ref, gb_ref, o_ref):
    a = jnp.dot(yr_ref[...], wr_ref[...], preferred_element_type=F32)
    b = jnp.dot(ya_ref[...], wa_ref[...], preferred_element_type=F32)
    m = _sigmoid(ga_ref[...]) * a + _sigmoid(gb_ref[...]) * b
    o_ref[...] = m.astype(o_ref.dtype)


def _merge(y_rnn, y_att, w_o_rnn, w_o_att, zr):
    t, d = y_rnn.shape
    tm = _pick(t, (512, 256, 128))
    tn = 512
    nj = d // tn
    return pl.pallas_call(
        _merge_kernel,
        out_shape=jax.ShapeDtypeStruct((t, d), BF16),
        grid=(t // tm, nj),
        in_specs=[pl.BlockSpec((tm, d), lambda i, j: (i, 0)),
                  pl.BlockSpec((tm, d), lambda i, j: (i, 0)),
                  pl.BlockSpec((d, tn), lambda i, j: (0, j)),
                  pl.BlockSpec((d, tn), lambda i, j: (0, j)),
                  pl.BlockSpec((tm, tn), lambda i, j: (i, 2 * nj + j)),
                  pl.BlockSpec((tm, tn), lambda i, j: (i, 3 * nj + j))],
        out_specs=pl.BlockSpec((tm, tn), lambda i, j: (i, j)),
        compiler_params=_cparams(("parallel", "parallel")),
        name="merge",
    )(y_rnn, y_att, w_o_rnn, w_o_att, zr, zr)


LN1_SUB_ROWS = 256


def _ln1_router_kernel(*refs, alpha, n_skip):
    m_ref, x_ref, w_ref, g_ref, b_ref, wr_ref, br_ref, x1_ref, sel_ref, gate_ref = refs[n_skip:]
    tm = x_ref.shape[0]
    sub = min(tm, LN1_SUB_ROWS)
    for r in range(0, tm, sub):
        rows = slice(r, r + sub)
        y = alpha * x_ref[rows, :] + jnp.dot(m_ref[rows, :], w_ref[...], preferred_element_type=F32)
        x1 = _layer_norm(y, g_ref[...], b_ref[...])
        x1_ref[rows, :] = x1
        x_hi = x1.astype(BF16)
        x_lo = (x1 - x_hi.astype(F32)).astype(BF16)
        part = (jnp.dot(x_hi, wr_ref[...], preferred_element_type=F32)
                + jnp.dot(x_lo, wr_ref[...], preferred_element_type=F32))
        logits = part[:, :LANES] + part[:, LANES:] + br_ref[...]
        lane = lax.broadcasted_iota(I32, logits.shape, 1)
        live = lane < N_EXPERTS
        cur = jnp.where(live, logits, -jnp.inf)
        top = jnp.max(cur, axis=1, keepdims=True)
        sel = jnp.zeros(logits.shape, jnp.bool_)
        for _ in range(TOP_K):
            mx = jnp.max(cur, axis=1, keepdims=True)
            first = jnp.min(jnp.where(cur == mx, lane, LANES), axis=1, keepdims=True)
            pick = lane == first
            sel = jnp.logical_or(sel, pick)
            cur = jnp.where(pick, -jnp.inf, cur)
        e = jnp.where(sel, jnp.exp(logits - top), 0.0)
        sel_ref[rows, :] = jnp.where(sel, 1.0, 0.0)
        gate_ref[rows, :] = e / jnp.sum(e, axis=1, keepdims=True)


def _ln1_router(m, x, w_out, g, b, w_router, b_router, alpha, row0, t_all, prev=None):
    t, d = x.shape
    tm = _pick(t, (2 * LN1_SUB_ROWS, LN1_SUB_ROWS, 128))
    assert row0 % tm == 0
    r0 = row0 // tm
    in_specs = [pl.BlockSpec((tm, d), lambda i: (i, 0)),
                pl.BlockSpec((tm, d), lambda i: (i, 0)),
                _const_spec((d, d)), _const_spec((1, d)), _const_spec((1, d)),
                _const_spec((d, 2 * LANES)), _const_spec((1, LANES))]
    args = [m, x, w_out, g, b, w_router, b_router]
    aliases = {}
    if prev is not None:
        in_specs = [pl.BlockSpec(memory_space=pl.ANY)] * 3 + in_specs
        args = list(prev) + args
        aliases = {0: 0, 1: 1, 2: 2}

    return pl.pallas_call(
        functools.partial(_ln1_router_kernel, alpha=alpha, n_skip=len(aliases)),
        out_shape=[jax.ShapeDtypeStruct((t_all, d), F32),
                   jax.ShapeDtypeStruct((t_all, LANES), F32),
                   jax.ShapeDtypeStruct((t_all, LANES), F32)],
        grid=(t // tm,),
        in_specs=in_specs,
        out_specs=[pl.BlockSpec((tm, d), lambda i: (r0 + i, 0)),
                   pl.BlockSpec((tm, LANES), lambda i: (r0 + i, 0)),
                   pl.BlockSpec((tm, LANES), lambda i: (r0 + i, 0))],
        input_output_aliases=aliases,
        compiler_params=_cparams(("parallel",)),
        name="ln1_router",
    )(*args)


def _sort_key(x):
    bits = lax.bitcast_convert_type(x, I32)
    return bits ^ (jnp.right_shift(bits, 31) & 0x7FFFFFFF)


def _count(keys_ref, nch, cw, tvec, strict):
    rows = keys_ref.shape[0]
    tb = jnp.broadcast_to(tvec, (rows, LANES))

    def body(c, cnt):
        off = c * cw
        for s in range(cw // LANES):
            k = keys_ref[:, pl.ds(pl.multiple_of(off + s * LANES, LANES), LANES)]
            hit = (k > tb) if strict else (k >= tb)
            cnt = cnt + jnp.where(hit, 1.0, 0.0)
        return cnt

    cnt = lax.fori_loop(0, nch, body, jnp.zeros((rows, LANES), F32))
    return jnp.sum(cnt, axis=1, keepdims=True)


def _select_threshold(keys_ref, nch, cw, n_sel, few):
    rows = keys_ref.shape[0]
    settled = few > 0.5

    def cond(st):
        b, _, cnt_cur = st
        open_rows = jnp.where(jnp.logical_or(settled, cnt_cur == n_sel), 0.0, 1.0)
        return jnp.logical_and(b < 32, jnp.max(open_rows) > 0.0)

    def bit_step(st):
        b, cur, cnt_cur = st
        cand = cur | lax.shift_left(jnp.int32(1), 31 - b)
        cnt = _count(keys_ref, nch, cw, cand ^ INT_MIN, False)
        take = cnt >= n_sel
        return b + 1, jnp.where(take, cand, cur), jnp.where(take, cnt, cnt_cur)

    everything = jnp.zeros((rows, 1), F32) + jnp.asarray(nch * cw, F32)
    _, cur, n_ge = lax.while_loop(cond, bit_step, (jnp.int32(0), jnp.zeros((rows, 1), I32), everything))
    t = cur ^ INT_MIN
    tied = jnp.where(jnp.logical_or(settled, n_ge <= n_sel), 0.0, 1.0)

    @pl.when(jnp.max(tied) > 0.0)
    def _():
        n_gt = _count(keys_ref, nch, cw, t, True)
        tb = jnp.broadcast_to(t, (rows, LANES))
        needb = jnp.broadcast_to(jnp.where(tied > 0.5, n_sel - n_gt, 1e9), (rows, LANES))
        r = lax.broadcasted_iota(I32, (LANES, LANES), 0)
        c = lax.broadcasted_iota(I32, (LANES, LANES), 1)
        upper = jnp.where(r < c, 1.0, 0.0).astype(BF16)

        def fix(j, run):
            sl = pl.ds(pl.multiple_of(j * LANES, LANES), LANES)
            k = keys_ref[:, sl]
            eq = k == tb
            eqf = jnp.where(eq, 1.0, 0.0)
            before = jnp.dot(eqf.astype(BF16), upper, preferred_element_type=F32) + run
            drop = jnp.logical_and(eq, before >= needb)
            keys_ref[:, sl] = jnp.where(drop, INT_MIN, k)
            return run + jnp.sum(eqf, axis=1, keepdims=True)

        lax.fori_loop(0, nch * (cw // LANES), fix, jnp.zeros((rows, 1), F32))

    return jnp.maximum(t, INT_MIN + 1)


def _attn_prompt_kernel(qi_ref, kw_ref, q_ref, kit_ref, kt_ref, vx_ref, o_ref,
                        keys_scr, lhs_scr, wb_scr, tb_scr, qg_scr, s_scr, p_scr, m_scr, acc_scr,
                        *, n_sel, kc):
    i = pl.program_id(0)
    qb = Q_BLOCK
    nch = (i * qb + qb + kc - 1) // kc
    kw = kw_ref[...]
    lane = lax.broadcasted_iota(I32, (qb, LANES), 1)
    for h in range(IDX_HEADS):
        wb_scr[h] = jnp.broadcast_to(kw[:, IDX_DIM + h:IDX_DIM + h + 1], (qb, LANES))
        blk = qi_ref[:, (h // 2) * LANES:(h // 2 + 1) * LANES]
        keep = (lane < IDX_DIM) if h % 2 == 0 else (lane >= IDX_DIM)
        lhs_scr[h] = jnp.where(keep, blk, jnp.zeros_like(blk))

    sw = min(256, kc)

    def score_chunk(c, carry):
        for s in range(kc // sw):
            o2 = pl.multiple_of(c * kc + s * sw, sw)
            kt = kit_ref[:, pl.ds(o2, sw)]
            acc = jnp.zeros((qb, sw), F32)
            for h in range(IDX_HEADS):
                sc = jnp.dot(lhs_scr[h], kt, preferred_element_type=F32)
                wb = wb_scr[h]
                acc = acc + jnp.maximum(sc, 0.0) * jnp.concatenate([wb] * (sw // LANES), axis=1)
            kpos = o2 + lax.broadcasted_iota(I32, (qb, sw), 1)
            qpos = i * qb + lax.broadcasted_iota(I32, (qb, sw), 0)
            keys_scr[:, pl.ds(o2, sw)] = jnp.where(kpos <= qpos, _sort_key(acc), INT_MIN)
        return carry

    lax.fori_loop(0, nch, score_chunk, 0)
    n_causal = i * qb + lax.broadcasted_iota(I32, (qb, 1), 0) + 1
    t = _select_threshold(keys_scr, nch, kc, n_sel, jnp.where(n_causal <= n_sel, 1.0, 0.0))
    rows = Q_PER_KV * qb
    strip = 32
    nrep = kc // LANES
    tb_scr[...] = jnp.broadcast_to(t, (qb, LANES))

    for g in range(N_KV):
        for j in range(Q_PER_KV):
            h = g * Q_PER_KV + j
            qg_scr[g, j * qb:(j + 1) * qb, :] = q_ref[:, h * HEAD_DIM:(h + 1) * HEAD_DIM]
    m_scr[...] = jnp.full(m_scr.shape, NEG_BIG, F32)
    acc_scr[...] = jnp.zeros_like(acc_scr)

    def body(c, carry):
        off = pl.multiple_of(c * kc, kc)
        for g in range(N_KV):
            s_scr[g] = jnp.dot(qg_scr[g], kt_ref[g * HEAD_DIM:(g + 1) * HEAD_DIM, pl.ds(off, kc)],
                               preferred_element_type=F32)
            for rq in range(0, qb, strip):
                tb = jnp.concatenate([tb_scr[rq:rq + strip, :]] * nrep, axis=1)
                for j in range(Q_PER_KV):
                    r = j * qb + rq
                    km = keys_scr[rq:rq + strip, pl.ds(off, kc)] >= tb
                    s = jnp.where(km, s_scr[g, r:r + strip, :], NEG_BIG)
                    s_scr[g, r:r + strip, :] = s
                    m_old = m_scr[g, r:r + strip, :]
                    m_new = jnp.maximum(m_old, jnp.max(s, axis=1, keepdims=True))
                    alpha = jnp.exp2(m_old - m_new)
                    m_scr[g, r:r + strip, :] = m_new
                    acc_scr[g, r:r + strip, :] = (acc_scr[g, r:r + strip, :]
                                                  * jnp.concatenate([alpha] * 2, axis=1))
            for r in range(0, rows, strip):
                m_new = m_scr[g, r:r + strip, :]
                p = jnp.exp2(s_scr[g, r:r + strip, :] - jnp.concatenate([m_new] * nrep, axis=1))
                p_scr[g, r:r + strip, :] = p.astype(BF16)
            acc_scr[g] += jnp.dot(p_scr[g], vx_ref[pl.ds(off, kc), g * 2 * HEAD_DIM:(g + 1) * 2 * HEAD_DIM],
                                  preferred_element_type=F32)
        return carry

    def body_pair(c2, carry):
        body(2 * c2, carry)
        return body(2 * c2 + 1, carry)

    lax.fori_loop(0, nch // 2, body_pair, 0)

    @pl.when(nch % 2 == 1)
    def _():
        body(nch - 1, 0)

    for g in range(N_KV):
        acc = acc_scr[g]
        out = acc[:, :HEAD_DIM] / acc[:, HEAD_DIM:HEAD_DIM + 1]
        for j in range(Q_PER_KV):
            h = g * Q_PER_KV + j
            o_ref[:, h * HEAD_DIM:(h + 1) * HEAD_DIM] = out[j * qb:(j + 1) * qb].astype(o_ref.dtype)


def _attn_prompt(qi, kw, q, kit2, kt, vx, s, n_sel):
    kc = min(512, s)
    d = N_HEADS * HEAD_DIM
    rows = Q_PER_KV * Q_BLOCK
    return pl.pallas_call(
        functools.partial(_attn_prompt_kernel, n_sel=n_sel, kc=kc),
        out_shape=jax.ShapeDtypeStruct((s, d), BF16),
        grid=(s // Q_BLOCK,),
        in_specs=[pl.BlockSpec((Q_BLOCK, IDX_HEADS * IDX_DIM), lambda i: (i, 0)),
                  pl.BlockSpec((Q_BLOCK, LANES), lambda i: (i, 0)),
                  pl.BlockSpec((Q_BLOCK, d), lambda i: (i, 0)),
                  _const_spec((LANES, s)), _const_spec((N_KV * HEAD_DIM, s)),
                  _const_spec((s, N_KV * 2 * HEAD_DIM))],
        out_specs=pl.BlockSpec((Q_BLOCK, d), lambda i: (i, 0)),
        scratch_shapes=[pltpu.VMEM((Q_BLOCK, s), I32),
                        pltpu.VMEM((IDX_HEADS, Q_BLOCK, LANES), BF16),
                        pltpu.VMEM((IDX_HEADS, Q_BLOCK, LANES), F32),
                        pltpu.VMEM((Q_BLOCK, LANES), I32),
                        pltpu.VMEM((N_KV, rows, HEAD_DIM), BF16),
                        pltpu.VMEM((N_KV, rows, kc), F32),
                        pltpu.VMEM((N_KV, rows, kc), BF16),
                        pltpu.VMEM((N_KV, rows, LANES), F32),
                        pltpu.VMEM((N_KV, rows, 2 * HEAD_DIM), F32)],
        compiler_params=_cparams(("parallel",)),
        name="attn_prompt",
    )(qi, kw, q, kit2, kt, vx)


IDX_SEQS_PER_STEP = 4


def _idx_sample_kernel(pt_ref, qi_ref, w_ref, kn_ref, *refs, npg, past, group):
    pages, o_ref = refs[:group * npg], refs[group * npg]
    lane = lax.broadcasted_iota(I32, (1, LANES), 1)
    for g in range(group):
        qi = qi_ref[g]
        kp = jnp.concatenate([p[...] for p in pages[g * npg:(g + 1) * npg]], axis=1).astype(BF16)
        sc = jnp.dot(qi, kp, preferred_element_type=F32)
        w = w_ref[g]
        o_ref[g, :, 0:past] = jnp.sum(jnp.maximum(sc, 0.0) * w, axis=0, keepdims=True)
        kn = kn_ref[g][:, :IDX_DIM].astype(BF16).astype(F32)
        sn = jnp.sum(qi.astype(F32) * kn, axis=1, keepdims=True)
        new = jnp.sum(jnp.maximum(sn, 0.0) * w, axis=0, keepdims=True)
        o_ref[g, :, past:past + LANES] = jnp.where(lane == 0, new, -jnp.inf)


def _idx_sample(page_table, qi3, w3, kn3, cache_kidx_t):
    n, npg = page_table.shape
    page = cache_kidx_t.shape[2]
    past = npg * page
    group = IDX_SEQS_PER_STEP if n % IDX_SEQS_PER_STEP == 0 else 1
    page_specs = [pl.BlockSpec((None, IDX_DIM, page),
                               lambda b, pt, g=g, p=p: (pt[(b * group + g) * npg + p], 0, 0))
                  for g in range(group) for p in range(npg)]
    return pl.pallas_call(
        functools.partial(_idx_sample_kernel, npg=npg, past=past, group=group),
        out_shape=jax.ShapeDtypeStruct((n, 1, past + LANES), F32),
        grid_spec=pltpu.PrefetchScalarGridSpec(
            num_scalar_prefetch=1, grid=(n // group,),
            in_specs=[pl.BlockSpec((group, IDX_HEADS, IDX_DIM), lambda b, pt: (b, 0, 0)),
                      pl.BlockSpec((group, IDX_HEADS, 1), lambda b, pt: (b, 0, 0)),
                      pl.BlockSpec((group, 1, LANES), lambda b, pt: (b, 0, 0))] + page_specs,
            out_specs=pl.BlockSpec((group, 1, past + LANES), lambda b, pt: (b, 0, 0))),
        compiler_params=_cparams(("parallel",)),
        name="idx_sample",
    )(page_table.reshape(-1), qi3, w3, kn3, *([cache_kidx_t] * (group * npg)))


def _mask_sample_kernel(sc_ref, m_ref, keys_scr, *, n_valid, n_sel):
    rows, width = sc_ref.shape
    pos = lax.broadcasted_iota(I32, (rows, width), 1)
    keys_scr[...] = jnp.where(pos < n_valid, _sort_key(sc_ref[...]), INT_MIN)
    few = jnp.full((rows, 1), 1.0 if n_valid <= n_sel else 0.0, F32)
    t = _select_threshold(keys_scr, width // LANES, LANES, n_sel, few)
    m_ref[...] = jnp.where(keys_scr[...] >= t, 1.0, 0.0)


def _mask_sample(sc, n_valid, n_sel):
    n, width = sc.shape
    return pl.pallas_call(
        functools.partial(_mask_sample_kernel, n_valid=n_valid, n_sel=n_sel),
        out_shape=jax.ShapeDtypeStruct((n, width), F32),
        scratch_shapes=[pltpu.VMEM((n, width), I32)],
        name="mask_sample",
    )(sc)


def _attn_sample_kernel(pt_ref, q_ref, m4_ref, mn_ref, kn_ref, vn_ref, *refs, npg):
    kpages, vpages, o_ref = refs[:npg], refs[npg:2 * npg], refs[2 * npg]
    q = q_ref[...]
    kp = jnp.concatenate([p[...] for p in kpages], axis=0).astype(BF16)
    s = lax.dot_general(q, kp, (((1,), (1,)), ((), ())), preferred_element_type=F32)
    grp = lax.broadcasted_iota(I32, s.shape, 0) // Q_PER_KV
    own = (lax.broadcasted_iota(I32, s.shape, 1) & (N_KV - 1)) == grp
    s = jnp.where(own, jnp.where(m4_ref[...] > 0.5, s, NEG_BIG), NEG_BIG)
    grp_h = lax.broadcasted_iota(I32, (N_HEADS, HEAD_DIM), 0) // Q_PER_KV
    kn = kn_ref[...].astype(F32)
    vn = vn_ref[...].astype(F32)
    kn_h = jnp.zeros((N_HEADS, HEAD_DIM), F32)
    vn_h = jnp.zeros((N_HEADS, HEAD_DIM), F32)
    for g in range(N_KV):
        kn_h = jnp.where(grp_h == g, kn[g:g + 1, :], kn_h)
        vn_h = jnp.where(grp_h == g, vn[g:g + 1, :], vn_h)
    sn = jnp.sum(q.astype(F32) * kn_h, axis=1, keepdims=True)
    sn = jnp.where(mn_ref[:, 0:1] > 0.5, sn, NEG_BIG)
    m = jnp.maximum(jnp.max(s, axis=1, keepdims=True), sn)
    p = jnp.exp2(s - m)
    pn = jnp.exp2(sn - m)
    l = jnp.sum(p, axis=1, keepdims=True) + pn
    vp = jnp.concatenate([r[...] for r in vpages], axis=0).astype(BF16)
    o = jnp.dot(p.astype(BF16), vp, preferred_element_type=F32)
    o_ref[...] = ((o + pn.astype(BF16).astype(F32) * vn_h) / l).astype(o_ref.dtype)


def _attn_sample(page_table, q3, mask4, mask_new, kn3, vn3, cache_k, cache_v):
    n, npg = page_table.shape
    prow = cache_k.shape[1]
    pspec = lambda p: pl.BlockSpec((None, prow, HEAD_DIM), lambda b, pt, p=p: (pt[b * npg + p], 0, 0))
    return pl.pallas_call(
        functools.partial(_attn_sample_kernel, npg=npg),
        out_shape=jax.ShapeDtypeStruct((n, N_HEADS, HEAD_DIM), BF16),
        grid_spec=pltpu.PrefetchScalarGridSpec(
            num_scalar_prefetch=1, grid=(n,),
            in_specs=[pl.BlockSpec((None, N_HEADS, HEAD_DIM), lambda b, pt: (b, 0, 0)),
                      pl.BlockSpec((None, 1, npg * prow), lambda b, pt: (b, 0, 0)),
                      pl.BlockSpec((None, 1, LANES), lambda b, pt: (b, 0, 0)),
                      pl.BlockSpec((None, N_KV, HEAD_DIM), lambda b, pt: (b, 0, 0)),
                      pl.BlockSpec((None, N_KV, HEAD_DIM), lambda b, pt: (b, 0, 0))]
            + [pspec(p) for p in range(npg)] + [pspec(p) for p in range(npg)],
            out_specs=pl.BlockSpec((None, N_HEADS, HEAD_DIM), lambda b, pt: (b, 0, 0))),
        compiler_params=_cparams(("parallel",)),
        name="attn_sample",
    )(page_table.reshape(-1), q3, mask4, mask_new, kn3, vn3, *([cache_k] * npg), *([cache_v] * npg))


MOE_RB = 128
MOE_GROUP = 4
MOE_RC = 1280
MOE_TF = 512
MOE_TN = 512


def _moe_sizes(n_tok):
    n_assign = n_tok * TOP_K
    n_slots = (n_assign // MOE_RB + N_EXPERTS) * MOE_RB
    n_units = N_EXPERTS + n_assign // MOE_RC
    return n_slots, n_units


def _moe_rank_kernel(sel_ref, rank_ref, cnt_ref, carry):
    i = pl.program_id(0)
    tp = sel_ref.shape[0]

    @pl.when(i == 0)
    def _():
        carry[...] = jnp.zeros_like(carry)

    a = sel_ref[...]
    r = lax.broadcasted_iota(I32, (tp, tp), 0)
    c = lax.broadcasted_iota(I32, (tp, tp), 1)
    lower = jnp.where(c < r, 1.0, 0.0).astype(BF16)
    rank_ref[...] = jnp.dot(lower, a.astype(BF16), preferred_element_type=F32) + carry[0:1, :]
    carry[...] = carry[...] + jnp.sum(a, axis=0, keepdims=True)
    cnt_ref[...] = carry[...]


def _moe_rank(sel):
    t = sel.shape[0]
    tp = _pick(t, (256, 128))
    return pl.pallas_call(
        _moe_rank_kernel,
        out_shape=[jax.ShapeDtypeStruct((t, LANES), F32), jax.ShapeDtypeStruct((8, LANES), F32)],
        grid=(t // tp,),
        in_specs=[pl.BlockSpec((tp, LANES), lambda i: (i, 0))],
        out_specs=[pl.BlockSpec((tp, LANES), lambda i: (i, 0)),
                   pl.BlockSpec((8, LANES), lambda i: (0, 0))],
        scratch_shapes=[pltpu.VMEM((8, LANES), F32)],
        compiler_params=_cparams(("arbitrary",)),
        name="moe_rank",
    )(sel)


def _moe_dest_kernel(sel_ref, gate_ref, rank_ref, cnt_ref, dest_ref, g4_ref, tab_ref):
    cnt = cnt_ref[...]
    lane8 = lax.broadcasted_iota(I32, cnt.shape, 1)
    padded = jnp.ceil(cnt * (1.0 / MOE_RB)) * MOE_RB
    incl = padded
    for d in (1, 2, 4, 8, 16, 32, 64):
        incl = incl + jnp.where(lane8 >= d, pltpu.roll(incl, d, 1), 0.0)
    start = incl - padded
    row8 = lax.broadcasted_iota(I32, cnt.shape, 0)
    tab_ref[...] = jnp.where(row8 == 0, start, jnp.where(row8 == 1, padded, 0.0)).astype(I32)
    dest = start[0:1, :] + rank_ref[...]
    gate = gate_ref[...]
    cur = sel_ref[...]
    lane = lax.broadcasted_iota(I32, cur.shape, 1)
    d4 = jnp.zeros(cur.shape, F32)
    g4 = jnp.zeros(cur.shape, F32)
    for k in range(TOP_K):
        first = jnp.min(jnp.where(cur > 0.5, lane, LANES), axis=1, keepdims=True)
        pick = lane == first
        dk = jnp.sum(jnp.where(pick, dest, 0.0), axis=1, keepdims=True)
        gk = jnp.sum(jnp.where(pick, gate, 0.0), axis=1, keepdims=True)
        d4 = jnp.where(lane == k, dk, d4)
        g4 = jnp.where(lane == k, gk, g4)
        cur = jnp.where(pick, 0.0, cur)
    dest_ref[...] = d4.astype(I32)
    g4_ref[...] = g4


def _moe_dest(sel, gate, rank, cnt):
    t = sel.shape[0]
    tp = _pick(t, (256, 128))
    row = lambda: pl.BlockSpec((tp, LANES), lambda i: (i, 0))
    one = lambda: pl.BlockSpec((8, LANES), lambda i: (0, 0))
    return pl.pallas_call(
        _moe_dest_kernel,
        out_shape=[jax.ShapeDtypeStruct((t, LANES), I32), jax.ShapeDtypeStruct((t, LANES), F32),
                   jax.ShapeDtypeStruct((8, LANES), I32)],
        grid=(t // tp,),
        in_specs=[row(), row(), row(), one()],
        out_specs=[row(), row(), one()],
        compiler_params=_cparams(("arbitrary",)),
        name="moe_dest",
    )(sel, gate, rank, cnt)


def _moe_units_kernel(tab_ref, unit_ref, *, n_units):
    rcb = MOE_RC // MOE_RB

    def per_expert(e, state):
        u0, _ = state
        nb = tab_ref[1, e] // MOE_RB
        b0 = tab_ref[0, e] // MOE_RB
        nu = (nb + rcb - 1) // rcb

        def per_unit(j, carry):
            unit_ref[0, u0 + j] = e
            unit_ref[1, u0 + j] = b0 + j * rcb
            unit_ref[2, u0 + j] = jnp.minimum(nb - j * rcb, rcb)
            return carry

        lax.fori_loop(0, nu, per_unit, 0)
        return u0 + nu, jnp.where(nu > 0, e, state[1])

    used, last = lax.fori_loop(0, N_EXPERTS, per_expert, (jnp.int32(0), jnp.int32(0)))

    def fill(u, carry):
        unit_ref[0, u] = last
        unit_ref[1, u] = 0
        unit_ref[2, u] = 0
        return carry

    lax.fori_loop(used, n_units, fill, 0)


def _moe_units(tab, n_units):
    return pl.pallas_call(
        functools.partial(_moe_units_kernel, n_units=n_units),
        out_shape=jax.ShapeDtypeStruct((3, n_units), I32),
        in_specs=[pl.BlockSpec(memory_space=pltpu.SMEM)],
        out_specs=pl.BlockSpec(memory_space=pltpu.SMEM),
        name="moe_units",
    )(tab)


def _moe_scatter_kernel(dest_ref, x_ref, xs_ref, sem):
    tm = x_ref.shape[0]

    def issue(r, carry):
        for k in range(TOP_K):
            pltpu.make_async_copy(x_ref.at[pl.ds(r, 1)], xs_ref.at[pl.ds(dest_ref[r * TOP_K + k], 1)],
                                  sem).start()
        return carry

    lax.fori_loop(0, tm, issue, 0)
    for _ in range(TOP_K):
        pltpu.make_async_copy(x_ref, xs_ref.at[pl.ds(0, tm)], sem).wait()


def _moe_scatter(dest_flat, x1, n_slots):
    t, d = x1.shape
    tm = _pick(t, (256, 128))
    return pl.pallas_call(
        _moe_scatter_kernel,
        out_shape=jax.ShapeDtypeStruct((n_slots, d), F32),
        grid=(t // tm,),
        in_specs=[pl.BlockSpec((tm * TOP_K,), lambda i: (i,), memory_space=pltpu.SMEM),
                  pl.BlockSpec((tm, d), lambda i: (i, 0))],
        out_specs=pl.BlockSpec(memory_space=pl.ANY),
        scratch_shapes=[pltpu.SemaphoreType.DMA(())],
        compiler_params=_cparams(("arbitrary",)),
        name="moe_scatter",
    )(dest_flat, x1)


def _moe_expert_kernel(unit_ref, xs_ref, wg_ref, wl_ref, bg_ref, bl_ref, wd_ref, bd_ref, ys_ref,
                       x_scr, act_scr, xst, yst1, yst2, ystg, wg_bf, wl_bf, wd_bf,
                       xsem, ysem1, ysem2, ysemg, *, ju):
    u = pl.program_id(0)
    j = pl.program_id(1)
    b0 = unit_ref[1, u]
    ns = unit_ref[2, u]
    rb = MOE_RB

    def x_copy(s, slot):
        return pltpu.make_async_copy(xs_ref.at[pl.ds((b0 + s) * rb, rb)], xst.at[slot], xsem.at[slot])

    def rows_of(s, n=1):
        return pl.ds(s * rb, n * rb) if isinstance(s, int) else pl.ds(pl.multiple_of(s * rb, rb), n * rb)

    def up_block(s, n=1):
        xb = x_scr[rows_of(s, n), :]
        hg = jnp.dot(xb, wg_bf[...], preferred_element_type=F32) + bg_ref[...]
        hl = jnp.dot(xb, wl_bf[...], preferred_element_type=F32) + bl_ref[...]
        glu = jnp.minimum(hg, SWIGLU_LIMIT)
        lin = jnp.clip(hl, -SWIGLU_LIMIT, SWIGLU_LIMIT)
        act = glu * _sigmoid(SWIGLU_ALPHA * glu) * (lin + 1.0)
        act_scr[rows_of(s, n), pl.ds(pl.multiple_of(j * MOE_TF, MOE_TF), MOE_TF)] = act.astype(BF16)

    n_grp = (ns - 1) // MOE_GROUP
    rem = (ns - 1) % MOE_GROUP
    has2 = rem >= 2
    has1 = rem % 2 == 1
    tail2_at = 1 + MOE_GROUP * n_grp
    tail1_at = tail2_at + jnp.where(has2, 2, 0)

    def cast_up_weights():
        wg_bf[...] = wg_ref[...].astype(BF16)
        wl_bf[...] = wl_ref[...].astype(BF16)

    @pl.when(jnp.logical_and(ns > 0, j == 0))
    def _():
        x_copy(0, 0).start()
        cast_up_weights()

        @pl.when(ns > 1)
        def _():
            x_copy(1, 1).start()

        x_copy(0, 0).wait()
        x_scr[rows_of(0), :] = xst[0].astype(BF16)
        up_block(0)

        def body(s, carry):
            slot = s % 2

            @pl.when(s + 1 < ns)
            def _():
                x_copy(s + 1, 1 - slot).start()

            x_copy(s, slot).wait()
            x_scr[rows_of(s), :] = xst[slot].astype(BF16)
            up_block(s)
            return carry

        lax.fori_loop(1, ns, body, 0)

    @pl.when(jnp.logical_and(ns > 0, jnp.logical_and(j > 0, j < ju)))
    def _():
        cast_up_weights()
        up_block(0)

        def body(p, carry):
            up_block(1 + MOE_GROUP * p, MOE_GROUP)
            return carry

        lax.fori_loop(0, n_grp, body, 0)

        @pl.when(has2)
        def _():
            up_block(tail2_at, 2)

        @pl.when(has1)
        def _():
            up_block(tail1_at)

    @pl.when(jnp.logical_and(ns > 0, j >= ju))
    def _():
        col = pl.multiple_of((j - ju) * MOE_TN, MOE_TN)

        def y_copy(stage, sem, s, n):
            return pltpu.make_async_copy(
                stage, ys_ref.at[pl.ds((b0 + s) * rb, n * rb), pl.ds(col, MOE_TN)], sem)

        def yg_copy(p, slot):
            return y_copy(ystg.at[slot], ysemg.at[slot], 1 + MOE_GROUP * p, MOE_GROUP)

        def down(s, n=1):
            return jnp.dot(act_scr[rows_of(s, n), :], wd_bf[...], preferred_element_type=F32) + bd_ref[...]

        wd_bf[...] = wd_ref[...].astype(BF16)
        yst1[0] = down(0)
        y_copy(yst1.at[0], ysem1.at[0], 0, 1).start()

        def body(p, carry):
            slot = p % 2

            @pl.when(p >= 2)
            def _():
                yg_copy(p - 2, slot).wait()

            ystg[slot] = down(1 + MOE_GROUP * p, MOE_GROUP)
            yg_copy(p, slot).start()
            return carry

        lax.fori_loop(0, n_grp, body, 0)

        @pl.when(has2)
        def _():
            yst2[...] = down(tail2_at, 2)
            y_copy(yst2, ysem2.at[0], tail2_at, 2).start()

        @pl.when(has1)
        def _():
            yst1[1] = down(tail1_at)
            y_copy(yst1.at[1], ysem1.at[1], tail1_at, 1).start()

        @pl.when(n_grp >= 2)
        def _():
            yg_copy(n_grp - 2, n_grp % 2).wait()

        @pl.when(n_grp >= 1)
        def _():
            yg_copy(n_grp - 1, (n_grp - 1) % 2).wait()

        y_copy(yst1.at[0], ysem1.at[0], 0, 1).wait()

        @pl.when(has2)
        def _():
            y_copy(yst2, ysem2.at[0], tail2_at, 2).wait()

        @pl.when(has1)
        def _():
            y_copy(yst1.at[1], ysem1.at[1], tail1_at, 1).wait()


def _moe_experts(units, xs, w_up, b_up, w_down, b_down):
    n_slots, d = xs.shape
    n_units = units.shape[1]
    dff = w_down.shape[1]
    ju, jd = dff // MOE_TF, d // MOE_TN

    def up_idx(off):
        return lambda u, j, un: (un[0, u], 0, off + jnp.where(un[2, u] > 0, jnp.minimum(j, ju - 1), ju - 1))

    def dn_idx(u, j, un):
        return (un[0, u], 0, jnp.where(un[2, u] > 0, jnp.maximum(j - ju, 0), jd - 1))

    return pl.pallas_call(
        functools.partial(_moe_expert_kernel, ju=ju),
        out_shape=jax.ShapeDtypeStruct((n_slots, d), F32),
        grid_spec=pltpu.PrefetchScalarGridSpec(
            num_scalar_prefetch=1, grid=(n_units, ju + jd),
            in_specs=[pl.BlockSpec(memory_space=pl.ANY),
                      pl.BlockSpec((None, d, MOE_TF), up_idx(0)),
                      pl.BlockSpec((None, d, MOE_TF), up_idx(ju)),
                      pl.BlockSpec((None, 1, MOE_TF), up_idx(0)),
                      pl.BlockSpec((None, 1, MOE_TF), up_idx(ju)),
                      pl.BlockSpec((None, dff, MOE_TN), dn_idx),
                      pl.BlockSpec((None, 1, MOE_TN), dn_idx)],
            out_specs=pl.BlockSpec(memory_space=pl.ANY),
            scratch_shapes=[pltpu.VMEM((MOE_RC, d), BF16), pltpu.VMEM((MOE_RC, dff), BF16),
                            pltpu.VMEM((2, MOE_RB, d), F32), pltpu.VMEM((2, MOE_RB, MOE_TN), F32),
                            pltpu.VMEM((2 * MOE_RB, MOE_TN), F32),
                            pltpu.VMEM((2, MOE_GROUP * MOE_RB, MOE_TN), F32),
                            pltpu.VMEM((d, MOE_TF), BF16), pltpu.VMEM((d, MOE_TF), BF16),
                            pltpu.VMEM((dff, MOE_TN), BF16),
                            pltpu.SemaphoreType.DMA((2,)), pltpu.SemaphoreType.DMA((2,)),
                            pltpu.SemaphoreType.DMA((1,)), pltpu.SemaphoreType.DMA((2,))]),
        compiler_params=_cparams(("arbitrary", "arbitrary")),
        name="moe_experts",
    )(units, xs, w_up, w_up, b_up, b_up, w_down, b_down)


COMBINE_SUB_ROWS = 128


def _moe_combine_kernel(dest_ref, g4_ref, x1_ref, ys_ref, p_ref, wg_ref, wp_ref, g_ref, b_ref,
                        y_ref, gbuf, sem, *, alpha):
    tm = x1_ref.shape[0]

    def issue(r, carry):
        for k in range(TOP_K):
            pltpu.make_async_copy(ys_ref.at[pl.ds(dest_ref[r * TOP_K + k], 1)],
                                  gbuf.at[k, pl.ds(r, 1)], sem).start()
        return carry

    lax.fori_loop(0, tm, issue, 0)
    for k in range(TOP_K):
        pltpu.make_async_copy(ys_ref.at[pl.ds(0, tm)], gbuf.at[k], sem).wait()
    sub = min(tm, COMBINE_SUB_ROWS)
    for r in range(0, tm, sub):
        rows = slice(r, r + sub)
        g4 = g4_ref[rows, :]
        moe = g4[:, 0:1] * gbuf[0, rows, :]
        for k in range(1, TOP_K):
            moe = moe + g4[:, k:k + 1] * gbuf[k, rows, :]
        x2 = _layer_norm(alpha * x1_ref[rows, :] + moe, g_ref[...], b_ref[...])
        gate = _sigmoid(jnp.dot(x2.astype(BF16), wg_ref[...], preferred_element_type=F32))
        y_ref[rows, :] = x2 + gate * jnp.dot(p_ref[rows, :].astype(BF16), wp_ref[...],
                                             preferred_element_type=F32)


def _moe_combine(dest_flat, g4, x1, ys, p, w_gate, w_proj, g, b, alpha, row0, t):
    d = x1.shape[1]
    tm = _pick(t, (2 * COMBINE_SUB_ROWS, COMBINE_SUB_ROWS))
    assert row0 % tm == 0
    r0 = row0 // tm
    pd = p.shape[1]
    return pl.pallas_call(
        functools.partial(_moe_combine_kernel, alpha=alpha),
        out_shape=jax.ShapeDtypeStruct((t, d), F32),
        grid=(t // tm,),
        in_specs=[pl.BlockSpec((tm * TOP_K,), lambda i: (r0 + i,), memory_space=pltpu.SMEM),
                  pl.BlockSpec((tm, LANES), lambda i: (r0 + i, 0)),
                  pl.BlockSpec((tm, d), lambda i: (r0 + i, 0)),
                  pl.BlockSpec(memory_space=pl.ANY),
                  pl.BlockSpec((tm, pd), lambda i: (i, 0)),
                  _const_spec((d, d)), _const_spec((pd, d)), _const_spec((1, d)), _const_spec((1, d))],
        out_specs=pl.BlockSpec((tm, d), lambda i: (i, 0)),
        scratch_shapes=[pltpu.VMEM((TOP_K, tm, d), F32), pltpu.SemaphoreType.DMA(())],
        compiler_params=_cparams(("arbitrary",)),
        name="moe_combine",
    )(dest_flat, g4, x1, ys, p, w_gate, w_proj, g, b)


def _project_all(x, w, pos, prompt):
    t128, t64, tkw = _rope_tables(pos)
    xb = x.astype(BF16)
    dq = N_HEADS * HEAD_DIM
    dkv = N_KV * HEAD_DIM
    dqi = IDX_HEADS * IDX_DIM
    d = x.shape[1]
    (q,) = _proj(xb, w, 0, dq, t128, "rope128", ((BF16, None),), scale=LOG2_E * HEAD_DIM ** -0.5)
    k32, kb = _proj(xb, w, dq, dkv, t128, "rope128",
                    ((F32, None), (BF16, "transposed" if prompt else None)))
    v32, vb = _proj(xb, w, dq + dkv, dkv, (), "plain",
                    ((F32, None), (BF16, "with_ones" if prompt else None)))
    (qi,) = _proj(xb, w, dq + 2 * dkv, dqi, t64, "rope64", ((BF16, None),))
    c0 = dq + 2 * dkv + dqi
    (kw,) = _proj(xb, w, c0, LANES, tkw, "rope64", ((F32, None),))
    (zr,) = _proj(xb, w, c0, 4 * d, (), "plain", ((F32, None),), shift=IDX_DIM + IDX_HEADS)
    return q, k32, kb, v32, vb, qi, zr, kw


def kernel(x_prompt, x_sample, cache_k, cache_v, cache_kidx, state_h, state_conv, page_table, p_prompt, p_sample, w_in, conv_w, conv_b, rg_wa, rg_ba, rg_wx, rg_bx, rg_lambda, w_o_rnn, w_o_att, w_out, ln1_g, ln1_b, w_router, b_router, w_up, b_up, w_down, b_down, ln2_g, ln2_b, w_ple_gate, w_ple_proj):
    bsz, s, d = x_prompt.shape
    n, dec_t, _ = x_sample.shape
    assert bsz == 1 and dec_t == 1 and s % LANES == 0 and n % LANES == 0
    depth = w_in.shape[0]
    npg = page_table.shape[1]
    n_phys, page = cache_k.shape[1], cache_k.shape[2]
    past = npg * page
    alpha = (2 * depth) ** 0.25
    dq, dkv, dqi = N_HEADS * HEAD_DIM, N_KV * HEAD_DIM, IDX_HEADS * IDX_DIM
    t_all = s + n
    n_slots, n_units = _moe_sizes(t_all)
    pos_p = jnp.arange(s, dtype=I32)
    pos_s = jnp.full((n,), past, I32)
    row = lambda a: a[None, :]

    hp, hs = x_prompt[0], x_sample[:, 0]
    st_p = [[], [], [], [], []]
    st_s = [[], [], [], [], []]
    for i in range(depth):
        wa, wx = rg_wa[i].astype(BF16), rg_wx[i].astype(BF16)
        rnn_w = (conv_w[i], row(conv_b[i]), wa, wx, row(rg_ba[i]), row(rg_bx[i]), row(rg_lambda[i]))
        wor, woa, wo = w_o_rnn[i].astype(BF16), w_o_att[i].astype(BF16), w_out[i].astype(BF16)
        wr = jnp.pad(w_router[i], ((0, 0), (0, LANES - N_EXPERTS)))
        wr_hi = wr.astype(BF16)
        wr = jnp.concatenate([wr_hi, (wr - wr_hi.astype(F32)).astype(BF16)], axis=1)
        br = jnp.pad(b_router[i], (0, LANES - N_EXPERTS))[None, :]
        ln1 = (row(ln1_g[i]), row(ln1_b[i]))

        w_in_t = jnp.swapaxes(w_in[i], 0, 1)
        q, k32, kt, v32, vx, qi, zr, kw = _project_all(hp, w_in_t, pos_p, True)
        y_rnn, h_p = _rnn_prompt(zr, s, *rnn_w)
        kit = kw[:, :IDX_DIM].T.astype(BF16)
        y_att = _attn_prompt(qi, kw, q, jnp.concatenate([kit, kit], 0), kt, vx, s,
                             min(TOPK_MAX, s // 4))
        m = _merge(y_rnn, y_att, wor, woa, zr)
        bufs = _ln1_router(m, hp, wo, *ln1, wr, br, alpha, 0, t_all)
        st = (k32.reshape(1, s, N_KV, HEAD_DIM), v32.reshape(1, s, N_KV, HEAD_DIM),
              kw[None, :, :IDX_DIM], h_p, zr[None, s - (CONV_W - 1):s, :d])
        for lst, a in zip(st_p, st):
            lst.append(a)

        q, k32, kb, v32, vb, qi, zr, kw = _project_all(hs, w_in_t, pos_s, False)
        y_rnn, h_s = _rnn_sample(zr, state_conv[i], state_h[i], *rnn_w)
        sc = _idx_sample(page_table, qi.reshape(n, IDX_HEADS, IDX_DIM),
                         kw[:, IDX_DIM:IDX_DIM + IDX_HEADS, None], kw[:, None, :],
                         jnp.swapaxes(cache_kidx[i], 1, 2))
        mask = _mask_sample(sc[:, 0], past + 1, min(TOPK_MAX, (past + 1) // 4))
        y_att = _attn_sample(page_table, q.reshape(n, N_HEADS, HEAD_DIM),
                             jnp.repeat(mask[:, :past], N_KV, axis=1)[:, None, :],
                             mask[:, None, past:past + LANES],
                             kb.reshape(n, N_KV, HEAD_DIM), vb.reshape(n, N_KV, HEAD_DIM),
                             cache_k[i].reshape(n_phys, page * N_KV, HEAD_DIM),
                             cache_v[i].reshape(n_phys, page * N_KV, HEAD_DIM)).reshape(n, dq)
        m = _merge(y_rnn, y_att, wor, woa, zr)
        x1, sel, gate = _ln1_router(m, hs, wo, *ln1, wr, br, alpha, s, t_all, prev=bufs)
        st = (k32.reshape(n, 1, N_KV, HEAD_DIM), v32.reshape(n, 1, N_KV, HEAD_DIM),
              kw[:, None, :IDX_DIM], h_s,
              jnp.concatenate([state_conv[i][:, 1:], zr[:, None, :d]], axis=1))
        for lst, a in zip(st_s, st):
            lst.append(a)

        rank, cnt = _moe_rank(sel)
        dest4, g4, tab = _moe_dest(sel, gate, rank, cnt)
        units = _moe_units(tab, n_units)
        dest_flat = dest4[:, :TOP_K].reshape(-1)
        xs = _moe_scatter(dest_flat, x1, n_slots)
        ys = _moe_experts(units, xs, w_up[i], b_up[i][:, None, :], w_down[i], b_down[i][:, None, :])
        tail = (w_ple_gate[i].astype(BF16), w_ple_proj[i].astype(BF16), row(ln2_g[i]), row(ln2_b[i]),
                alpha)
        hp = _moe_combine(dest_flat, g4, x1, ys, p_prompt[i][0], *tail, 0, s)
        hs = _moe_combine(dest_flat, g4, x1, ys, p_sample[i][:, 0], *tail, s, n)

    outs_p = [jnp.stack(l) for l in st_p]
    outs_s = [jnp.stack(l) for l in st_s]
    return (hp[None], hs[:, None], *outs_p, *outs_s)
```

```python
import functools

import jax
import jax.numpy as jnp
from jax import lax
from jax.experimental import pallas as pl
from jax.experimental.pallas import tpu as pltpu

F32 = jnp.float32
BF16 = jnp.bfloat16
I32 = jnp.int32

N_HEADS = 16
HEAD_DIM = 128
N_KV = 4
Q_PER_KV = N_HEADS // N_KV
IDX_HEADS = 16
IDX_DIM = 64
TOPK_MAX = 256
Q_BLOCK = 128
ROPE_THETA = 10000.0
RNN_BW = 128
CONV_W = 4
RG_C = 8.0
N_EXPERTS = 32
TOP_K = 4
SWIGLU_LIMIT = 7.0
SWIGLU_ALPHA = 1.702
LN_EPS = 1e-5
LANES = 128
INT_MIN = -(2 ** 31)
NEG_BIG = -1e30
LOG2_E = 1.4426950408889634
VMEM_LIMIT = 56 * 1024 * 1024


def _pick(n, cands):
    for c in cands:
        if n % c == 0:
            return c
    return n


def _cparams(sem):
    return pltpu.CompilerParams(dimension_semantics=sem, vmem_limit_bytes=VMEM_LIMIT)


def _const_spec(shape):
    nd = len(shape)
    return pl.BlockSpec(shape, lambda *a: (0,) * nd, pipeline_mode=pl.Buffered(1))


def _proj_kernel(*refs, mode, kinds, shift, n_tab, scale):
    n_w = 2 if shift else 1
    x_ref, w_refs = refs[0], refs[1:1 + n_w]
    tabs = refs[1 + n_w:1 + n_w + n_tab]
    outs = refs[1 + n_w + n_tab:1 + n_w + n_tab + len(kinds)]
    w_bf = refs[-1]
    tn = w_bf.shape[1]

    @pl.when(pl.program_id(1) == 0)
    def _():
        for c in range(0, tn, LANES):
            lo = c + shift
            if lo + LANES <= tn:
                rows = w_refs[0][lo:lo + LANES, :]
            elif lo >= tn:
                rows = w_refs[1][lo - tn:lo - tn + LANES, :]
            else:
                rows = jnp.concatenate([w_refs[0][lo:tn, :], w_refs[1][0:lo + LANES - tn, :]], axis=0)
            w_bf[:, c:c + LANES] = rows.T.astype(BF16)

    z = jnp.dot(x_ref[...], w_bf[...], preferred_element_type=F32)
    for h in range(tn // LANES):
        zh = z[:, h * LANES:(h + 1) * LANES]
        if mode == "plain":
            r = zh
        elif mode == "rope128":
            r = zh * tabs[0][...] + pltpu.roll(zh, 64, 1) * tabs[1][...]
        else:
            r = (zh * tabs[0][...] + pltpu.roll(zh, 96, 1) * tabs[1][...]
                 + pltpu.roll(zh, 32, 1) * tabs[2][...])
        if scale != 1.0:
            r = r * scale
        for o, kind in zip(outs, kinds):
            if kind == "transposed":
                o[h * LANES:(h + 1) * LANES, :] = r.T.astype(o.dtype)
            elif kind == "with_ones":
                lane = lax.broadcasted_iota(I32, r.shape, 1)
                o[:, 2 * h * LANES:(2 * h + 1) * LANES] = r.astype(o.dtype)
                o[:, (2 * h + 1) * LANES:(2 * h + 2) * LANES] = jnp.where(lane == 0, 1.0, 0.0).astype(o.dtype)
            else:
                o[:, h * LANES:(h + 1) * LANES] = r.astype(o.dtype)


def _proj(x, wt, col0, ncols, tabs, mode, outs, scale=1.0, shift=0):
    t, k = x.shape
    tm = _pick(t, (1024, 512, 256, 128))
    tn = _pick(ncols, (1024, 512, 256, 128))
    assert col0 % tn == 0 and shift % 8 == 0 and shift < LANES
    c0 = col0 // tn
    in_specs = [pl.BlockSpec((tm, k), lambda j, i: (i, 0)),
                pl.BlockSpec((tn, k), lambda j, i: (c0 + j, 0))]
    ws = [wt]
    if shift:
        in_specs.append(pl.BlockSpec((LANES, k), lambda j, i: ((c0 + j + 1) * (tn // LANES), 0)))
        ws.append(wt)
    in_specs += [pl.BlockSpec((tm, LANES), lambda j, i: (i, 0)) for _ in tabs]
    shapes, specs = [], []
    for dt, kind in outs:
        if kind == "transposed":
            shapes.append(jax.ShapeDtypeStruct((ncols, t), dt))
            specs.append(pl.BlockSpec((tn, tm), lambda j, i: (j, i)))
        elif kind == "with_ones":
            shapes.append(jax.ShapeDtypeStruct((t, 2 * ncols), dt))
            specs.append(pl.BlockSpec((tm, 2 * tn), lambda j, i: (i, j)))
        else:
            shapes.append(jax.ShapeDtypeStruct((t, ncols), dt))
            specs.append(pl.BlockSpec((tm, tn), lambda j, i: (i, j)))
    return pl.pallas_call(
        functools.partial(_proj_kernel, mode=mode, kinds=tuple(kd for _, kd in outs), shift=shift,
                          n_tab=len(tabs), scale=scale),
        out_shape=shapes,
        grid=(ncols // tn, t // tm),
        in_specs=in_specs,
        out_specs=specs,
        scratch_shapes=[pltpu.VMEM((k, tn), BF16)],
        compiler_params=_cparams(("parallel", "arbitrary")),
        name="proj_" + mode,
    )(x, *ws, *tabs)


def _rope_tables(pos):
    posf = pos.astype(F32)[:, None]
    h128 = HEAD_DIM // 2
    inv = ROPE_THETA ** (-jnp.arange(h128, dtype=F32) / h128)
    c, s = jnp.cos(posf * inv), jnp.sin(posf * inv)
    t128 = (jnp.concatenate([c, c], 1), jnp.concatenate([-s, s], 1))
    h64 = IDX_DIM // 2
    inv = ROPE_THETA ** (-jnp.arange(h64, dtype=F32) / h64)
    c, s = jnp.cos(posf * inv), jnp.sin(posf * inv)
    z = jnp.zeros_like(s)
    c64 = jnp.concatenate([c, c], 1)
    sa64 = jnp.concatenate([-s, z], 1)
    sb64 = jnp.concatenate([z, s], 1)
    t64 = tuple(jnp.concatenate([a, a], 1) for a in (c64, sa64, sb64))
    idx_scale = (IDX_HEADS * IDX_DIM) ** -0.5
    n = pos.shape[0]
    ck = jnp.concatenate([c64, jnp.full((n, IDX_HEADS), idx_scale, F32),
                          jnp.zeros((n, LANES - IDX_DIM - IDX_HEADS), F32)], 1)
    z64 = jnp.zeros((n, LANES - IDX_DIM), F32)
    tkw = (ck, jnp.concatenate([sa64, z64], 1), jnp.concatenate([sb64, z64], 1))
    return t128, t64, tkw


def _sigmoid(x):
    return 0.5 * jnp.tanh(0.5 * x) + 0.5


def _softplus_neg(lam):
    return jnp.maximum(-lam, 0.0) + jnp.log1p(jnp.exp(-jnp.abs(lam)))


def _gelu_tanh(x):
    return 0.5 * x * (1.0 + jnp.tanh(0.7978845608028654 * (x + 0.044715 * (x * x * x))))


def _rglru_gates(xc, wa_ref, wx_ref, ba, bx, lam):
    nblk = xc.shape[1] // RNN_BW
    rs, gs = [], []
    for j in range(nblk):
        xj = xc[:, j * RNN_BW:(j + 1) * RNN_BW].astype(BF16)
        rs.append(jnp.dot(xj, wa_ref[j], preferred_element_type=F32))
        gs.append(jnp.dot(xj, wx_ref[j], preferred_element_type=F32))
    r = _sigmoid(jnp.concatenate(rs, 1) + ba)
    g = _sigmoid(jnp.concatenate(gs, 1) + bx)
    log_a = -RG_C * r * _softplus_neg(lam)
    a = jnp.exp(log_a)
    mult = jnp.sqrt(jnp.tanh(-log_a) * (1.0 + a * a))
    return a, mult, g * xc


def _rnn_prompt_kernel(xr_ref, gr_ref, cw_ref, cb_ref, wa_ref, wx_ref, ba_ref, bx_ref, lam_ref,
                       y_ref, hl_ref, xbuf, hcar, a_scr, b_scr, h_scr):
    t = pl.program_id(1)
    tm, cw = xr_ref.shape

    @pl.when(t == 0)
    def _():
        xbuf[0:8, :] = jnp.zeros((8, cw), F32)
        hcar[...] = jnp.zeros_like(hcar)

    x = xr_ref[...]
    xbuf[8:8 + tm, :] = x
    w = cw_ref[...]
    xc = (cb_ref[...] + w[3:4] * x + w[2:3] * xbuf[7:7 + tm, :]
          + w[1:2] * xbuf[6:6 + tm, :] + w[0:1] * xbuf[5:5 + tm, :])
    xbuf[0:8, :] = x[tm - 8:tm, :]
    a, mult, gx = _rglru_gates(xc, wa_ref, wx_ref, ba_ref[...], bx_ref[...], lam_ref[...])
    pos = t * tm + lax.broadcasted_iota(I32, (tm, cw), 0)
    mult = jnp.where(pos == 0, 1.0, mult)
    a_scr[...] = a
    b_scr[...] = mult * gx
    row8 = lax.broadcasted_iota(I32, (8, cw), 0)

    def group(g, carry):
        r0 = pl.multiple_of(g * 8, 8)
        av = a_scr[pl.ds(r0, 8), :]
        bv = b_scr[pl.ds(r0, 8), :]
        for d in (1, 2, 4):
            a_s = pltpu.roll(av, d, 0)
            b_s = pltpu.roll(bv, d, 0)
            m = row8 >= d
            bv = jnp.where(m, av * b_s + bv, bv)
            av = jnp.where(m, av * a_s, av)
        h = av * carry + bv
        h_scr[pl.ds(r0, 8), :] = h
        return h[7:8, :]

    carry = lax.fori_loop(0, tm // 8, group, hcar[0:1, :])
    hcar[0:1, :] = carry
    y_ref[...] = (_gelu_tanh(gr_ref[...]) * h_scr[...]).astype(y_ref.dtype)

    @pl.when(t == pl.num_programs(1) - 1)
    def _():
        hl_ref[...] = carry


def _rnn_prompt(zr, s, conv_w, conv_b, wa, wx, ba, bx, lam):
    d = conv_w.shape[1]
    cw = 512
    tm = _pick(s, (256, 128))
    ncb = d // cw
    nb = cw // RNN_BW
    vec = lambda: pl.BlockSpec((1, cw), lambda c, t: (0, c))
    y, hl = pl.pallas_call(
        _rnn_prompt_kernel,
        out_shape=[jax.ShapeDtypeStruct((s, d), BF16), jax.ShapeDtypeStruct((1, d), F32)],
        grid=(ncb, s // tm),
        in_specs=[pl.BlockSpec((tm, cw), lambda c, t: (t, c)),
                  pl.BlockSpec((tm, cw), lambda c, t: (t, ncb + c)),
                  pl.BlockSpec((CONV_W, cw), lambda c, t: (0, c)),
                  vec(),
                  pl.BlockSpec((nb, RNN_BW, RNN_BW), lambda c, t: (c, 0, 0)),
                  pl.BlockSpec((nb, RNN_BW, RNN_BW), lambda c, t: (c, 0, 0)),
                  vec(), vec(), vec()],
        out_specs=[pl.BlockSpec((tm, cw), lambda c, t: (t, c)),
                   pl.BlockSpec((1, cw), lambda c, t: (0, c))],
        scratch_shapes=[pltpu.VMEM((tm + 8, cw), F32), pltpu.VMEM((8, cw), F32),
                        pltpu.VMEM((tm, cw), F32), pltpu.VMEM((tm, cw), F32),
                        pltpu.VMEM((tm, cw), F32)],
        compiler_params=_cparams(("parallel", "arbitrary")),
        name="rnn_prompt",
    )(zr, zr, conv_w, conv_b, wa, wx, ba, bx, lam)
    return y, hl


def _rnn_sample_kernel(xr_ref, gr_ref, c0_ref, c1_ref, c2_ref, h0_ref, cw_ref, cb_ref,
                       wa_ref, wx_ref, ba_ref, bx_ref, lam_ref, y_ref, h_ref):
    w = cw_ref[...]
    xc = (cb_ref[...] + w[3:4] * xr_ref[...] + w[2:3] * c2_ref[...]
          + w[1:2] * c1_ref[...] + w[0:1] * c0_ref[...])
    a, mult, gx = _rglru_gates(xc, wa_ref, wx_ref, ba_ref[...], bx_ref[...], lam_ref[...])
    h = a * h0_ref[...] + mult * gx
    h_ref[...] = h
    y_ref[...] = (_gelu_tanh(gr_ref[...]) * h).astype(y_ref.dtype)


def _rnn_sample(zr, conv_state, h0, conv_w, conv_b, wa, wx, ba, bx, lam):
    n, d = h0.shape
    cw = 512
    ncb = d // cw
    nb = cw // RNN_BW
    blk = lambda off: pl.BlockSpec((n, cw), lambda c: (0, off + c))
    vec = lambda: pl.BlockSpec((1, cw), lambda c: (0, c))
    wsp = lambda: pl.BlockSpec((nb, RNN_BW, RNN_BW), lambda c: (c, 0, 0))
    return pl.pallas_call(
        _rnn_sample_kernel,
        out_shape=[jax.ShapeDtypeStruct((n, d), BF16), jax.ShapeDtypeStruct((n, d), F32)],
        grid=(ncb,),
        in_specs=[blk(0), blk(ncb), blk(0), blk(0), blk(0), blk(0),
                  pl.BlockSpec((CONV_W, cw), lambda c: (0, c)), vec(), wsp(), wsp(),
                  vec(), vec(), vec()],
        out_specs=[blk(0), blk(0)],
        compiler_params=_cparams(("parallel",)),
        name="rnn_sample",
    )(zr, zr, conv_state[:, 0], conv_state[:, 1], conv_state[:, 2], h0,
      conv_w, conv_b, wa, wx, ba, bx, lam)


def _layer_norm(x, g, b):
    mu = jnp.mean(x, axis=-1, keepdims=True)
    xc = x - mu
    var = jnp.mean(xc * xc, axis=-1, keepdims=True)
    return xc * lax.rsqrt(var + LN_EPS) * g + b


def _merge_kernel(yr_ref, ya_ref, wr_ref, wa_ref, ga_ref, gb_ref, o_ref):
    a = jnp.dot(yr_ref[...], wr_ref[...], preferred_element_type=F32)
    b = jnp.dot(ya_ref[...], wa_ref[...], preferred_element_type=F32)
    m = _sigmoid(ga_ref[...]) * a + _sigmoid(gb_ref[...]) * b
    o_ref[...] = m.astype(o_ref.dtype)


def _merge(y_rnn, y_att, w_o_rnn, w_o_att, zr):
    t, d = y_rnn.shape
    tm = _pick(t, (512, 256, 128))
    tn = 512
    nj = d // tn
    return pl.pallas_call(
        _merge_kernel,
        out_shape=jax.ShapeDtypeStruct((t, d), BF16),
        grid=(t // tm, nj),
        in_specs=[pl.BlockSpec((tm, d), lambda i, j: (i, 0)),
                  pl.BlockSpec((tm, d), lambda i, j: (i, 0)),
                  pl.BlockSpec((d, tn), lambda i, j: (0, j)),
                  pl.BlockSpec((d, tn), lambda i, j: (0, j)),
                  pl.BlockSpec((tm, tn), lambda i, j: (i, 2 * nj + j)),
                  pl.BlockSpec((tm, tn), lambda i, j: (i, 3 * nj + j))],
        out_specs=pl.BlockSpec((tm, tn), lambda i, j: (i, j)),
        compiler_params=_cparams(("parallel", "parallel")),
        name="merge",
    )(y_rnn, y_att, w_o_rnn, w_o_att, zr, zr)


LN1_SUB_ROWS = 256


def _ln1_router_kernel(*refs, alpha, n_skip):
    m_ref, x_ref, w_ref, g_ref, b_ref, wr_ref, br_ref, x1_ref, sel_ref, gate_ref = refs[n_skip:]
    tm = x_ref.shape[0]
    sub = min(tm, LN1_SUB_ROWS)
    for r in range(0, tm, sub):
        rows = slice(r, r + sub)
        y = alpha * x_ref[rows, :] + jnp.dot(m_ref[rows, :], w_ref[...], preferred_element_type=F32)
        x1 = _layer_norm(y, g_ref[...], b_ref[...])
        x1_ref[rows, :] = x1
        x_hi = x1.astype(BF16)
        x_lo = (x1 - x_hi.astype(F32)).astype(BF16)
        part = (jnp.dot(x_hi, wr_ref[...], preferred_element_type=F32)
                + jnp.dot(x_lo, wr_ref[...], preferred_element_type=F32))
        logits = part[:, :LANES] + part[:, LANES:] + br_ref[...]
        lane = lax.broadcasted_iota(I32, logits.shape, 1)
        live = lane < N_EXPERTS
        cur = jnp.where(live, logits, -jnp.inf)
        top = jnp.max(cur, axis=1, keepdims=True)
        sel = jnp.zeros(logits.shape, jnp.bool_)
        for _ in range(TOP_K):
            mx = jnp.max(cur, axis=1, keepdims=True)
            first = jnp.min(jnp.where(cur == mx, lane, LANES), axis=1, keepdims=True)
            pick = lane == first
            sel = jnp.logical_or(sel, pick)
            cur = jnp.where(pick, -jnp.inf, cur)
        e = jnp.where(sel, jnp.exp(logits - top), 0.0)
        sel_ref[rows, :] = jnp.where(sel, 1.0, 0.0)
        gate_ref[rows, :] = e / jnp.sum(e, axis=1, keepdims=True)


def _ln1_router(m, x, w_out, g, b, w_router, b_router, alpha, row0, t_all, prev=None):
    t, d = x.shape
    tm = _pick(t, (2 * LN1_SUB_ROWS, LN1_SUB_ROWS, 128))
    assert row0 % tm == 0
    r0 = row0 // tm
    in_specs = [pl.BlockSpec((tm, d), lambda i: (i, 0)),
                pl.BlockSpec((tm, d), lambda i: (i, 0)),
                _const_spec((d, d)), _const_spec((1, d)), _const_spec((1, d)),
                _const_spec((d, 2 * LANES)), _const_spec((1, LANES))]
    args = [m, x, w_out, g, b, w_router, b_router]
    aliases = {}
    if prev is not None:
        in_specs = [pl.BlockSpec(memory_space=pl.ANY)] * 3 + in_specs
        args = list(prev) + args
        aliases = {0: 0, 1: 1, 2: 2}

    return pl.pallas_call(
        functools.partial(_ln1_router_kernel, alpha=alpha, n_skip=len(aliases)),
        out_shape=[jax.ShapeDtypeStruct((t_all, d), F32),
                   jax.ShapeDtypeStruct((t_all, LANES), F32),
                   jax.ShapeDtypeStruct((t_all, LANES), F32)],
        grid=(t // tm,),
        in_specs=in_specs,
        out_specs=[pl.BlockSpec((tm, d), lambda i: (r0 + i, 0)),
                   pl.BlockSpec((tm, LANES), lambda i: (r0 + i, 0)),
                   pl.BlockSpec((tm, LANES), lambda i: (r0 + i, 0))],
        input_output_aliases=aliases,
        compiler_params=_cparams(("parallel",)),
        name="ln1_router",
    )(*args)


def _sort_key(x):
    bits = lax.bitcast_convert_type(x, I32)
    return bits ^ (jnp.right_shift(bits, 31) & 0x7FFFFFFF)


def _count(keys_ref, nch, cw, tvec, strict):
    rows = keys_ref.shape[0]
    tb = jnp.broadcast_to(tvec, (rows, LANES))

    def body(c, cnt):
        off = c * cw
        for s in range(cw // LANES):
            k = keys_ref[:, pl.ds(pl.multiple_of(off + s * LANES, LANES), LANES)]
            hit = (k > tb) if strict else (k >= tb)
            cnt = cnt + jnp.where(hit, 1.0, 0.0)
        return cnt

    cnt = lax.fori_loop(0, nch, body, jnp.zeros((rows, LANES), F32))
    return jnp.sum(cnt, axis=1, keepdims=True)


def _select_threshold(keys_ref, nch, cw, n_sel, few):
    rows = keys_ref.shape[0]
    settled = few > 0.5

    def cond(st):
        b, _, cnt_cur = st
        open_rows = jnp.where(jnp.logical_or(settled, cnt_cur == n_sel), 0.0, 1.0)
        return jnp.logical_and(b < 32, jnp.max(open_rows) > 0.0)

    def bit_step(st):
        b, cur, cnt_cur = st
        cand = cur | lax.shift_left(jnp.int32(1), 31 - b)
        cnt = _count(keys_ref, nch, cw, cand ^ INT_MIN, False)
        take = cnt >= n_sel
        return b + 1, jnp.where(take, cand, cur), jnp.where(take, cnt, cnt_cur)

    everything = jnp.zeros((rows, 1), F32) + jnp.asarray(nch * cw, F32)
    _, cur, n_ge = lax.while_loop(cond, bit_step, (jnp.int32(0), jnp.zeros((rows, 1), I32), everything))
    t = cur ^ INT_MIN
    tied = jnp.where(jnp.logical_or(settled, n_ge <= n_sel), 0.0, 1.0)

    @pl.when(jnp.max(tied) > 0.0)
    def _():
        n_gt = _count(keys_ref, nch, cw, t, True)
        tb = jnp.broadcast_to(t, (rows, LANES))
        needb = jnp.broadcast_to(jnp.where(tied > 0.5, n_sel - n_gt, 1e9), (rows, LANES))
        r = lax.broadcasted_iota(I32, (LANES, LANES), 0)
        c = lax.broadcasted_iota(I32, (LANES, LANES), 1)
        upper = jnp.where(r < c, 1.0, 0.0).astype(BF16)

        def fix(j, run):
            sl = pl.ds(pl.multiple_of(j * LANES, LANES), LANES)
            k = keys_ref[:, sl]
            eq = k == tb
            eqf = jnp.where(eq, 1.0, 0.0)
            before = jnp.dot(eqf.astype(BF16), upper, preferred_element_type=F32) + run
            drop = jnp.logical_and(eq, before >= needb)
            keys_ref[:, sl] = jnp.where(drop, INT_MIN, k)
            return run + jnp.sum(eqf, axis=1, keepdims=True)

        lax.fori_loop(0, nch * (cw // LANES), fix, jnp.zeros((rows, 1), F32))

    return jnp.maximum(t, INT_MIN + 1)


def _attn_prompt_kernel(qi_ref, kw_ref, q_ref, kit_ref, kt_ref, vx_ref, o_ref,
                        keys_scr, lhs_scr, wb_scr, tb_scr, qg_scr, s_scr, p_scr, m_scr, acc_scr,
                        *, n_sel, kc):
    i = pl.program_id(0)
    qb = Q_BLOCK
    nch = (i * qb + qb + kc - 1) // kc
    kw = kw_ref[...]
    lane = lax.broadcasted_iota(I32, (qb, LANES), 1)
    for h in range(IDX_HEADS):
        wb_scr[h] = jnp.broadcast_to(kw[:, IDX_DIM + h:IDX_DIM + h + 1], (qb, LANES))
        blk = qi_ref[:, (h // 2) * LANES:(h // 2 + 1) * LANES]
        keep = (lane < IDX_DIM) if h % 2 == 0 else (lane >= IDX_DIM)
        lhs_scr[h] = jnp.where(keep, blk, jnp.zeros_like(blk))

    sw = min(256, kc)

    def score_chunk(c, carry):
        for s in range(kc // sw):
            o2 = pl.multiple_of(c * kc + s * sw, sw)
            kt = kit_ref[:, pl.ds(o2, sw)]
            acc = jnp.zeros((qb, sw), F32)
            for h in range(IDX_HEADS):
                sc = jnp.dot(lhs_scr[h], kt, preferred_element_type=F32)
                wb = wb_scr[h]
                acc = acc + jnp.maximum(sc, 0.0) * jnp.concatenate([wb] * (sw // LANES), axis=1)
            kpos = o2 + lax.broadcasted_iota(I32, (qb, sw), 1)
            qpos = i * qb + lax.broadcasted_iota(I32, (qb, sw), 0)
            keys_scr[:, pl.ds(o2, sw)] = jnp.where(kpos <= qpos, _sort_key(acc), INT_MIN)
        return carry

    lax.fori_loop(0, nch, score_chunk, 0)
    n_causal = i * qb + lax.broadcasted_iota(I32, (qb, 1), 0) + 1
    t = _select_threshold(keys_scr, nch, kc, n_sel, jnp.where(n_causal <= n_sel, 1.0, 0.0))
    rows = Q_PER_KV * qb
    strip = 32
    nrep = kc // LANES
    tb_scr[...] = jnp.broadcast_to(t, (qb, LANES))

    for g in range(N_KV):
        for j in range(Q_PER_KV):
            h = g * Q_PER_KV + j
            qg_scr[g, j * qb:(j + 1) * qb, :] = q_ref[:, h * HEAD_DIM:(h + 1) * HEAD_DIM]
    m_scr[...] = jnp.full(m_scr.shape, NEG_BIG, F32)
    acc_scr[...] = jnp.zeros_like(acc_scr)

    def body(c, carry):
        off = pl.multiple_of(c * kc, kc)
        for g in range(N_KV):
            s_scr[g] = jnp.dot(qg_scr[g], kt_ref[g * HEAD_DIM:(g + 1) * HEAD_DIM, pl.ds(off, kc)],
                               preferred_element_type=F32)
            for rq in range(0, qb, strip):
                tb = jnp.concatenate([tb_scr[rq:rq + strip, :]] * nrep, axis=1)
                for j in range(Q_PER_KV):
                    r = j * qb + rq
                    km = keys_scr[rq:rq + strip, pl.ds(off, kc)] >= tb
                    s = jnp.where(km, s_scr[g, r:r + strip, :], NEG_BIG)
                    s_scr[g, r:r + strip, :] = s
                    m_old = m_scr[g, r:r + strip, :]
                    m_new = jnp.maximum(m_old, jnp.max(s, axis=1, keepdims=True))
                    alpha = jnp.exp2(m_old - m_new)
                    m_scr[g, r:r + strip, :] = m_new
                    acc_scr[g, r:r + strip, :] = (acc_scr[g, r:r + strip, :]
                                                  * jnp.concatenate([alpha] * 2, axis=1))
            for r in range(0, rows, strip):
                m_new = m_scr[g, r:r + strip, :]
                p = jnp.exp2(s_scr[g, r:r + strip, :] - jnp.concatenate([m_new] * nrep, axis=1))
                p_scr[g, r:r + strip, :] = p.astype(BF16)
            acc_scr[g] += jnp.dot(p_scr[g], vx_ref[pl.ds(off, kc), g * 2 * HEAD_DIM:(g + 1) * 2 * HEAD_DIM],
                                  preferred_element_type=F32)
        return carry

    def body_pair(c2, carry):
        body(2 * c2, carry)
        return body(2 * c2 + 1, carry)

    lax.fori_loop(0, nch // 2, body_pair, 0)

    @pl.when(nch % 2 == 1)
    def _():
        body(nch - 1, 0)

    for g in range(N_KV):
        acc = acc_scr[g]
        out = acc[:, :HEAD_DIM] / acc[:, HEAD_DIM:HEAD_DIM + 1]
        for j in range(Q_PER_KV):
            h = g * Q_PER_KV + j
            o_ref[:, h * HEAD_DIM:(h + 1) * HEAD_DIM] = out[j * qb:(j + 1) * qb].astype(o_ref.dtype)


def _attn_prompt(qi, kw, q, kit2, kt, vx, s, n_sel):
    kc = min(512, s)
    d = N_HEADS * HEAD_DIM
    rows = Q_PER_KV * Q_BLOCK
    return pl.pallas_call(
        functools.partial(_attn_prompt_kernel, n_sel=n_sel, kc=kc),
        out_shape=jax.ShapeDtypeStruct((s, d), BF16),
        grid=(s // Q_BLOCK,),
        in_specs=[pl.BlockSpec((Q_BLOCK, IDX_HEADS * IDX_DIM), lambda i: (i, 0)),
                  pl.BlockSpec((Q_BLOCK, LANES), lambda i: (i, 0)),
                  pl.BlockSpec((Q_BLOCK, d), lambda i: (i, 0)),
                  _const_spec((LANES, s)), _const_spec((N_KV * HEAD_DIM, s)),
                  _const_spec((s, N_KV * 2 * HEAD_DIM))],
        out_specs=pl.BlockSpec((Q_BLOCK, d), lambda i: (i, 0)),
        scratch_shapes=[pltpu.VMEM((Q_BLOCK, s), I32),
                        pltpu.VMEM((IDX_HEADS, Q_BLOCK, LANES), BF16),
                        pltpu.VMEM((IDX_HEADS, Q_BLOCK, LANES), F32),
                        pltpu.VMEM((Q_BLOCK, LANES), I32),
                        pltpu.VMEM((N_KV, rows, HEAD_DIM), BF16),
                        pltpu.VMEM((N_KV, rows, kc), F32),
                        pltpu.VMEM((N_KV, rows, kc), BF16),
                        pltpu.VMEM((N_KV, rows, LANES), F32),
                        pltpu.VMEM((N_KV, rows, 2 * HEAD_DIM), F32)],
        compiler_params=_cparams(("parallel",)),
        name="attn_prompt",
    )(qi, kw, q, kit2, kt, vx)


IDX_SEQS_PER_STEP = 4


def _idx_sample_kernel(pt_ref, qi_ref, w_ref, kn_ref, *refs, npg, past, group):
    pages, o_ref = refs[:group * npg], refs[group * npg]
    lane = lax.broadcasted_iota(I32, (1, LANES), 1)
    for g in range(group):
        qi = qi_ref[g]
        kp = jnp.concatenate([p[...] for p in pages[g * npg:(g + 1) * npg]], axis=1).astype(BF16)
        sc = jnp.dot(qi, kp, preferred_element_type=F32)
        w = w_ref[g]
        o_ref[g, :, 0:past] = jnp.sum(jnp.maximum(sc, 0.0) * w, axis=0, keepdims=True)
        kn = kn_ref[g][:, :IDX_DIM].astype(BF16).astype(F32)
        sn = jnp.sum(qi.astype(F32) * kn, axis=1, keepdims=True)
        new = jnp.sum(jnp.maximum(sn, 0.0) * w, axis=0, keepdims=True)
        o_ref[g, :, past:past + LANES] = jnp.where(lane == 0, new, -jnp.inf)


def _idx_sample(page_table, qi3, w3, kn3, cache_kidx_t):
    n, npg = page_table.shape
    page = cache_kidx_t.shape[2]
    past = npg * page
    group = IDX_SEQS_PER_STEP if n % IDX_SEQS_PER_STEP == 0 else 1
    page_specs = [pl.BlockSpec((None, IDX_DIM, page),
                               lambda b, pt, g=g, p=p: (pt[(b * group + g) * npg + p], 0, 0))
                  for g in range(group) for p in range(npg)]
    return pl.pallas_call(
        functools.partial(_idx_sample_kernel, npg=npg, past=past, group=group),
        out_shape=jax.ShapeDtypeStruct((n, 1, past + LANES), F32),
        grid_spec=pltpu.PrefetchScalarGridSpec(
            num_scalar_prefetch=1, grid=(n // group,),
            in_specs=[pl.BlockSpec((group, IDX_HEADS, IDX_DIM), lambda b, pt: (b, 0, 0)),
                      pl.BlockSpec((group, IDX_HEADS, 1), lambda b, pt: (b, 0, 0)),
                      pl.BlockSpec((group, 1, LANES), lambda b, pt: (b, 0, 0))] + page_specs,
            out_specs=pl.BlockSpec((group, 1, past + LANES), lambda b, pt: (b, 0, 0))),
        compiler_params=_cparams(("parallel",)),
        name="idx_sample",
    )(page_table.reshape(-1), qi3, w3, kn3, *([cache_kidx_t] * (group * npg)))


def _mask_sample_kernel(sc_ref, m_ref, keys_scr, *, n_valid, n_sel):
    rows, width = sc_ref.shape
    pos = lax.broadcasted_iota(I32, (rows, width), 1)
    keys_scr[...] = jnp.where(pos < n_valid, _sort_key(sc_ref[...]), INT_MIN)
    few = jnp.full((rows, 1), 1.0 if n_valid <= n_sel else 0.0, F32)
    t = _select_threshold(keys_scr, width // LANES, LANES, n_sel, few)
    m_ref[...] = jnp.where(keys_scr[...] >= t, 1.0, 0.0)


def _mask_sample(sc, n_valid, n_sel):
    n, width = sc.shape
    return pl.pallas_call(
        functools.partial(_mask_sample_kernel, n_valid=n_valid, n_sel=n_sel),
        out_shape=jax.ShapeDtypeStruct((n, width), F32),
        scratch_shapes=[pltpu.VMEM((n, width), I32)],
        name="mask_sample",
    )(sc)


def _attn_sample_kernel(pt_ref, q_ref, m4_ref, mn_ref, kn_ref, vn_ref, *refs, npg):
    kpages, vpages, o_ref = refs[:npg], refs[npg:2 * npg], refs[2 * npg]
    q = q_ref[...]
    kp = jnp.concatenate([p[...] for p in kpages], axis=0).astype(BF16)
    s = lax.dot_general(q, kp, (((1,), (1,)), ((), ())), preferred_element_type=F32)
    grp = lax.broadcasted_iota(I32, s.shape, 0) // Q_PER_KV
    own = (lax.broadcasted_iota(I32, s.shape, 1) & (N_KV - 1)) == grp
    s = jnp.where(own, jnp.where(m4_ref[...] > 0.5, s, NEG_BIG), NEG_BIG)
    grp_h = lax.broadcasted_iota(I32, (N_HEADS, HEAD_DIM), 0) // Q_PER_KV
    kn = kn_ref[...].astype(F32)
    vn = vn_ref[...].astype(F32)
    kn_h = jnp.zeros((N_HEADS, HEAD_DIM), F32)
    vn_h = jnp.zeros((N_HEADS, HEAD_DIM), F32)
    for g in range(N_KV):
        kn_h = jnp.where(grp_h == g, kn[g:g + 1, :], kn_h)
        vn_h = jnp.where(grp_h == g, vn[g:g + 1, :], vn_h)
    sn = jnp.sum(q.astype(F32) * kn_h, axis=1, keepdims=True)
    sn = jnp.where(mn_ref[:, 0:1] > 0.5, sn, NEG_BIG)
    m = jnp.maximum(jnp.max(s, axis=1, keepdims=True), sn)
    p = jnp.exp2(s - m)
    pn = jnp.exp2(sn - m)
    l = jnp.sum(p, axis=1, keepdims=True) + pn
    vp = jnp.concatenate([r[...] for r in vpages], axis=0).astype(BF16)
    o = jnp.dot(p.astype(BF16), vp, preferred_element_type=F32)
    o_ref[...] = ((o + pn.astype(BF16).astype(F32) * vn_h) / l).astype(o_ref.dtype)


def _attn_sample(page_table, q3, mask4, mask_new, kn3, vn3, cache_k, cache_v):
    n, npg = page_table.shape
    prow = cache_k.shape[1]
    pspec = lambda p: pl.BlockSpec((None, prow, HEAD_DIM), lambda b, pt, p=p: (pt[b * npg + p], 0, 0))
    return pl.pallas_call(
        functools.partial(_attn_sample_kernel, npg=npg),
        out_shape=jax.ShapeDtypeStruct((n, N_HEADS, HEAD_DIM), BF16),
        grid_spec=pltpu.PrefetchScalarGridSpec(
            num_scalar_prefetch=1, grid=(n,),
            in_specs=[pl.BlockSpec((None, N_HEADS, HEAD_DIM), lambda b, pt: (b, 0, 0)),
                      pl.BlockSpec((None, 1, npg * prow), lambda b, pt: (b, 0, 0)),
                      pl.BlockSpec((None, 1, LANES), lambda b, pt: (b, 0, 0)),
                      pl.BlockSpec((None, N_KV, HEAD_DIM), lambda b, pt: (b, 0, 0)),
                      pl.BlockSpec((None, N_KV, HEAD_DIM), lambda b, pt: (b, 0, 0))]
            + [pspec(p) for p in range(npg)] + [pspec(p) for p in range(npg)],
            out_specs=pl.BlockSpec((None, N_HEADS, HEAD_DIM), lambda b, pt: (b, 0, 0))),
        compiler_params=_cparams(("parallel",)),
        name="attn_sample",
    )(page_table.reshape(-1), q3, mask4, mask_new, kn3, vn3, *([cache_k] * npg), *([cache_v] * npg))


MOE_RB = 256
MOE_GROUP = 2
MOE_RC = 1280
MOE_TF = 512
MOE_TN = 512


def _moe_sizes(n_tok):
    n_assign = n_tok * TOP_K
    n_slots = (n_assign // MOE_RB + N_EXPERTS) * MOE_RB
    n_units = N_EXPERTS + n_assign // MOE_RC
    return n_slots, n_units


def _moe_rank_kernel(sel_ref, rank_ref, cnt_ref, carry):
    i = pl.program_id(0)
    tp = sel_ref.shape[0]

    @pl.when(i == 0)
    def _():
        carry[...] = jnp.zeros_like(carry)

    a = sel_ref[...]
    r = lax.broadcasted_iota(I32, (tp, tp), 0)
    c = lax.broadcasted_iota(I32, (tp, tp), 1)
    lower = jnp.where(c < r, 1.0, 0.0).astype(BF16)
    rank_ref[...] = jnp.dot(lower, a.astype(BF16), preferred_element_type=F32) + carry[0:1, :]
    carry[...] = carry[...] + jnp.sum(a, axis=0, keepdims=True)
    cnt_ref[...] = carry[...]


def _moe_rank(sel):
    t = sel.shape[0]
    tp = _pick(t, (256, 128))
    return pl.pallas_call(
        _moe_rank_kernel,
        out_shape=[jax.ShapeDtypeStruct((t, LANES), F32), jax.ShapeDtypeStruct((8, LANES), F32)],
        grid=(t // tp,),
        in_specs=[pl.BlockSpec((tp, LANES), lambda i: (i, 0))],
        out_specs=[pl.BlockSpec((tp, LANES), lambda i: (i, 0)),
                   pl.BlockSpec((8, LANES), lambda i: (0, 0))],
        scratch_shapes=[pltpu.VMEM((8, LANES), F32)],
        compiler_params=_cparams(("arbitrary",)),
        name="moe_rank",
    )(sel)


def _moe_dest_kernel(sel_ref, gate_ref, rank_ref, cnt_ref, dest_ref, g4_ref, tab_ref):
    cnt = cnt_ref[...]
    lane8 = lax.broadcasted_iota(I32, cnt.shape, 1)
    padded = jnp.ceil(cnt * (1.0 / MOE_RB)) * MOE_RB
    incl = padded
    for d in (1, 2, 4, 8, 16, 32, 64):
        incl = incl + jnp.where(lane8 >= d, pltpu.roll(incl, d, 1), 0.0)
    start = incl - padded
    row8 = lax.broadcasted_iota(I32, cnt.shape, 0)
    tab_ref[...] = jnp.where(row8 == 0, start, jnp.where(row8 == 1, padded, 0.0)).astype(I32)
    dest = start[0:1, :] + rank_ref[...]
    gate = gate_ref[...]
    cur = sel_ref[...]
    lane = lax.broadcasted_iota(I32, cur.shape, 1)
    d4 = jnp.zeros(cur.shape, F32)
    g4 = jnp.zeros(cur.shape, F32)
    for k in range(TOP_K):
        first = jnp.min(jnp.where(cur > 0.5, lane, LANES), axis=1, keepdims=True)
        pick = lane == first
        dk = jnp.sum(jnp.where(pick, dest, 0.0), axis=1, keepdims=True)
        gk = jnp.sum(jnp.where(pick, gate, 0.0), axis=1, keepdims=True)
        d4 = jnp.where(lane == k, dk, d4)
        g4 = jnp.where(lane == k, gk, g4)
        cur = jnp.where(pick, 0.0, cur)
    dest_ref[...] = d4.astype(I32)
    g4_ref[...] = g4


def _moe_dest(sel, gate, rank, cnt):
    t = sel.shape[0]
    tp = _pick(t, (256, 128))
    row = lambda: pl.BlockSpec((tp, LANES), lambda i: (i, 0))
    one = lambda: pl.BlockSpec((8, LANES), lambda i: (0, 0))
    return pl.pallas_call(
        _moe_dest_kernel,
        out_shape=[jax.ShapeDtypeStruct((t, LANES), I32), jax.ShapeDtypeStruct((t, LANES), F32),
                   jax.ShapeDtypeStruct((8, LANES), I32)],
        grid=(t // tp,),
        in_specs=[row(), row(), row(), one()],
        out_specs=[row(), row(), one()],
        compiler_params=_cparams(("arbitrary",)),
        name="moe_dest",
    )(sel, gate, rank, cnt)


def _moe_units_kernel(tab_ref, unit_ref, *, n_units):
    rcb = MOE_RC // MOE_RB

    def per_expert(e, state):
        u0, _ = state
        nb = tab_ref[1, e] // MOE_RB
        b0 = tab_ref[0, e] // MOE_RB
        nu = (nb + rcb - 1) // rcb

        def per_unit(j, carry):
            unit_ref[0, u0 + j] = e
            unit_ref[1, u0 + j] = b0 + j * rcb
            unit_ref[2, u0 + j] = jnp.minimum(nb - j * rcb, rcb)
            return carry

        lax.fori_loop(0, nu, per_unit, 0)
        return u0 + nu, jnp.where(nu > 0, e, state[1])

    used, last = lax.fori_loop(0, N_EXPERTS, per_expert, (jnp.int32(0), jnp.int32(0)))

    def fill(u, carry):
        unit_ref[0, u] = last
        unit_ref[1, u] = 0
        unit_ref[2, u] = 0
        return carry

    lax.fori_loop(used, n_units, fill, 0)


def _moe_units(tab, n_units):
    return pl.pallas_call(
        functools.partial(_moe_units_kernel, n_units=n_units),
        out_shape=jax.ShapeDtypeStruct((3, n_units), I32),
        in_specs=[pl.BlockSpec(memory_space=pltpu.SMEM)],
        out_specs=pl.BlockSpec(memory_space=pltpu.SMEM),
        name="moe_units",
    )(tab)


def _moe_scatter_kernel(dest_ref, x_ref, xs_ref, sem):
    tm = x_ref.shape[0]

    def issue(r, carry):
        for k in range(TOP_K):
            pltpu.make_async_copy(x_ref.at[pl.ds(r, 1)], xs_ref.at[pl.ds(dest_ref[r * TOP_K + k], 1)],
                                  sem).start()
        return carry

    lax.fori_loop(0, tm, issue, 0)
    for _ in range(TOP_K):
        pltpu.make_async_copy(x_ref, xs_ref.at[pl.ds(0, tm)], sem).wait()


def _moe_scatter(dest_flat, x1, n_slots):
    t, d = x1.shape
    tm = _pick(t, (256, 128))
    return pl.pallas_call(
        _moe_scatter_kernel,
        out_shape=jax.ShapeDtypeStruct((n_slots, d), F32),
        grid=(t // tm,),
        in_specs=[pl.BlockSpec((tm * TOP_K,), lambda i: (i,), memory_space=pltpu.SMEM),
                  pl.BlockSpec((tm, d), lambda i: (i, 0))],
        out_specs=pl.BlockSpec(memory_space=pl.ANY),
        scratch_shapes=[pltpu.SemaphoreType.DMA(())],
        compiler_params=_cparams(("arbitrary",)),
        name="moe_scatter",
    )(dest_flat, x1)


def _moe_expert_kernel(unit_ref, xs_ref, wg_ref, wl_ref, bg_ref, bl_ref, wd_ref, bd_ref, ys_ref,
                       x_scr, act_scr, xst, yst1, yst2, ystg, wg_bf, wl_bf, wd_bf,
                       xsem, ysem1, ysem2, ysemg, *, ju):
    u = pl.program_id(0)
    j = pl.program_id(1)
    b0 = unit_ref[1, u]
    ns = unit_ref[2, u]
    rb = MOE_RB

    def x_copy(s, slot):
        return pltpu.make_async_copy(xs_ref.at[pl.ds((b0 + s) * rb, rb)], xst.at[slot], xsem.at[slot])

    def rows_of(s, n=1):
        return pl.ds(s * rb, n * rb) if isinstance(s, int) else pl.ds(pl.multiple_of(s * rb, rb), n * rb)

    def up_block(s, n=1):
        xb = x_scr[rows_of(s, n), :]
        hg = jnp.dot(xb, wg_bf[...], preferred_element_type=F32) + bg_ref[...]
        hl = jnp.dot(xb, wl_bf[...], preferred_element_type=F32) + bl_ref[...]
        glu = jnp.minimum(hg, SWIGLU_LIMIT)
        lin = jnp.clip(hl, -SWIGLU_LIMIT, SWIGLU_LIMIT)
        act = glu * _sigmoid(SWIGLU_ALPHA * glu) * (lin + 1.0)
        act_scr[rows_of(s, n), pl.ds(pl.multiple_of(j * MOE_TF, MOE_TF), MOE_TF)] = act.astype(BF16)

    n_grp = (ns - 1) // MOE_GROUP
    rem = (ns - 1) % MOE_GROUP
    has2 = rem >= 2
    has1 = rem % 2 == 1
    tail2_at = 1 + MOE_GROUP * n_grp
    tail1_at = tail2_at + jnp.where(has2, 2, 0)

    def cast_up_weights():
        wg_bf[...] = wg_ref[...].astype(BF16)
        wl_bf[...] = wl_ref[...].astype(BF16)

    @pl.when(jnp.logical_and(ns > 0, j == 0))
    def _():
        x_copy(0, 0).start()
        cast_up_weights()

        @pl.when(ns > 1)
        def _():
            x_copy(1, 1).start()

        x_copy(0, 0).wait()
        x_scr[rows_of(0), :] = xst[0].astype(BF16)
        up_block(0)

        def body(s, carry):
            slot = s % 2

            @pl.when(s + 1 < ns)
            def _():
                x_copy(s + 1, 1 - slot).start()

            x_copy(s, slot).wait()
            x_scr[rows_of(s), :] = xst[slot].astype(BF16)
            up_block(s)
            return carry

        lax.fori_loop(1, ns, body, 0)

    @pl.when(jnp.logical_and(ns > 0, jnp.logical_and(j > 0, j < ju)))
    def _():
        cast_up_weights()
        up_block(0)

        def body(p, carry):
            up_block(1 + MOE_GROUP * p, MOE_GROUP)
            return carry

        lax.fori_loop(0, n_grp, body, 0)

        @pl.when(has2)
        def _():
            up_block(tail2_at, 2)

        @pl.when(has1)
        def _():
            up_block(tail1_at)

    @pl.when(jnp.logical_and(ns > 0, j >= ju))
    def _():
        col = pl.multiple_of((j - ju) * MOE_TN, MOE_TN)

        def y_copy(stage, sem, s, n):
            return pltpu.make_async_copy(
                stage, ys_ref.at[pl.ds((b0 + s) * rb, n * rb), pl.ds(col, MOE_TN)], sem)

        def yg_copy(p, slot):
            return y_copy(ystg.at[slot], ysemg.at[slot], 1 + MOE_GROUP * p, MOE_GROUP)

        def down(s, n=1):
            return jnp.dot(act_scr[rows_of(s, n), :], wd_bf[...], preferred_element_type=F32) + bd_ref[...]

        wd_bf[...] = wd_ref[...].astype(BF16)
        yst1[0] = down(0)
        y_copy(yst1.at[0], ysem1.at[0], 0, 1).start()

        def body(p, carry):
            slot = p % 2

            @pl.when(p >= 2)
            def _():
                yg_copy(p - 2, slot).wait()

            ystg[slot] = down(1 + MOE_GROUP * p, MOE_GROUP)
            yg_copy(p, slot).start()
            return carry

        lax.fori_loop(0, n_grp, body, 0)

        @pl.when(has2)
        def _():
            yst2[...] = down(tail2_at, 2)
            y_copy(yst2, ysem2.at[0], tail2_at, 2).start()

        @pl.when(has1)
        def _():
            yst1[1] = down(tail1_at)
            y_copy(yst1.at[1], ysem1.at[1], tail1_at, 1).start()

        @pl.when(n_grp >= 2)
        def _():
            yg_copy(n_grp - 2, n_grp % 2).wait()

        @pl.when(n_grp >= 1)
        def _():
            yg_copy(n_grp - 1, (n_grp - 1) % 2).wait()

        y_copy(yst1.at[0], ysem1.at[0], 0, 1).wait()

        @pl.when(has2)
        def _():
            y_copy(yst2, ysem2.at[0], tail2_at, 2).wait()

        @pl.when(has1)
        def _():
            y_copy(yst1.at[1], ysem1.at[1], tail1_at, 1).wait()


def _moe_experts(units, xs, w_up, b_up, w_down, b_down):
    n_slots, d = xs.shape
    n_units = units.shape[1]
    dff = w_down.shape[1]
    ju, jd = dff // MOE_TF, d // MOE_TN

    def up_idx(off):
        return lambda u, j, un: (un[0, u], 0, off + jnp.where(un[2, u] > 0, jnp.minimum(j, ju - 1), ju - 1))

    def dn_idx(u, j, un):
        return (un[0, u], 0, jnp.where(un[2, u] > 0, jnp.maximum(j - ju, 0), jd - 1))

    return pl.pallas_call(
        functools.partial(_moe_expert_kernel, ju=ju),
        out_shape=jax.ShapeDtypeStruct((n_slots, d), F32),
        grid_spec=pltpu.PrefetchScalarGridSpec(
            num_scalar_prefetch=1, grid=(n_units, ju + jd),
            in_specs=[pl.BlockSpec(memory_space=pl.ANY),
                      pl.BlockSpec((None, d, MOE_TF), up_idx(0)),
                      pl.BlockSpec((None, d, MOE_TF), up_idx(ju)),
                      pl.BlockSpec((None, 1, MOE_TF), up_idx(0)),
                      pl.BlockSpec((None, 1, MOE_TF), up_idx(ju)),
                      pl.BlockSpec((None, dff, MOE_TN), dn_idx),
                      pl.BlockSpec((None, 1, MOE_TN), dn_idx)],
            out_specs=pl.BlockSpec(memory_space=pl.ANY),
            scratch_shapes=[pltpu.VMEM((MOE_RC, d), BF16), pltpu.VMEM((MOE_RC, dff), BF16),
                            pltpu.VMEM((2, MOE_RB, d), F32), pltpu.VMEM((2, MOE_RB, MOE_TN), F32),
                            pltpu.VMEM((2 * MOE_RB, MOE_TN), F32),
                            pltpu.VMEM((2, MOE_GROUP * MOE_RB, MOE_TN), F32),
                            pltpu.VMEM((d, MOE_TF), BF16), pltpu.VMEM((d, MOE_TF), BF16),
                            pltpu.VMEM((dff, MOE_TN), BF16),
                            pltpu.SemaphoreType.DMA((2,)), pltpu.SemaphoreType.DMA((2,)),
                            pltpu.SemaphoreType.DMA((1,)), pltpu.SemaphoreType.DMA((2,))]),
        compiler_params=_cparams(("arbitrary", "arbitrary")),
        name="moe_experts",
    )(units, xs, w_up, w_up, b_up, b_up, w_down, b_down)


COMBINE_SUB_ROWS = 128


def _moe_combine_kernel(dest_ref, g4_ref, x1_ref, ys_ref, p_ref, wg_ref, wp_ref, g_ref, b_ref,
                        y_ref, gbuf, sem, *, alpha):
    tm = x1_ref.shape[0]

    def issue(r, carry):
        for k in range(TOP_K):
            pltpu.make_async_copy(ys_ref.at[pl.ds(dest_ref[r * TOP_K + k], 1)],
                                  gbuf.at[k, pl.ds(r, 1)], sem).start()
        return carry

    lax.fori_loop(0, tm, issue, 0)
    for k in range(TOP_K):
        pltpu.make_async_copy(ys_ref.at[pl.ds(0, tm)], gbuf.at[k], sem).wait()
    sub = min(tm, COMBINE_SUB_ROWS)
    for r in range(0, tm, sub):
        rows = slice(r, r + sub)
        g4 = g4_ref[rows, :]
        moe = g4[:, 0:1] * gbuf[0, rows, :]
        for k in range(1, TOP_K):
            moe = moe + g4[:, k:k + 1] * gbuf[k, rows, :]
        x2 = _layer_norm(alpha * x1_ref[rows, :] + moe, g_ref[...], b_ref[...])
        gate = _sigmoid(jnp.dot(x2.astype(BF16), wg_ref[...], preferred_element_type=F32))
        y_ref[rows, :] = x2 + gate * jnp.dot(p_ref[rows, :].astype(BF16), wp_ref[...],
                                             preferred_element_type=F32)


def _moe_combine(dest_flat, g4, x1, ys, p, w_gate, w_proj, g, b, alpha, row0, t):
    d = x1.shape[1]
    tm = _pick(t, (2 * COMBINE_SUB_ROWS, COMBINE_SUB_ROWS))
    assert row0 % tm == 0
    r0 = row0 // tm
    pd = p.shape[1]
    return pl.pallas_call(
        functools.partial(_moe_combine_kernel, alpha=alpha),
        out_shape=jax.ShapeDtypeStruct((t, d), F32),
        grid=(t // tm,),
        in_specs=[pl.BlockSpec((tm * TOP_K,), lambda i: (r0 + i,), memory_space=pltpu.SMEM),
                  pl.BlockSpec((tm, LANES), lambda i: (r0 + i, 0)),
                  pl.BlockSpec((tm, d), lambda i: (r0 + i, 0)),
                  pl.BlockSpec(memory_space=pl.ANY),
                  pl.BlockSpec((tm, pd), lambda i: (i, 0)),
                  _const_spec((d, d)), _const_spec((pd, d)), _const_spec((1, d)), _const_spec((1, d))],
        out_specs=pl.BlockSpec((tm, d), lambda i: (i, 0)),
        scratch_shapes=[pltpu.VMEM((TOP_K, tm, d), F32), pltpu.SemaphoreType.DMA(())],
        compiler_params=_cparams(("arbitrary",)),
        name="moe_combine",
    )(dest_flat, g4, x1, ys, p, w_gate, w_proj, g, b)


def _project_all(x, w, pos, prompt):
    t128, t64, tkw = _rope_tables(pos)
    xb = x.astype(BF16)
    dq = N_HEADS * HEAD_DIM
    dkv = N_KV * HEAD_DIM
    dqi = IDX_HEADS * IDX_DIM
    d = x.shape[1]
    (q,) = _proj(xb, w, 0, dq, t128, "rope128", ((BF16, None),), scale=LOG2_E * HEAD_DIM ** -0.5)
    k32, kb = _proj(xb, w, dq, dkv, t128, "rope128",
                    ((F32, None), (BF16, "transposed" if prompt else None)))
    v32, vb = _proj(xb, w, dq + dkv, dkv, (), "plain",
                    ((F32, None), (BF16, "with_ones" if prompt else None)))
    (qi,) = _proj(xb, w, dq + 2 * dkv, dqi, t64, "rope64", ((BF16, None),))
    c0 = dq + 2 * dkv + dqi
    (kw,) = _proj(xb, w, c0, LANES, tkw, "rope64", ((F32, None),))
    (zr,) = _proj(xb, w, c0, 4 * d, (), "plain", ((F32, None),), shift=IDX_DIM + IDX_HEADS)
    return q, k32, kb, v32, vb, qi, zr, kw


def kernel(x_prompt, x_sample, cache_k, cache_v, cache_kidx, state_h, state_conv, page_table, p_prompt, p_sample, w_in, conv_w, conv_b, rg_wa, rg_ba, rg_wx, rg_bx, rg_lambda, w_o_rnn, w_o_att, w_out, ln1_g, ln1_b, w_router, b_router, w_up, b_up, w_down, b_down, ln2_g, ln2_b, w_ple_gate, w_ple_proj):
    bsz, s, d = x_prompt.shape
    n, dec_t, _ = x_sample.shape
    assert bsz == 1 and dec_t == 1 and s % LANES == 0 and n % LANES == 0
    depth = w_in.shape[0]
    npg = page_table.shape[1]
    n_phys, page = cache_k.shape[1], cache_k.shape[2]
    past = npg * page
    alpha = (2 * depth) ** 0.25
    dq, dkv, dqi = N_HEADS * HEAD_DIM, N_KV * HEAD_DIM, IDX_HEADS * IDX_DIM
    t_all = s + n
    n_slots, n_units = _moe_sizes(t_all)
    pos_p = jnp.arange(s, dtype=I32)
    pos_s = jnp.full((n,), past, I32)
    row = lambda a: a[None, :]

    hp, hs = x_prompt[0], x_sample[:, 0]
    st_p = [[], [], [], [], []]
    st_s = [[], [], [], [], []]
    for i in range(depth):
        wa, wx = rg_wa[i].astype(BF16), rg_wx[i].astype(BF16)
        rnn_w = (conv_w[i], row(conv_b[i]), wa, wx, row(rg_ba[i]), row(rg_bx[i]), row(rg_lambda[i]))
        wor, woa, wo = w_o_rnn[i].astype(BF16), w_o_att[i].astype(BF16), w_out[i].astype(BF16)
        wr = jnp.pad(w_router[i], ((0, 0), (0, LANES - N_EXPERTS)))
        wr_hi = wr.astype(BF16)
        wr = jnp.concatenate([wr_hi, (wr - wr_hi.astype(F32)).astype(BF16)], axis=1)
        br = jnp.pad(b_router[i], (0, LANES - N_EXPERTS))[None, :]
        ln1 = (row(ln1_g[i]), row(ln1_b[i]))

        w_in_t = jnp.swapaxes(w_in[i], 0, 1)
        q, k32, kt, v32, vx, qi, zr, kw = _project_all(hp, w_in_t, pos_p, True)
        y_rnn, h_p = _rnn_prompt(zr, s, *rnn_w)
        kit = kw[:, :IDX_DIM].T.astype(BF16)
        y_att = _attn_prompt(qi, kw, q, jnp.concatenate([kit, kit], 0), kt, vx, s,
                             min(TOPK_MAX, s // 4))
        m = _merge(y_rnn, y_att, wor, woa, zr)
        bufs = _ln1_router(m, hp, wo, *ln1, wr, br, alpha, 0, t_all)
        st = (k32.reshape(1, s, N_KV, HEAD_DIM), v32.reshape(1, s, N_KV, HEAD_DIM),
              kw[None, :, :IDX_DIM], h_p, zr[None, s - (CONV_W - 1):s, :d])
        for lst, a in zip(st_p, st):
            lst.append(a)

        q, k32, kb, v32, vb, qi, zr, kw = _project_all(hs, w_in_t, pos_s, False)
        y_rnn, h_s = _rnn_sample(zr, state_conv[i], state_h[i], *rnn_w)
        sc = _idx_sample(page_table, qi.reshape(n, IDX_HEADS, IDX_DIM),
                         kw[:, IDX_DIM:IDX_DIM + IDX_HEADS, None], kw[:, None, :],
                         jnp.swapaxes(cache_kidx[i], 1, 2))
        mask = _mask_sample(sc[:, 0], past + 1, min(TOPK_MAX, (past + 1) // 4))
        y_att = _attn_sample(page_table, q.reshape(n, N_HEADS, HEAD_DIM),
                             jnp.repeat(mask[:, :past], N_KV, axis=1)[:, None, :],
                             mask[:, None, past:past + LANES],
                             kb.reshape(n, N_KV, HEAD_DIM), vb.reshape(n, N_KV, HEAD_DIM),
                             cache_k[i].reshape(n_phys, page * N_KV, HEAD_DIM),
                             cache_v[i].reshape(n_phys, page * N_KV, HEAD_DIM)).reshape(n, dq)
        m = _merge(y_rnn, y_att, wor, woa, zr)
        x1, sel, gate = _ln1_router(m, hs, wo, *ln1, wr, br, alpha, s, t_all, prev=bufs)
        st = (k32.reshape(n, 1, N_KV, HEAD_DIM), v32.reshape(n, 1, N_KV, HEAD_DIM),
              kw[:, None, :IDX_DIM], h_s,
              jnp.concatenate([state_conv[i][:, 1:], zr[:, None, :d]], axis=1))
        for lst, a in zip(st_s, st):
            lst.append(a)

        rank, cnt = _moe_rank(sel)
        dest4, g4, tab = _moe_dest(sel, gate, rank, cnt)
        units = _moe_units(tab, n_units)
        dest_flat = dest4[:, :TOP_K].reshape(-1)
        xs = _moe_scatter(dest_flat, x1, n_slots)
        ys = _moe_experts(units, xs, w_up[i], b_up[i][:, None, :], w_down[i], b_down[i][:, None, :])
        tail = (w_ple_gate[i].astype(BF16), w_ple_proj[i].astype(BF16), row(ln2_g[i]), row(ln2_b[i]),
                alpha)
        hp = _moe_combine(dest_flat, g4, x1, ys, p_prompt[i][0], *tail, 0, s)
        hs = _moe_combine(dest_flat, g4, x1, ys, p_sample[i][:, 0], *tail, s, n)

    outs_p = [jnp.stack(l) for l in st_p]
    outs_s = [jnp.stack(l) for l in st_s]
    return (hp[None], hs[:, None], *outs_p, *outs_s)
```

```python
import functools

import jax
import jax.numpy as jnp
from jax import lax
from jax.experimental import pallas as pl
from jax.experimental.pallas import tpu as pltpu

F32 = jnp.float32
BF16 = jnp.bfloat16
I32 = jnp.int32

N_HEADS = 16
HEAD_DIM = 128
N_KV = 4
Q_PER_KV = N_HEADS // N_KV
IDX_HEADS = 16
IDX_DIM = 64
TOPK_MAX = 256
Q_BLOCK = 128
ROPE_THETA = 10000.0
RNN_BW = 128
CONV_W = 4
RG_C = 8.0
N_EXPERTS = 32
TOP_K = 4
SWIGLU_LIMIT = 7.0
SWIGLU_ALPHA = 1.702
LN_EPS = 1e-5
LANES = 128
INT_MIN = -(2 ** 31)
NEG_BIG = -1e30
LOG2_E = 1.4426950408889634
VMEM_LIMIT = 56 * 1024 * 1024


def _pick(n, cands):
    for c in cands:
        if n % c == 0:
            return c
    return n


def _cparams(sem):
    return pltpu.CompilerParams(dimension_semantics=sem, vmem_limit_bytes=VMEM_LIMIT)


def _const_spec(shape):
    nd = len(shape)
    return pl.BlockSpec(shape, lambda *a: (0,) * nd, pipeline_mode=pl.Buffered(1))


def _proj_kernel(*refs, mode, kinds, shift, n_tab, scale):
    n_w = 2 if shift else 1
    x_ref, w_refs = refs[0], refs[1:1 + n_w]
    tabs = refs[1 + n_w:1 + n_w + n_tab]
    outs = refs[1 + n_w + n_tab:1 + n_w + n_tab + len(kinds)]
    w_bf = refs[-1]
    tn = w_bf.shape[1]

    @pl.when(pl.program_id(1) == 0)
    def _():
        for c in range(0, tn, LANES):
            lo = c + shift
            if lo + LANES <= tn:
                rows = w_refs[0][lo:lo + LANES, :]
            elif lo >= tn:
                rows = w_refs[1][lo - tn:lo - tn + LANES, :]
            else:
                rows = jnp.concatenate([w_refs[0][lo:tn, :], w_refs[1][0:lo + LANES - tn, :]], axis=0)
            w_bf[:, c:c + LANES] = rows.T.astype(BF16)

    z = jnp.dot(x_ref[...], w_bf[...], preferred_element_type=F32)
    for h in range(tn // LANES):
        zh = z[:, h * LANES:(h + 1) * LANES]
        if mode == "plain":
            r = zh
        elif mode == "rope128":
            r = zh * tabs[0][...] + pltpu.roll(zh, 64, 1) * tabs[1][...]
        else:
            r = (zh * tabs[0][...] + pltpu.roll(zh, 96, 1) * tabs[1][...]
                 + pltpu.roll(zh, 32, 1) * tabs[2][...])
        if scale != 1.0:
            r = r * scale
        for o, kind in zip(outs, kinds):
            if kind == "transposed":
                o[h * LANES:(h + 1) * LANES, :] = r.T.astype(o.dtype)
            elif kind == "with_ones":
                lane = lax.broadcasted_iota(I32, r.shape, 1)
                o[:, 2 * h * LANES:(2 * h + 1) * LANES] = r.astype(o.dtype)
                o[:, (2 * h + 1) * LANES:(2 * h + 2) * LANES] = jnp.where(lane == 0, 1.0, 0.0).astype(o.dtype)
            else:
                o[:, h * LANES:(h + 1) * LANES] = r.astype(o.dtype)


def _proj(x, wt, col0, ncols, tabs, mode, outs, scale=1.0, shift=0):
    t, k = x.shape
    tm = _pick(t, (1024, 512, 256, 128))
    tn = _pick(ncols, (1024, 512, 256, 128))
    assert col0 % tn == 0 and shift % 8 == 0 and shift < LANES
    c0 = col0 // tn
    in_specs = [pl.BlockSpec((tm, k), lambda j, i: (i, 0)),
                pl.BlockSpec((tn, k), lambda j, i: (c0 + j, 0))]
    ws = [wt]
    if shift:
        in_specs.append(pl.BlockSpec((LANES, k), lambda j, i: ((c0 + j + 1) * (tn // LANES), 0)))
        ws.append(wt)
    in_specs += [pl.BlockSpec((tm, LANES), lambda j, i: (i, 0)) for _ in tabs]
    shapes, specs = [], []
    for dt, kind in outs:
        if kind == "transposed":
            shapes.append(jax.ShapeDtypeStruct((ncols, t), dt))
            specs.append(pl.BlockSpec((tn, tm), lambda j, i: (j, i)))
        elif kind == "with_ones":
            shapes.append(jax.ShapeDtypeStruct((t, 2 * ncols), dt))
            specs.append(pl.BlockSpec((tm, 2 * tn), lambda j, i: (i, j)))
        else:
            shapes.append(jax.ShapeDtypeStruct((t, ncols), dt))
            specs.append(pl.BlockSpec((tm, tn), lambda j, i: (i, j)))
    return pl.pallas_call(
        functools.partial(_proj_kernel, mode=mode, kinds=tuple(kd for _, kd in outs), shift=shift,
                          n_tab=len(tabs), scale=scale),
        out_shape=shapes,
        grid=(ncols // tn, t // tm),
        in_specs=in_specs,
        out_specs=specs,
        scratch_shapes=[pltpu.VMEM((k, tn), BF16)],
        compiler_params=_cparams(("parallel", "arbitrary")),
        name="proj_" + mode,
    )(x, *ws, *tabs)


def _rope_tables(pos):
    posf = pos.astype(F32)[:, None]
    h128 = HEAD_DIM // 2
    inv = ROPE_THETA ** (-jnp.arange(h128, dtype=F32) / h128)
    c, s = jnp.cos(posf * inv), jnp.sin(posf * inv)
    t128 = (jnp.concatenate([c, c], 1), jnp.concatenate([-s, s], 1))
    h64 = IDX_DIM // 2
    inv = ROPE_THETA ** (-jnp.arange(h64, dtype=F32) / h64)
    c, s = jnp.cos(posf * inv), jnp.sin(posf * inv)
    z = jnp.zeros_like(s)
    c64 = jnp.concatenate([c, c], 1)
    sa64 = jnp.concatenate([-s, z], 1)
    sb64 = jnp.concatenate([z, s], 1)
    t64 = tuple(jnp.concatenate([a, a], 1) for a in (c64, sa64, sb64))
    idx_scale = (IDX_HEADS * IDX_DIM) ** -0.5
    n = pos.shape[0]
    ck = jnp.concatenate([c64, jnp.full((n, IDX_HEADS), idx_scale, F32),
                          jnp.zeros((n, LANES - IDX_DIM - IDX_HEADS), F32)], 1)
    z64 = jnp.zeros((n, LANES - IDX_DIM), F32)
    tkw = (ck, jnp.concatenate([sa64, z64], 1), jnp.concatenate([sb64, z64], 1))
    return t128, t64, tkw


def _sigmoid(x):
    return 0.5 * jnp.tanh(0.5 * x) + 0.5


def _softplus_neg(lam):
    return jnp.maximum(-lam, 0.0) + jnp.log1p(jnp.exp(-jnp.abs(lam)))


def _gelu_tanh(x):
    return 0.5 * x * (1.0 + jnp.tanh(0.7978845608028654 * (x + 0.044715 * (x * x * x))))


def _rglru_gates(xc, wa_ref, wx_ref, ba, bx, lam):
    nblk = xc.shape[1] // RNN_BW
    rs, gs = [], []
    for j in range(nblk):
        xj = xc[:, j * RNN_BW:(j + 1) * RNN_BW].astype(BF16)
        rs.append(jnp.dot(xj, wa_ref[j], preferred_element_type=F32))
        gs.append(jnp.dot(xj, wx_ref[j], preferred_element_type=F32))
    r = _sigmoid(jnp.concatenate(rs, 1) + ba)
    g = _sigmoid(jnp.concatenate(gs, 1) + bx)
    log_a = -RG_C * r * _softplus_neg(lam)
    a = jnp.exp(log_a)
    mult = jnp.sqrt(jnp.tanh(-log_a) * (1.0 + a * a))
    return a, mult, g * xc


def _rnn_prompt_kernel(xr_ref, gr_ref, cw_ref, cb_ref, wa_ref, wx_ref, ba_ref, bx_ref, lam_ref,
                       y_ref, hl_ref, xbuf, hcar, a_scr, b_scr, h_scr):
    t = pl.program_id(1)
    tm, cw = xr_ref.shape

    @pl.when(t == 0)
    def _():
        xbuf[0:8, :] = jnp.zeros((8, cw), F32)
        hcar[...] = jnp.zeros_like(hcar)

    x = xr_ref[...]
    xbuf[8:8 + tm, :] = x
    w = cw_ref[...]
    xc = (cb_ref[...] + w[3:4] * x + w[2:3] * xbuf[7:7 + tm, :]
          + w[1:2] * xbuf[6:6 + tm, :] + w[0:1] * xbuf[5:5 + tm, :])
    xbuf[0:8, :] = x[tm - 8:tm, :]
    a, mult, gx = _rglru_gates(xc, wa_ref, wx_ref, ba_ref[...], bx_ref[...], lam_ref[...])
    pos = t * tm + lax.broadcasted_iota(I32, (tm, cw), 0)
    mult = jnp.where(pos == 0, 1.0, mult)
    a_scr[...] = a
    b_scr[...] = mult * gx
    row8 = lax.broadcasted_iota(I32, (8, cw), 0)

    def group(g, carry):
        r0 = pl.multiple_of(g * 8, 8)
        av = a_scr[pl.ds(r0, 8), :]
        bv = b_scr[pl.ds(r0, 8), :]
        for d in (1, 2, 4):
            a_s = pltpu.roll(av, d, 0)
            b_s = pltpu.roll(bv, d, 0)
            m = row8 >= d
            bv = jnp.where(m, av * b_s + bv, bv)
            av = jnp.where(m, av * a_s, av)
        h = av * carry + bv
        h_scr[pl.ds(r0, 8), :] = h
        return h[7:8, :]

    carry = lax.fori_loop(0, tm // 8, group, hcar[0:1, :])
    hcar[0:1, :] = carry
    y_ref[...] = (_gelu_tanh(gr_ref[...]) * h_scr[...]).astype(y_ref.dtype)

    @pl.when(t == pl.num_programs(1) - 1)
    def _():
        hl_ref[...] = carry


def _rnn_prompt(zr, s, conv_w, conv_b, wa, wx, ba, bx, lam):
    d = conv_w.shape[1]
    cw = 512
    tm = _pick(s, (256, 128))
    ncb = d // cw
    nb = cw // RNN_BW
    vec = lambda: pl.BlockSpec((1, cw), lambda c, t: (0, c))
    y, hl = pl.pallas_call(
        _rnn_prompt_kernel,
        out_shape=[jax.ShapeDtypeStruct((s, d), BF16), jax.ShapeDtypeStruct((1, d), F32)],
        grid=(ncb, s // tm),
        in_specs=[pl.BlockSpec((tm, cw), lambda c, t: (t, c)),
                  pl.BlockSpec((tm, cw), lambda c, t: (t, ncb + c)),
                  pl.BlockSpec((CONV_W, cw), lambda c, t: (0, c)),
                  vec(),
                  pl.BlockSpec((nb, RNN_BW, RNN_BW), lambda c, t: (c, 0, 0)),
                  pl.BlockSpec((nb, RNN_BW, RNN_BW), lambda c, t: (c, 0, 0)),
                  vec(), vec(), vec()],
        out_specs=[pl.BlockSpec((tm, cw), lambda c, t: (t, c)),
                   pl.BlockSpec((1, cw), lambda c, t: (0, c))],
        scratch_shapes=[pltpu.VMEM((tm + 8, cw), F32), pltpu.VMEM((8, cw), F32),
                        pltpu.VMEM((tm, cw), F32), pltpu.VMEM((tm, cw), F32),
                        pltpu.VMEM((tm, cw), F32)],
        compiler_params=_cparams(("parallel", "arbitrary")),
        name="rnn_prompt",
    )(zr, zr, conv_w, conv_b, wa, wx, ba, bx, lam)
    return y, hl


def _rnn_sample_kernel(xr_ref, gr_ref, c0_ref, c1_ref, c2_ref, h0_ref, cw_ref, cb_ref,
                       wa_ref, wx_ref, ba_ref, bx_ref, lam_ref, y_ref, h_ref):
    w = cw_ref[...]
    xc = (cb_ref[...] + w[3:4] * xr_ref[...] + w[2:3] * c2_ref[...]
          + w[1:2] * c1_ref[...] + w[0:1] * c0_ref[...])
    a, mult, gx = _rglru_gates(xc, wa_ref, wx_ref, ba_ref[...], bx_ref[...], lam_ref[...])
    h = a * h0_ref[...] + mult * gx
    h_ref[...] = h
    y_ref[...] = (_gelu_tanh(gr_ref[...]) * h).astype(y_ref.dtype)


def _rnn_sample(zr, conv_state, h0, conv_w, conv_b, wa, wx, ba, bx, lam):
    n, d = h0.shape
    cw = 512
    ncb = d // cw
    nb = cw // RNN_BW
    blk = lambda off: pl.BlockSpec((n, cw), lambda c: (0, off + c))
    vec = lambda: pl.BlockSpec((1, cw), lambda c: (0, c))
    wsp = lambda: pl.BlockSpec((nb, RNN_BW, RNN_BW), lambda c: (c, 0, 0))
    return pl.pallas_call(
        _rnn_sample_kernel,
        out_shape=[jax.ShapeDtypeStruct((n, d), BF16), jax.ShapeDtypeStruct((n, d), F32)],
        grid=(ncb,),
        in_specs=[blk(0), blk(ncb), blk(0), blk(0), blk(0), blk(0),
                  pl.BlockSpec((CONV_W, cw), lambda c: (0, c)), vec(), wsp(), wsp(),
                  vec(), vec(), vec()],
        out_specs=[blk(0), blk(0)],
        compiler_params=_cparams(("parallel",)),
        name="rnn_sample",
    )(zr, zr, conv_state[:, 0], conv_state[:, 1], conv_state[:, 2], h0,
      conv_w, conv_b, wa, wx, ba, bx, lam)


def _layer_norm(x, g, b):
    mu = jnp.mean(x, axis=-1, keepdims=True)
    xc = x - mu
    var = jnp.mean(xc * xc, axis=-1, keepdims=True)
    return xc * lax.rsqrt(var + LN_EPS) * g + b


def _merge_kernel(yr_ref, ya_ref, wr_ref, wa_ref, ga_ref, gb_ref, o_ref):
    a = jnp.dot(yr_ref[...], wr_ref[...], preferred_element_type=F32)
    b = jnp.dot(ya_ref[...], wa_ref[...], preferred_element_type=F32)
    m = _sigmoid(ga_ref[...]) * a + _sigmoid(gb_ref[...]) * b
    o_ref[...] = m.astype(o_ref.dtype)


def _merge(y_rnn, y_att, w_o_rnn, w_o_att, zr):
    t, d = y_rnn.shape
    tm = _pick(t, (512, 256, 128))
    tn = 512
    nj = d // tn
    return pl.pallas_call(
        _merge_kernel,
        out_shape=jax.ShapeDtypeStruct((t, d), BF16),
        grid=(t // tm, nj),
        in_specs=[pl.BlockSpec((tm, d), lambda i, j: (i, 0)),
                  pl.BlockSpec((tm, d), lambda i, j: (i, 0)),
                  pl.BlockSpec((d, tn), lambda i, j: (0, j)),
                  pl.BlockSpec((d, tn), lambda i, j: (0, j)),
                  pl.BlockSpec((tm, tn), lambda i, j: (i, 2 * nj + j)),
                  pl.BlockSpec((tm, tn), lambda i, j: (i, 3 * nj + j))],
        out_specs=pl.BlockSpec((tm, tn), lambda i, j: (i, j)),
        compiler_params=_cparams(("parallel", "parallel")),
        name="merge",
    )(y_rnn, y_att, w_o_rnn, w_o_att, zr, zr)


LN1_SUB_ROWS = 256


def _ln1_router_kernel(*refs, alpha, n_skip):
    m_ref, x_ref, w_ref, g_ref, b_ref, wr_ref, br_ref, x1_ref, sel_ref, gate_ref = refs[n_skip:]
    tm = x_ref.shape[0]
    sub = min(tm, LN1_SUB_ROWS)
    for r in range(0, tm, sub):
        rows = slice(r, r + sub)
        y = alpha * x_ref[rows, :] + jnp.dot(m_ref[rows, :], w_ref[...], preferred_element_type=F32)
        x1 = _layer_norm(y, g_ref[...], b_ref[...])
        x1_ref[rows, :] = x1
        x_hi = x1.astype(BF16)
        x_lo = (x1 - x_hi.astype(F32)).astype(BF16)
        part = (jnp.dot(x_hi, wr_ref[...], preferred_element_type=F32)
                + jnp.dot(x_lo, wr_ref[...], preferred_element_type=F32))
        logits = part[:, :LANES] + part[:, LANES:] + br_ref[...]
        lane = lax.broadcasted_iota(I32, logits.shape, 1)
        live = lane < N_EXPERTS
        cur = jnp.where(live, logits, -jnp.inf)
        top = jnp.max(cur, axis=1, keepdims=True)
        sel = jnp.zeros(logits.shape, jnp.bool_)
        for _ in range(TOP_K):
            mx = jnp.max(cur, axis=1, keepdims=True)
            first = jnp.min(jnp.where(cur == mx, lane, LANES), axis=1, keepdims=True)
            pick = lane == first
            sel = jnp.logical_or(sel, pick)
            cur = jnp.where(pick, -jnp.inf, cur)
        e = jnp.where(sel, jnp.exp(logits - top), 0.0)
        sel_ref[rows, :] = jnp.where(sel, 1.0, 0.0)
        gate_ref[rows, :] = e / jnp.sum(e, axis=1, keepdims=True)


def _ln1_router(m, x, w_out, g, b, w_router, b_router, alpha, row0, t_all, prev=None):
    t, d = x.shape
    tm = _pick(t, (2 * LN1_SUB_ROWS, LN1_SUB_ROWS, 128))
    assert row0 % tm == 0
    r0 = row0 // tm
    in_specs = [pl.BlockSpec((tm, d), lambda i: (i, 0)),
                pl.BlockSpec((tm, d), lambda i: (i, 0)),
                _const_spec((d, d)), _const_spec((1, d)), _const_spec((1, d)),
                _const_spec((d, 2 * LANES)), _const_spec((1, LANES))]
    args = [m, x, w_out, g, b, w_router, b_router]
    aliases = {}
    if prev is not None:
        in_specs = [pl.BlockSpec(memory_space=pl.ANY)] * 3 + in_specs
        args = list(prev) + args
        aliases = {0: 0, 1: 1, 2: 2}

    return pl.pallas_call(
        functools.partial(_ln1_router_kernel, alpha=alpha, n_skip=len(aliases)),
        out_shape=[jax.ShapeDtypeStruct((t_all, d), F32),
                   jax.ShapeDtypeStruct((t_all, LANES), F32),
                   jax.ShapeDtypeStruct((t_all, LANES), F32)],
        grid=(t // tm,),
        in_specs=in_specs,
        out_specs=[pl.BlockSpec((tm, d), lambda i: (r0 + i, 0)),
                   pl.BlockSpec((tm, LANES), lambda i: (r0 + i, 0)),
                   pl.BlockSpec((tm, LANES), lambda i: (r0 + i, 0))],
        input_output_aliases=aliases,
        compiler_params=_cparams(("parallel",)),
        name="ln1_router",
    )(*args)


def _sort_key(x):
    bits = lax.bitcast_convert_type(x, I32)
    return bits ^ (jnp.right_shift(bits, 31) & 0x7FFFFFFF)


def _count(keys_ref, nch, cw, tvec, strict):
    rows = keys_ref.shape[0]
    tb = jnp.broadcast_to(tvec, (rows, LANES))

    def body(c, cnt):
        off = c * cw
        for s in range(cw // LANES):
            k = keys_ref[:, pl.ds(pl.multiple_of(off + s * LANES, LANES), LANES)]
            hit = (k > tb) if strict else (k >= tb)
            cnt = cnt + jnp.where(hit, 1.0, 0.0)
        return cnt

    cnt = lax.fori_loop(0, nch, body, jnp.zeros((rows, LANES), F32))
    return jnp.sum(cnt, axis=1, keepdims=True)


def _select_threshold(keys_ref, nch, cw, n_sel, few):
    rows = keys_ref.shape[0]
    settled = few > 0.5

    def cond(st):
        b, _, cnt_cur = st
        open_rows = jnp.where(jnp.logical_or(settled, cnt_cur == n_sel), 0.0, 1.0)
        return jnp.logical_and(b < 32, jnp.max(open_rows) > 0.0)

    def bit_step(st):
        b, cur, cnt_cur = st
        cand = cur | lax.shift_left(jnp.int32(1), 31 - b)
        cnt = _count(keys_ref, nch, cw, cand ^ INT_MIN, False)
        take = cnt >= n_sel
        return b + 1, jnp.where(take, cand, cur), jnp.where(take, cnt, cnt_cur)

    everything = jnp.zeros((rows, 1), F32) + jnp.asarray(nch * cw, F32)
    _, cur, n_ge = lax.while_loop(cond, bit_step, (jnp.int32(0), jnp.zeros((rows, 1), I32), everything))
    t = cur ^ INT_MIN
    tied = jnp.where(jnp.logical_or(settled, n_ge <= n_sel), 0.0, 1.0)

    @pl.when(jnp.max(tied) > 0.0)
    def _():
        n_gt = _count(keys_ref, nch, cw, t, True)
        tb = jnp.broadcast_to(t, (rows, LANES))
        needb = jnp.broadcast_to(jnp.where(tied > 0.5, n_sel - n_gt, 1e9), (rows, LANES))
        r = lax.broadcasted_iota(I32, (LANES, LANES), 0)
        c = lax.broadcasted_iota(I32, (LANES, LANES), 1)
        upper = jnp.where(r < c, 1.0, 0.0).astype(BF16)

        def fix(j, run):
            sl = pl.ds(pl.multiple_of(j * LANES, LANES), LANES)
            k = keys_ref[:, sl]
            eq = k == tb
            eqf = jnp.where(eq, 1.0, 0.0)
            before = jnp.dot(eqf.astype(BF16), upper, preferred_element_type=F32) + run
            drop = jnp.logical_and(eq, before >= needb)
            keys_ref[:, sl] = jnp.where(drop, INT_MIN, k)
            return run + jnp.sum(eqf, axis=1, keepdims=True)

        lax.fori_loop(0, nch * (cw // LANES), fix, jnp.zeros((rows, 1), F32))

    return jnp.maximum(t, INT_MIN + 1)


def _attn_prompt_kernel(qi_ref, kw_ref, q_ref, kit_ref, kt_ref, vx_ref, o_ref,
                        keys_scr, lhs_scr, wb_scr, tb_scr, qg_scr, s_scr, p_scr, m_scr, acc_scr,
                        *, n_sel, kc):
    i = pl.program_id(0)
    qb = Q_BLOCK
    nch = (i * qb + qb + kc - 1) // kc
    kw = kw_ref[...]
    lane = lax.broadcasted_iota(I32, (qb, LANES), 1)
    for h in range(IDX_HEADS):
        wb_scr[h] = jnp.broadcast_to(kw[:, IDX_DIM + h:IDX_DIM + h + 1], (qb, LANES))
        blk = qi_ref[:, (h // 2) * LANES:(h // 2 + 1) * LANES]
        keep = (lane < IDX_DIM) if h % 2 == 0 else (lane >= IDX_DIM)
        lhs_scr[h] = jnp.where(keep, blk, jnp.zeros_like(blk))

    sw = min(256, kc)

    def score_chunk(c, carry):
        for s in range(kc // sw):
            o2 = pl.multiple_of(c * kc + s * sw, sw)
            kt = kit_ref[:, pl.ds(o2, sw)]
            acc = jnp.zeros((qb, sw), F32)
            for h in range(IDX_HEADS):
                sc = jnp.dot(lhs_scr[h], kt, preferred_element_type=F32)
                wb = wb_scr[h]
                acc = acc + jnp.maximum(sc, 0.0) * jnp.concatenate([wb] * (sw // LANES), axis=1)
            kpos = o2 + lax.broadcasted_iota(I32, (qb, sw), 1)
            qpos = i * qb + lax.broadcasted_iota(I32, (qb, sw), 0)
            keys_scr[:, pl.ds(o2, sw)] = jnp.where(kpos <= qpos, _sort_key(acc), INT_MIN)
        return carry

    lax.fori_loop(0, nch, score_chunk, 0)
    n_causal = i * qb + lax.broadcasted_iota(I32, (qb, 1), 0) + 1
    t = _select_threshold(keys_scr, nch, kc, n_sel, jnp.where(n_causal <= n_sel, 1.0, 0.0))
    rows = Q_PER_KV * qb
    strip = 32
    nrep = kc // LANES
    tb_scr[...] = jnp.broadcast_to(t, (qb, LANES))

    for g in range(N_KV):
        for j in range(Q_PER_KV):
            h = g * Q_PER_KV + j
            qg_scr[g, j * qb:(j + 1) * qb, :] = q_ref[:, h * HEAD_DIM:(h + 1) * HEAD_DIM]
    m_scr[...] = jnp.full(m_scr.shape, NEG_BIG, F32)
    acc_scr[...] = jnp.zeros_like(acc_scr)

    def body(c, carry):
        off = pl.multiple_of(c * kc, kc)
        for g in range(N_KV):
            s_scr[g] = jnp.dot(qg_scr[g], kt_ref[g * HEAD_DIM:(g + 1) * HEAD_DIM, pl.ds(off, kc)],
                               preferred_element_type=F32)
            for rq in range(0, qb, strip):
                tb = jnp.concatenate([tb_scr[rq:rq + strip, :]] * nrep, axis=1)
                for j in range(Q_PER_KV):
                    r = j * qb + rq
                    km = keys_scr[rq:rq + strip, pl.ds(off, kc)] >= tb
                    s = jnp.where(km, s_scr[g, r:r + strip, :], NEG_BIG)
                    s_scr[g, r:r + strip, :] = s
                    m_old = m_scr[g, r:r + strip, :]
                    m_new = jnp.maximum(m_old, jnp.max(s, axis=1, keepdims=True))
                    alpha = jnp.exp2(m_old - m_new)
                    m_scr[g, r:r + strip, :] = m_new
                    acc_scr[g, r:r + strip, :] = (acc_scr[g, r:r + strip, :]
                                                  * jnp.concatenate([alpha] * 2, axis=1))
            for r in range(0, rows, strip):
                m_new = m_scr[g, r:r + strip, :]
                p = jnp.exp2(s_scr[g, r:r + strip, :] - jnp.concatenate([m_new] * nrep, axis=1))
                p_scr[g, r:r + strip, :] = p.astype(BF16)
            acc_scr[g] += jnp.dot(p_scr[g], vx_ref[pl.ds(off, kc), g * 2 * HEAD_DIM:(g + 1) * 2 * HEAD_DIM],
                                  preferred_element_type=F32)
        return carry

    def body_pair(c2, carry):
        body(2 * c2, carry)
        return body(2 * c2 + 1, carry)

    lax.fori_loop(0, nch // 2, body_pair, 0)

    @pl.when(nch % 2 == 1)
    def _():
        body(nch - 1, 0)

    for g in range(N_KV):
        acc = acc_scr[g]
        out = acc[:, :HEAD_DIM] / acc[:, HEAD_DIM:HEAD_DIM + 1]
        for j in range(Q_PER_KV):
            h = g * Q_PER_KV + j
            o_ref[:, h * HEAD_DIM:(h + 1) * HEAD_DIM] = out[j * qb:(j + 1) * qb].astype(o_ref.dtype)


def _attn_prompt(qi, kw, q, kit2, kt, vx, s, n_sel):
    kc = min(512, s)
    d = N_HEADS * HEAD_DIM
    rows = Q_PER_KV * Q_BLOCK
    return pl.pallas_call(
        functools.partial(_attn_prompt_kernel, n_sel=n_sel, kc=kc),
        out_shape=jax.ShapeDtypeStruct((s, d), BF16),
        grid=(s // Q_BLOCK,),
        in_specs=[pl.BlockSpec((Q_BLOCK, IDX_HEADS * IDX_DIM), lambda i: (i, 0)),
                  pl.BlockSpec((Q_BLOCK, LANES), lambda i: (i, 0)),
                  pl.BlockSpec((Q_BLOCK, d), lambda i: (i, 0)),
                  _const_spec((LANES, s)), _const_spec((N_KV * HEAD_DIM, s)),
                  _const_spec((s, N_KV * 2 * HEAD_DIM))],
        out_specs=pl.BlockSpec((Q_BLOCK, d), lambda i: (i, 0)),
        scratch_shapes=[pltpu.VMEM((Q_BLOCK, s), I32),
                        pltpu.VMEM((IDX_HEADS, Q_BLOCK, LANES), BF16),
                        pltpu.VMEM((IDX_HEADS, Q_BLOCK, LANES), F32),
                        pltpu.VMEM((Q_BLOCK, LANES), I32),
                        pltpu.VMEM((N_KV, rows, HEAD_DIM), BF16),
                        pltpu.VMEM((N_KV, rows, kc), F32),
                        pltpu.VMEM((N_KV, rows, kc), BF16),
                        pltpu.VMEM((N_KV, rows, LANES), F32),
                        pltpu.VMEM((N_KV, rows, 2 * HEAD_DIM), F32)],
        compiler_params=_cparams(("parallel",)),
        name="attn_prompt",
    )(qi, kw, q, kit2, kt, vx)


IDX_SEQS_PER_STEP = 4


def _idx_sample_kernel(pt_ref, qi_ref, w_ref, kn_ref, *refs, npg, past, group):
    pages, o_ref = refs[:group * npg], refs[group * npg]
    lane = lax.broadcasted_iota(I32, (1, LANES), 1)
    for g in range(group):
        qi = qi_ref[g]
        kp = jnp.concatenate([p[...] for p in pages[g * npg:(g + 1) * npg]], axis=1).astype(BF16)
        sc = jnp.dot(qi, kp, preferred_element_type=F32)
        w = w_ref[g]
        o_ref[g, :, 0:past] = jnp.sum(jnp.maximum(sc, 0.0) * w, axis=0, keepdims=True)
        kn = kn_ref[g][:, :IDX_DIM].astype(BF16).astype(F32)
        sn = jnp.sum(qi.astype(F32) * kn, axis=1, keepdims=True)
        new = jnp.sum(jnp.maximum(sn, 0.0) * w, axis=0, keepdims=True)
        o_ref[g, :, past:past + LANES] = jnp.where(lane == 0, new, -jnp.inf)


def _idx_sample(page_table, qi3, w3, kn3, cache_kidx_t):
    n, npg = page_table.shape
    page = cache_kidx_t.shape[2]
    past = npg * page
    group = IDX_SEQS_PER_STEP if n % IDX_SEQS_PER_STEP == 0 else 1
    page_specs = [pl.BlockSpec((None, IDX_DIM, page),
                               lambda b, pt, g=g, p=p: (pt[(b * group + g) * npg + p], 0, 0))
                  for g in range(group) for p in range(npg)]
    return pl.pallas_call(
        functools.partial(_idx_sample_kernel, npg=npg, past=past, group=group),
        out_shape=jax.ShapeDtypeStruct((n, 1, past + LANES), F32),
        grid_spec=pltpu.PrefetchScalarGridSpec(
            num_scalar_prefetch=1, grid=(n // group,),
            in_specs=[pl.BlockSpec((group, IDX_HEADS, IDX_DIM), lambda b, pt: (b, 0, 0)),
                      pl.BlockSpec((group, IDX_HEADS, 1), lambda b, pt: (b, 0, 0)),
                      pl.BlockSpec((group, 1, LANES), lambda b, pt: (b, 0, 0))] + page_specs,
            out_specs=pl.BlockSpec((group, 1, past + LANES), lambda b, pt: (b, 0, 0))),
        compiler_params=_cparams(("parallel",)),
        name="idx_sample",
    )(page_table.reshape(-1), qi3, w3, kn3, *([cache_kidx_t] * (group * npg)))


def _mask_sample_kernel(sc_ref, m_ref, keys_scr, *, n_valid, n_sel):
    rows, width = sc_ref.shape
    pos = lax.broadcasted_iota(I32, (rows, width), 1)
    keys_scr[...] = jnp.where(pos < n_valid, _sort_key(sc_ref[...]), INT_MIN)
    few = jnp.full((rows, 1), 1.0 if n_valid <= n_sel else 0.0, F32)
    t = _select_threshold(keys_scr, width // LANES, LANES, n_sel, few)
    m_ref[...] = jnp.where(keys_scr[...] >= t, 1.0, 0.0)


def _mask_sample(sc, n_valid, n_sel):
    n, width = sc.shape
    return pl.pallas_call(
        functools.partial(_mask_sample_kernel, n_valid=n_valid, n_sel=n_sel),
        out_shape=jax.ShapeDtypeStruct((n, width), F32),
        scratch_shapes=[pltpu.VMEM((n, width), I32)],
        name="mask_sample",
    )(sc)


def _attn_sample_kernel(pt_ref, q_ref, m4_ref, mn_ref, kn_ref, vn_ref, *refs, npg):
    kpages, vpages, o_ref = refs[:npg], refs[npg:2 * npg], refs[2 * npg]
    q = q_ref[...]
    kp = jnp.concatenate([p[...] for p in kpages], axis=0).astype(BF16)
    s = lax.dot_general(q, kp, (((1,), (1,)), ((), ())), preferred_element_type=F32)
    grp = lax.broadcasted_iota(I32, s.shape, 0) // Q_PER_KV
    own = (lax.broadcasted_iota(I32, s.shape, 1) & (N_KV - 1)) == grp
    s = jnp.where(own, jnp.where(m4_ref[...] > 0.5, s, NEG_BIG), NEG_BIG)
    grp_h = lax.broadcasted_iota(I32, (N_HEADS, HEAD_DIM), 0) // Q_PER_KV
    kn = kn_ref[...].astype(F32)
    vn = vn_ref[...].astype(F32)
    kn_h = jnp.zeros((N_HEADS, HEAD_DIM), F32)
    vn_h = jnp.zeros((N_HEADS, HEAD_DIM), F32)
    for g in range(N_KV):
        kn_h = jnp.where(grp_h == g, kn[g:g + 1, :], kn_h)
        vn_h = jnp.where(grp_h == g, vn[g:g + 1, :], vn_h)
    sn = jnp.sum(q.astype(F32) * kn_h, axis=1, keepdims=True)
    sn = jnp.where(mn_ref[:, 0:1] > 0.5, sn, NEG_BIG)
    m = jnp.maximum(jnp.max(s, axis=1, keepdims=True), sn)
    p = jnp.exp2(s - m)
    pn = jnp.exp2(sn - m)
    l = jnp.sum(p, axis=1, keepdims=True) + pn
    vp = jnp.concatenate([r[...] for r in vpages], axis=0).astype(BF16)
    o = jnp.dot(p.astype(BF16), vp, preferred_element_type=F32)
    o_ref[...] = ((o + pn.astype(BF16).astype(F32) * vn_h) / l).astype(o_ref.dtype)


def _attn_sample(page_table, q3, mask4, mask_new, kn3, vn3, cache_k, cache_v):
    n, npg = page_table.shape
    prow = cache_k.shape[1]
    pspec = lambda p: pl.BlockSpec((None, prow, HEAD_DIM), lambda b, pt, p=p: (pt[b * npg + p], 0, 0))
    return pl.pallas_call(
        functools.partial(_attn_sample_kernel, npg=npg),
        out_shape=jax.ShapeDtypeStruct((n, N_HEADS, HEAD_DIM), BF16),
        grid_spec=pltpu.PrefetchScalarGridSpec(
            num_scalar_prefetch=1, grid=(n,),
            in_specs=[pl.BlockSpec((None, N_HEADS, HEAD_DIM), lambda b, pt: (b, 0, 0)),
                      pl.BlockSpec((None, 1, npg * prow), lambda b, pt: (b, 0, 0)),
                      pl.BlockSpec((None, 1, LANES), lambda b, pt: (b, 0, 0)),
                      pl.BlockSpec((None, N_KV, HEAD_DIM), lambda b, pt: (b, 0, 0)),
                      pl.BlockSpec((None, N_KV, HEAD_DIM), lambda b, pt: (b, 0, 0))]
            + [pspec(p) for p in range(npg)] + [pspec(p) for p in range(npg)],
            out_specs=pl.BlockSpec((None, N_HEADS, HEAD_DIM), lambda b, pt: (b, 0, 0))),
        compiler_params=_cparams(("parallel",)),
        name="attn_sample",
    )(page_table.reshape(-1), q3, mask4, mask_new, kn3, vn3, *([cache_k] * npg), *([cache_v] * npg))


MOE_RB = 256
MOE_GROUP = 2
MOE_RC = 1280
MOE_TF = 512
MOE_TN = 512


def _moe_sizes(n_tok):
    n_assign = n_tok * TOP_K
    n_slots = (n_assign // MOE_RB + N_EXPERTS) * MOE_RB
    n_units = N_EXPERTS + n_assign // MOE_RC
    return n_slots, n_units


def _moe_rank_kernel(sel_ref, rank_ref, cnt_ref, carry):
    i = pl.program_id(0)
    tp = sel_ref.shape[0]

    @pl.when(i == 0)
    def _():
        carry[...] = jnp.zeros_like(carry)

    a = sel_ref[...]
    r = lax.broadcasted_iota(I32, (tp, tp), 0)
    c = lax.broadcasted_iota(I32, (tp, tp), 1)
    lower = jnp.where(c < r, 1.0, 0.0).astype(BF16)
    rank_ref[...] = jnp.dot(lower, a.astype(BF16), preferred_element_type=F32) + carry[0:1, :]
    carry[...] = carry[...] + jnp.sum(a, axis=0, keepdims=True)
    cnt_ref[...] = carry[...]


def _moe_rank(sel):
    t = sel.shape[0]
    tp = _pick(t, (256, 128))
    return pl.pallas_call(
        _moe_rank_kernel,
        out_shape=[jax.ShapeDtypeStruct((t, LANES), F32), jax.ShapeDtypeStruct((8, LANES), F32)],
        grid=(t // tp,),
        in_specs=[pl.BlockSpec((tp, LANES), lambda i: (i, 0))],
        out_specs=[pl.BlockSpec((tp, LANES), lambda i: (i, 0)),
                   pl.BlockSpec((8, LANES), lambda i: (0, 0))],
        scratch_shapes=[pltpu.VMEM((8, LANES), F32)],
        compiler_params=_cparams(("arbitrary",)),
        name="moe_rank",
    )(sel)


def _moe_dest_kernel(sel_ref, gate_ref, rank_ref, cnt_ref, dest_ref, g4_ref, tab_ref):
    cnt = cnt_ref[...]
    lane8 = lax.broadcasted_iota(I32, cnt.shape, 1)
    padded = jnp.ceil(cnt * (1.0 / MOE_RB)) * MOE_RB
    incl = padded
    for d in (1, 2, 4, 8, 16, 32, 64):
        incl = incl + jnp.where(lane8 >= d, pltpu.roll(incl, d, 1), 0.0)
    start = incl - padded
    row8 = lax.broadcasted_iota(I32, cnt.shape, 0)
    tab_ref[...] = jnp.where(row8 == 0, start,
                             jnp.where(row8 == 1, padded, jnp.where(row8 == 2, cnt, 0.0))).astype(I32)
    dest = start[0:1, :] + rank_ref[...]
    gate = gate_ref[...]
    cur = sel_ref[...]
    lane = lax.broadcasted_iota(I32, cur.shape, 1)
    d4 = jnp.zeros(cur.shape, F32)
    g4 = jnp.zeros(cur.shape, F32)
    for k in range(TOP_K):
        first = jnp.min(jnp.where(cur > 0.5, lane, LANES), axis=1, keepdims=True)
        pick = lane == first
        dk = jnp.sum(jnp.where(pick, dest, 0.0), axis=1, keepdims=True)
        gk = jnp.sum(jnp.where(pick, gate, 0.0), axis=1, keepdims=True)
        d4 = jnp.where(lane == k, dk, d4)
        g4 = jnp.where(lane == k, gk, g4)
        cur = jnp.where(pick, 0.0, cur)
    dest_ref[...] = d4.astype(I32)
    g4_ref[...] = g4


def _moe_dest(sel, gate, rank, cnt):
    t = sel.shape[0]
    tp = _pick(t, (256, 128))
    row = lambda: pl.BlockSpec((tp, LANES), lambda i: (i, 0))
    one = lambda: pl.BlockSpec((8, LANES), lambda i: (0, 0))
    return pl.pallas_call(
        _moe_dest_kernel,
        out_shape=[jax.ShapeDtypeStruct((t, LANES), I32), jax.ShapeDtypeStruct((t, LANES), F32),
                   jax.ShapeDtypeStruct((8, LANES), I32)],
        grid=(t // tp,),
        in_specs=[row(), row(), row(), one()],
        out_specs=[row(), row(), one()],
        compiler_params=_cparams(("arbitrary",)),
        name="moe_dest",
    )(sel, gate, rank, cnt)


def _moe_units_kernel(tab_ref, unit_ref, *, n_units):
    rcb = MOE_RC // MOE_RB

    def per_expert(e, state):
        u0, _ = state
        nb = tab_ref[1, e] // MOE_RB
        b0 = tab_ref[0, e] // MOE_RB
        nu = (nb + rcb - 1) // rcb

        def per_unit(j, carry):
            unit_ref[0, u0 + j] = e
            unit_ref[1, u0 + j] = b0 + j * rcb
            unit_ref[2, u0 + j] = jnp.minimum(nb - j * rcb, rcb)
            return carry

        lax.fori_loop(0, nu, per_unit, 0)
        return u0 + nu, jnp.where(nu > 0, e, state[1])

    used, last = lax.fori_loop(0, N_EXPERTS, per_expert, (jnp.int32(0), jnp.int32(0)))

    def fill(u, carry):
        unit_ref[0, u] = last
        unit_ref[1, u] = 0
        unit_ref[2, u] = 0
        return carry

    lax.fori_loop(used, n_units, fill, 0)


def _moe_units(tab, n_units):
    return pl.pallas_call(
        functools.partial(_moe_units_kernel, n_units=n_units),
        out_shape=jax.ShapeDtypeStruct((3, n_units), I32),
        in_specs=[pl.BlockSpec(memory_space=pltpu.SMEM)],
        out_specs=pl.BlockSpec(memory_space=pltpu.SMEM),
        name="moe_units",
    )(tab)


def _moe_scatter_kernel(dest_ref, tab_ref, x_ref, xs_ref, zero_scr, sem, zsem):
    tm = x_ref.shape[0]

    @pl.when(pl.program_id(0) == 0)
    def _():
        zero_scr[...] = jnp.zeros_like(zero_scr)

        def pad_copy(row):
            return pltpu.make_async_copy(zero_scr.at[pl.ds(0, 1)], xs_ref.at[pl.ds(row, 1)], zsem)

        def per_expert(e, total):
            first = tab_ref[0, e] + tab_ref[2, e]
            n_pad = tab_ref[1, e] - tab_ref[2, e]

            def one(r, carry):
                pad_copy(first + r).start()
                return carry

            lax.fori_loop(0, n_pad, one, 0)
            return total + n_pad

        total = lax.fori_loop(0, N_EXPERTS, per_expert, jnp.int32(0))

        def drain(r, carry):
            pad_copy(0).wait()
            return carry

        lax.fori_loop(0, total, drain, 0)

    def issue(r, carry):
        for k in range(TOP_K):
            pltpu.make_async_copy(x_ref.at[pl.ds(r, 1)], xs_ref.at[pl.ds(dest_ref[r * TOP_K + k], 1)],
                                  sem).start()
        return carry

    lax.fori_loop(0, tm, issue, 0)
    for _ in range(TOP_K):
        pltpu.make_async_copy(x_ref, xs_ref.at[pl.ds(0, tm)], sem).wait()


def _moe_scatter(dest_flat, tab, x1, n_slots):
    t, d = x1.shape
    tm = _pick(t, (256, 128))
    return pl.pallas_call(
        _moe_scatter_kernel,
        out_shape=jax.ShapeDtypeStruct((n_slots, d), F32),
        grid=(t // tm,),
        in_specs=[pl.BlockSpec((tm * TOP_K,), lambda i: (i,), memory_space=pltpu.SMEM),
                  pl.BlockSpec(memory_space=pltpu.SMEM),
                  pl.BlockSpec((tm, d), lambda i: (i, 0))],
        out_specs=pl.BlockSpec(memory_space=pl.ANY),
        scratch_shapes=[pltpu.VMEM((8, d), F32), pltpu.SemaphoreType.DMA(()),
                        pltpu.SemaphoreType.DMA(())],
        compiler_params=_cparams(("arbitrary",)),
        name="moe_scatter",
    )(dest_flat, tab, x1)


def _moe_expert_kernel(unit_ref, xs_ref, wg_ref, wl_ref, bg_ref, bl_ref, wd_ref, bd_ref, ys_ref,
                       x_scr, act_scr, xst, yst1, yst2, ystg, wg_bf, wl_bf, wd_bf,
                       xsem, ysem1, ysem2, ysemg, *, ju):
    u = pl.program_id(0)
    j = pl.program_id(1)
    b0 = unit_ref[1, u]
    ns = unit_ref[2, u]
    rb = MOE_RB

    def x_copy(s, slot):
        return pltpu.make_async_copy(xs_ref.at[pl.ds((b0 + s) * rb, rb)], xst.at[slot], xsem.at[slot])

    def rows_of(s, n=1):
        return pl.ds(s * rb, n * rb) if isinstance(s, int) else pl.ds(pl.multiple_of(s * rb, rb), n * rb)

    def up_block(s, n=1):
        xb = x_scr[rows_of(s, n), :]
        hg = jnp.dot(xb, wg_bf[...], preferred_element_type=F32) + bg_ref[...]
        hl = jnp.dot(xb, wl_bf[...], preferred_element_type=F32) + bl_ref[...]
        glu = jnp.minimum(hg, SWIGLU_LIMIT)
        lin = jnp.clip(hl, -SWIGLU_LIMIT, SWIGLU_LIMIT)
        act = glu * _sigmoid(SWIGLU_ALPHA * glu) * (lin + 1.0)
        act_scr[rows_of(s, n), pl.ds(pl.multiple_of(j * MOE_TF, MOE_TF), MOE_TF)] = act.astype(BF16)

    n_grp = (ns - 1) // MOE_GROUP
    rem = (ns - 1) % MOE_GROUP
    has2 = rem >= 2
    has1 = rem % 2 == 1
    tail2_at = 1 + MOE_GROUP * n_grp
    tail1_at = tail2_at + jnp.where(has2, 2, 0)

    def cast_up_weights():
        wg_bf[...] = wg_ref[...].astype(BF16)
        wl_bf[...] = wl_ref[...].astype(BF16)

    @pl.when(jnp.logical_and(ns > 0, j == 0))
    def _():
        x_copy(0, 0).start()
        cast_up_weights()

        @pl.when(ns > 1)
        def _():
            x_copy(1, 1).start()

        x_copy(0, 0).wait()
        x_scr[rows_of(0), :] = xst[0].astype(BF16)
        up_block(0)

        def body(s, carry):
            slot = s % 2

            @pl.when(s + 1 < ns)
            def _():
                x_copy(s + 1, 1 - slot).start()

            x_copy(s, slot).wait()
            x_scr[rows_of(s), :] = xst[slot].astype(BF16)
            up_block(s)
            return carry

        lax.fori_loop(1, ns, body, 0)

    @pl.when(jnp.logical_and(ns > 0, jnp.logical_and(j > 0, j < ju)))
    def _():
        cast_up_weights()
        up_block(0)

        def body(p, carry):
            up_block(1 + MOE_GROUP * p, MOE_GROUP)
            return carry

        lax.fori_loop(0, n_grp, body, 0)

        @pl.when(has2)
        def _():
            up_block(tail2_at, 2)

        @pl.when(has1)
        def _():
            up_block(tail1_at)

    @pl.when(jnp.logical_and(ns > 0, j >= ju))
    def _():
        col = pl.multiple_of((j - ju) * MOE_TN, MOE_TN)

        def y_copy(stage, sem, s, n):
            return pltpu.make_async_copy(
                stage, ys_ref.at[pl.ds((b0 + s) * rb, n * rb), pl.ds(col, MOE_TN)], sem)

        def yg_copy(p, slot):
            return y_copy(ystg.at[slot], ysemg.at[slot], 1 + MOE_GROUP * p, MOE_GROUP)

        def down(s, n=1):
            return jnp.dot(act_scr[rows_of(s, n), :], wd_bf[...], preferred_element_type=F32) + bd_ref[...]

        wd_bf[...] = wd_ref[...].astype(BF16)
        yst1[0] = down(0)
        y_copy(yst1.at[0], ysem1.at[0], 0, 1).start()

        def body(p, carry):
            slot = p % 2

            @pl.when(p >= 2)
            def _():
                yg_copy(p - 2, slot).wait()

            ystg[slot] = down(1 + MOE_GROUP * p, MOE_GROUP)
            yg_copy(p, slot).start()
            return carry

        lax.fori_loop(0, n_grp, body, 0)

        @pl.when(has2)
        def _():
            yst2[...] = down(tail2_at, 2)
            y_copy(yst2, ysem2.at[0], tail2_at, 2).start()

        @pl.when(has1)
        def _():
            yst1[1] = down(tail1_at)
            y_copy(yst1.at[1], ysem1.at[1], tail1_at, 1).start()

        @pl.when(n_grp >= 2)
        def _():
            yg_copy(n_grp - 2, n_grp % 2).wait()

        @pl.when(n_grp >= 1)
        def _():
            yg_copy(n_grp - 1, (n_grp - 1) % 2).wait()

        y_copy(yst1.at[0], ysem1.at[0], 0, 1).wait()

        @pl.when(has2)
        def _():
            y_copy(yst2, ysem2.at[0], tail2_at, 2).wait()

        @pl.when(has1)
        def _():
            y_copy(yst1.at[1], ysem1.at[1], tail1_at, 1).wait()


def _moe_experts(units, xs, w_up, b_up, w_down, b_down):
    n_slots, d = xs.shape
    n_units = units.shape[1]
    dff = w_down.shape[1]
    ju, jd = dff // MOE_TF, d // MOE_TN

    def up_idx(off):
        return lambda u, j, un: (un[0, u], 0, off + jnp.where(un[2, u] > 0, jnp.minimum(j, ju - 1), ju - 1))

    def dn_idx(u, j, un):
        return (un[0, u], 0, jnp.where(un[2, u] > 0, jnp.maximum(j - ju, 0), jd - 1))

    return pl.pallas_call(
        functools.partial(_moe_expert_kernel, ju=ju),
        out_shape=jax.ShapeDtypeStruct((n_slots, d), F32),
        grid_spec=pltpu.PrefetchScalarGridSpec(
            num_scalar_prefetch=1, grid=(n_units, ju + jd),
            in_specs=[pl.BlockSpec(memory_space=pl.ANY),
                      pl.BlockSpec((None, d, MOE_TF), up_idx(0)),
                      pl.BlockSpec((None, d, MOE_TF), up_idx(ju)),
                      pl.BlockSpec((None, 1, MOE_TF), up_idx(0)),
                      pl.BlockSpec((None, 1, MOE_TF), up_idx(ju)),
                      pl.BlockSpec((None, dff, MOE_TN), dn_idx),
                      pl.BlockSpec((None, 1, MOE_TN), dn_idx)],
            out_specs=pl.BlockSpec(memory_space=pl.ANY),
            scratch_shapes=[pltpu.VMEM((MOE_RC, d), BF16), pltpu.VMEM((MOE_RC, dff), BF16),
                            pltpu.VMEM((2, MOE_RB, d), F32), pltpu.VMEM((2, MOE_RB, MOE_TN), F32),
                            pltpu.VMEM((2 * MOE_RB, MOE_TN), F32),
                            pltpu.VMEM((2, MOE_GROUP * MOE_RB, MOE_TN), F32),
                            pltpu.VMEM((d, MOE_TF), BF16), pltpu.VMEM((d, MOE_TF), BF16),
                            pltpu.VMEM((dff, MOE_TN), BF16),
                            pltpu.SemaphoreType.DMA((2,)), pltpu.SemaphoreType.DMA((2,)),
                            pltpu.SemaphoreType.DMA((1,)), pltpu.SemaphoreType.DMA((2,))]),
        compiler_params=_cparams(("arbitrary", "arbitrary")),
        name="moe_experts",
    )(units, xs, w_up, w_up, b_up, b_up, w_down, b_down)


COMBINE_SUB_ROWS = 128


def _moe_combine_kernel(dest_ref, g4_ref, x1_ref, ys_ref, p_ref, wg_ref, wp_ref, g_ref, b_ref,
                        y_ref, gbuf, sem, *, alpha):
    tm = x1_ref.shape[0]

    def issue(r, carry):
        for k in range(TOP_K):
            pltpu.make_async_copy(ys_ref.at[pl.ds(dest_ref[r * TOP_K + k], 1)],
                                  gbuf.at[k, pl.ds(r, 1)], sem).start()
        return carry

    lax.fori_loop(0, tm, issue, 0)
    for k in range(TOP_K):
        pltpu.make_async_copy(ys_ref.at[pl.ds(0, tm)], gbuf.at[k], sem).wait()
    sub = min(tm, COMBINE_SUB_ROWS)
    for r in range(0, tm, sub):
        rows = slice(r, r + sub)
        g4 = g4_ref[rows, :]
        moe = g4[:, 0:1] * gbuf[0, rows, :]
        for k in range(1, TOP_K):
            moe = moe + g4[:, k:k + 1] * gbuf[k, rows, :]
        x2 = _layer_norm(alpha * x1_ref[rows, :] + moe, g_ref[...], b_ref[...])
        gate = _sigmoid(jnp.dot(x2.astype(BF16), wg_ref[...], preferred_element_type=F32))
        y_ref[rows, :] = x2 + gate * jnp.dot(p_ref[rows, :].astype(BF16), wp_ref[...],
                                             preferred_element_type=F32)


def _moe_combine(dest_flat, g4, x1, ys, p, w_gate, w_proj, g, b, alpha, row0, t):
    d = x1.shape[1]
    tm = _pick(t, (2 * COMBINE_SUB_ROWS, COMBINE_SUB_ROWS))
    assert row0 % tm == 0
    r0 = row0 // tm
    pd = p.shape[1]
    return pl.pallas_call(
        functools.partial(_moe_combine_kernel, alpha=alpha),
        out_shape=jax.ShapeDtypeStruct((t, d), F32),
        grid=(t // tm,),
        in_specs=[pl.BlockSpec((tm * TOP_K,), lambda i: (r0 + i,), memory_space=pltpu.SMEM),
                  pl.BlockSpec((tm, LANES), lambda i: (r0 + i, 0)),
                  pl.BlockSpec((tm, d), lambda i: (r0 + i, 0)),
                  pl.BlockSpec(memory_space=pl.ANY),
                  pl.BlockSpec((tm, pd), lambda i: (i, 0)),
                  _const_spec((d, d)), _const_spec((pd, d)), _const_spec((1, d)), _const_spec((1, d))],
        out_specs=pl.BlockSpec((tm, d), lambda i: (i, 0)),
        scratch_shapes=[pltpu.VMEM((TOP_K, tm, d), F32), pltpu.SemaphoreType.DMA(())],
        compiler_params=_cparams(("arbitrary",)),
        name="moe_combine",
    )(dest_flat, g4, x1, ys, p, w_gate, w_proj, g, b)


def _project_all(x, w, pos, prompt):
    t128, t64, tkw = _rope_tables(pos)
    xb = x.astype(BF16)
    dq = N_HEADS * HEAD_DIM
    dkv = N_KV * HEAD_DIM
    dqi = IDX_HEADS * IDX_DIM
    d = x.shape[1]
    (q,) = _proj(xb, w, 0, dq, t128, "rope128", ((BF16, None),), scale=LOG2_E * HEAD_DIM ** -0.5)
    k32, kb = _proj(xb, w, dq, dkv, t128, "rope128",
                    ((F32, None), (BF16, "transposed" if prompt else None)))
    v32, vb = _proj(xb, w, dq + dkv, dkv, (), "plain",
                    ((F32, None), (BF16, "with_ones" if prompt else None)))
    (qi,) = _proj(xb, w, dq + 2 * dkv, dqi, t64, "rope64", ((BF16, None),))
    c0 = dq + 2 * dkv + dqi
    (kw,) = _proj(xb, w, c0, LANES, tkw, "rope64", ((F32, None),))
    (zr,) = _proj(xb, w, c0, 4 * d, (), "plain", ((F32, None),), shift=IDX_DIM + IDX_HEADS)
    return q, k32, kb, v32, vb, qi, zr, kw


def kernel(x_prompt, x_sample, cache_k, cache_v, cache_kidx, state_h, state_conv, page_table, p_prompt, p_sample, w_in, conv_w, conv_b, rg_wa, rg_ba, rg_wx, rg_bx, rg_lambda, w_o_rnn, w_o_att, w_out, ln1_g, ln1_b, w_router, b_router, w_up, b_up, w_down, b_down, ln2_g, ln2_b, w_ple_gate, w_ple_proj):
    bsz, s, d = x_prompt.shape
    n, dec_t, _ = x_sample.shape
    assert bsz == 1 and dec_t == 1 and s % LANES == 0 and n % LANES == 0
    depth = w_in.shape[0]
    npg = page_table.shape[1]
    n_phys, page = cache_k.shape[1], cache_k.shape[2]
    past = npg * page
    alpha = (2 * depth) ** 0.25
    dq, dkv, dqi = N_HEADS * HEAD_DIM, N_KV * HEAD_DIM, IDX_HEADS * IDX_DIM
    t_all = s + n
    n_slots, n_units = _moe_sizes(t_all)
    pos_p = jnp.arange(s, dtype=I32)
    pos_s = jnp.full((n,), past, I32)
    row = lambda a: a[None, :]

    hp, hs = x_prompt[0], x_sample[:, 0]
    st_p = [[], [], [], [], []]
    st_s = [[], [], [], [], []]
    for i in range(depth):
        wa, wx = rg_wa[i].astype(BF16), rg_wx[i].astype(BF16)
        rnn_w = (conv_w[i], row(conv_b[i]), wa, wx, row(rg_ba[i]), row(rg_bx[i]), row(rg_lambda[i]))
        wor, woa, wo = w_o_rnn[i].astype(BF16), w_o_att[i].astype(BF16), w_out[i].astype(BF16)
        wr = jnp.pad(w_router[i], ((0, 0), (0, LANES - N_EXPERTS)))
        wr_hi = wr.astype(BF16)
        wr = jnp.concatenate([wr_hi, (wr - wr_hi.astype(F32)).astype(BF16)], axis=1)
        br = jnp.pad(b_router[i], (0, LANES - N_EXPERTS))[None, :]
        ln1 = (row(ln1_g[i]), row(ln1_b[i]))

        w_in_t = jnp.swapaxes(w_in[i], 0, 1)
        q, k32, kt, v32, vx, qi, zr, kw = _project_all(hp, w_in_t, pos_p, True)
        y_rnn, h_p = _rnn_prompt(zr, s, *rnn_w)
        kit = kw[:, :IDX_DIM].T.astype(BF16)
        y_att = _attn_prompt(qi, kw, q, jnp.concatenate([kit, kit], 0), kt, vx, s,
                             min(TOPK_MAX, s // 4))
        m = _merge(y_rnn, y_att, wor, woa, zr)
        bufs = _ln1_router(m, hp, wo, *ln1, wr, br, alpha, 0, t_all)
        st = (k32.reshape(1, s, N_KV, HEAD_DIM), v32.reshape(1, s, N_KV, HEAD_DIM),
              kw[None, :, :IDX_DIM], h_p, zr[None, s - (CONV_W - 1):s, :d])
        for lst, a in zip(st_p, st):
            lst.append(a)

        q, k32, kb, v32, vb, qi, zr, kw = _project_all(hs, w_in_t, pos_s, False)
        y_rnn, h_s = _rnn_sample(zr, state_conv[i], state_h[i], *rnn_w)
        sc = _idx_sample(page_table, qi.reshape(n, IDX_HEADS, IDX_DIM),
                         kw[:, IDX_DIM:IDX_DIM + IDX_HEADS, None], kw[:, None, :],
                         jnp.swapaxes(cache_kidx[i], 1, 2))
        mask = _mask_sample(sc[:, 0], past + 1, min(TOPK_MAX, (past + 1) // 4))
        y_att = _attn_sample(page_table, q.reshape(n, N_HEADS, HEAD_DIM),
                             jnp.repeat(mask[:, :past], N_KV, axis=1)[:, None, :],
                             mask[:, None, past:past + LANES],
                             kb.reshape(n, N_KV, HEAD_DIM), vb.reshape(n, N_KV, HEAD_DIM),
                             cache_k[i].reshape(n_phys, page * N_KV, HEAD_DIM),
                             cache_v[i].reshape(n_phys, page * N_KV, HEAD_DIM)).reshape(n, dq)
        m = _merge(y_rnn, y_att, wor, woa, zr)
        x1, sel, gate = _ln1_router(m, hs, wo, *ln1, wr, br, alpha, s, t_all, prev=bufs)
        st = (k32.reshape(n, 1, N_KV, HEAD_DIM), v32.reshape(n, 1, N_KV, HEAD_DIM),
              kw[:, None, :IDX_DIM], h_s,
              jnp.concatenate([state_conv[i][:, 1:], zr[:, None, :d]], axis=1))
        for lst, a in zip(st_s, st):
            lst.append(a)

        rank, cnt = _moe_rank(sel)
        dest4, g4, tab = _moe_dest(sel, gate, rank, cnt)
        units = _moe_units(tab, n_units)
        dest_flat = dest4[:, :TOP_K].reshape(-1)
        xs = _moe_scatter(dest_flat, tab, x1, n_slots)
        ys = _moe_experts(units, xs, w_up[i], b_up[i][:, None, :], w_down[i], b_down[i][:, None, :])
        tail = (w_ple_gate[i].astype(BF16), w_ple_proj[i].astype(BF16), row(ln2_g[i]), row(ln2_b[i]),
                alpha)
        hp = _moe_combine(dest_flat, g4, x1, ys, p_prompt[i][0], *tail, 0, s)
        hs = _moe_combine(dest_flat, g4, x1, ys, p_sample[i][:, 0], *tail, s, n)

    outs_p = [jnp.stack(l) for l in st_p]
    outs_s = [jnp.stack(l) for l in st_s]
    return (hp[None], hs[:, None], *outs_p, *outs_s)
```

```python
import functools

import jax
import jax.numpy as jnp
from jax import lax
from jax.experimental import pallas as pl
from jax.experimental.pallas import tpu as pltpu

F32 = jnp.float32
BF16 = jnp.bfloat16
I32 = jnp.int32

N_HEADS = 16
HEAD_DIM = 128
N_KV = 4
Q_PER_KV = N_HEADS // N_KV
IDX_HEADS = 16
IDX_DIM = 64
TOPK_MAX = 256
Q_BLOCK = 128
ROPE_THETA = 10000.0
RNN_BW = 128
CONV_W = 4
RG_C = 8.0
N_EXPERTS = 32
TOP_K = 4
SWIGLU_LIMIT = 7.0
SWIGLU_ALPHA = 1.702
LN_EPS = 1e-5
LANES = 128
INT_MIN = -(2 ** 31)
NEG_BIG = -1e30
LOG2_E = 1.4426950408889634
VMEM_LIMIT = 56 * 1024 * 1024


def _pick(n, cands):
    for c in cands:
        if n % c == 0:
            return c
    return n


def _cparams(sem):
    return pltpu.CompilerParams(dimension_semantics=sem, vmem_limit_bytes=VMEM_LIMIT)


def _const_spec(shape):
    nd = len(shape)
    return pl.BlockSpec(shape, lambda *a: (0,) * nd, pipeline_mode=pl.Buffered(1))


def _proj_kernel(*refs, mode, kinds, shift, n_tab, scale):
    n_w = 2 if shift else 1
    x_ref, w_refs = refs[0], refs[1:1 + n_w]
    tabs = refs[1 + n_w:1 + n_w + n_tab]
    outs = refs[1 + n_w + n_tab:1 + n_w + n_tab + len(kinds)]
    w_bf = refs[-1]
    tn = w_bf.shape[1]

    @pl.when(pl.program_id(1) == 0)
    def _():
        for c in range(0, tn, LANES):
            lo = c + shift
            if lo + LANES <= tn:
                rows = w_refs[0][lo:lo + LANES, :]
            elif lo >= tn:
                rows = w_refs[1][lo - tn:lo - tn + LANES, :]
            else:
                rows = jnp.concatenate([w_refs[0][lo:tn, :], w_refs[1][0:lo + LANES - tn, :]], axis=0)
            w_bf[:, c:c + LANES] = rows.T.astype(BF16)

    z = jnp.dot(x_ref[...], w_bf[...], preferred_element_type=F32)
    for h in range(tn // LANES):
        zh = z[:, h * LANES:(h + 1) * LANES]
        if mode == "plain":
            r = zh
        elif mode == "rope128":
            r = zh * tabs[0][...] + pltpu.roll(zh, 64, 1) * tabs[1][...]
        else:
            r = (zh * tabs[0][...] + pltpu.roll(zh, 96, 1) * tabs[1][...]
                 + pltpu.roll(zh, 32, 1) * tabs[2][...])
        if scale != 1.0:
            r = r * scale
        for o, kind in zip(outs, kinds):
            if kind == "transposed":
                o[h * LANES:(h + 1) * LANES, :] = r.T.astype(o.dtype)
            elif kind == "with_ones":
                lane = lax.broadcasted_iota(I32, r.shape, 1)
                o[:, 2 * h * LANES:(2 * h + 1) * LANES] = r.astype(o.dtype)
                o[:, (2 * h + 1) * LANES:(2 * h + 2) * LANES] = jnp.where(lane == 0, 1.0, 0.0).astype(o.dtype)
            else:
                o[:, h * LANES:(h + 1) * LANES] = r.astype(o.dtype)


def _proj(x, wt, col0, ncols, tabs, mode, outs, scale=1.0, shift=0):
    t, k = x.shape
    tm = _pick(t, (1024, 512, 256, 128))
    tn = _pick(ncols, (1024, 512, 256, 128))
    assert col0 % tn == 0 and shift % 8 == 0 and shift < LANES
    c0 = col0 // tn
    in_specs = [pl.BlockSpec((tm, k), lambda j, i: (i, 0)),
                pl.BlockSpec((tn, k), lambda j, i: (c0 + j, 0))]
    ws = [wt]
    if shift:
        in_specs.append(pl.BlockSpec((LANES, k), lambda j, i: ((c0 + j + 1) * (tn // LANES), 0)))
        ws.append(wt)
    in_specs += [pl.BlockSpec((tm, LANES), lambda j, i: (i, 0)) for _ in tabs]
    shapes, specs = [], []
    for dt, kind in outs:
        if kind == "transposed":
            shapes.append(jax.ShapeDtypeStruct((ncols, t), dt))
            specs.append(pl.BlockSpec((tn, tm), lambda j, i: (j, i)))
        elif kind == "with_ones":
            shapes.append(jax.ShapeDtypeStruct((t, 2 * ncols), dt))
            specs.append(pl.BlockSpec((tm, 2 * tn), lambda j, i: (i, j)))
        else:
            shapes.append(jax.ShapeDtypeStruct((t, ncols), dt))
            specs.append(pl.BlockSpec((tm, tn), lambda j, i: (i, j)))
    return pl.pallas_call(
        functools.partial(_proj_kernel, mode=mode, kinds=tuple(kd for _, kd in outs), shift=shift,
                          n_tab=len(tabs), scale=scale),
        out_shape=shapes,
        grid=(ncols // tn, t // tm),
        in_specs=in_specs,
        out_specs=specs,
        scratch_shapes=[pltpu.VMEM((k, tn), BF16)],
        compiler_params=_cparams(("parallel", "arbitrary")),
        name="proj_" + mode,
    )(x, *ws, *tabs)


def _rope_tables(pos):
    posf = pos.astype(F32)[:, None]
    h128 = HEAD_DIM // 2
    inv = ROPE_THETA ** (-jnp.arange(h128, dtype=F32) / h128)
    c, s = jnp.cos(posf * inv), jnp.sin(posf * inv)
    t128 = (jnp.concatenate([c, c], 1), jnp.concatenate([-s, s], 1))
    h64 = IDX_DIM // 2
    inv = ROPE_THETA ** (-jnp.arange(h64, dtype=F32) / h64)
    c, s = jnp.cos(posf * inv), jnp.sin(posf * inv)
    z = jnp.zeros_like(s)
    c64 = jnp.concatenate([c, c], 1)
    sa64 = jnp.concatenate([-s, z], 1)
    sb64 = jnp.concatenate([z, s], 1)
    t64 = tuple(jnp.concatenate([a, a], 1) for a in (c64, sa64, sb64))
    idx_scale = (IDX_HEADS * IDX_DIM) ** -0.5
    n = pos.shape[0]
    ck = jnp.concatenate([c64, jnp.full((n, IDX_HEADS), idx_scale, F32),
                          jnp.zeros((n, LANES - IDX_DIM - IDX_HEADS), F32)], 1)
    z64 = jnp.zeros((n, LANES - IDX_DIM), F32)
    tkw = (ck, jnp.concatenate([sa64, z64], 1), jnp.concatenate([sb64, z64], 1))
    return t128, t64, tkw


def _sigmoid(x):
    return 0.5 * jnp.tanh(0.5 * x) + 0.5


def _softplus_neg(lam):
    return jnp.maximum(-lam, 0.0) + jnp.log1p(jnp.exp(-jnp.abs(lam)))


def _gelu_tanh(x):
    return 0.5 * x * (1.0 + jnp.tanh(0.7978845608028654 * (x + 0.044715 * (x * x * x))))


def _rglru_gates(xc, wa_ref, wx_ref, ba, bx, lam):
    nblk = xc.shape[1] // RNN_BW
    rs, gs = [], []
    for j in range(nblk):
        xj = xc[:, j * RNN_BW:(j + 1) * RNN_BW].astype(BF16)
        rs.append(jnp.dot(xj, wa_ref[j], preferred_element_type=F32))
        gs.append(jnp.dot(xj, wx_ref[j], preferred_element_type=F32))
    r = _sigmoid(jnp.concatenate(rs, 1) + ba)
    g = _sigmoid(jnp.concatenate(gs, 1) + bx)
    log_a = -RG_C * r * _softplus_neg(lam)
    a = jnp.exp(log_a)
    mult = jnp.sqrt(jnp.tanh(-log_a) * (1.0 + a * a))
    return a, mult, g * xc


def _rnn_prompt_kernel(xr_ref, gr_ref, cw_ref, cb_ref, wa_ref, wx_ref, ba_ref, bx_ref, lam_ref,
                       y_ref, hl_ref, xbuf, hcar, a_scr, b_scr, h_scr):
    t = pl.program_id(1)
    tm, cw = xr_ref.shape

    @pl.when(t == 0)
    def _():
        xbuf[0:8, :] = jnp.zeros((8, cw), F32)
        hcar[...] = jnp.zeros_like(hcar)

    x = xr_ref[...]
    xbuf[8:8 + tm, :] = x
    w = cw_ref[...]
    xc = (cb_ref[...] + w[3:4] * x + w[2:3] * xbuf[7:7 + tm, :]
          + w[1:2] * xbuf[6:6 + tm, :] + w[0:1] * xbuf[5:5 + tm, :])
    xbuf[0:8, :] = x[tm - 8:tm, :]
    a, mult, gx = _rglru_gates(xc, wa_ref, wx_ref, ba_ref[...], bx_ref[...], lam_ref[...])
    pos = t * tm + lax.broadcasted_iota(I32, (tm, cw), 0)
    mult = jnp.where(pos == 0, 1.0, mult)
    a_scr[...] = a
    b_scr[...] = mult * gx
    row8 = lax.broadcasted_iota(I32, (8, cw), 0)

    def group(g, carry):
        r0 = pl.multiple_of(g * 8, 8)
        av = a_scr[pl.ds(r0, 8), :]
        bv = b_scr[pl.ds(r0, 8), :]
        for d in (1, 2, 4):
            a_s = pltpu.roll(av, d, 0)
            b_s = pltpu.roll(bv, d, 0)
            m = row8 >= d
            bv = jnp.where(m, av * b_s + bv, bv)
            av = jnp.where(m, av * a_s, av)
        h = av * carry + bv
        h_scr[pl.ds(r0, 8), :] = h
        return h[7:8, :]

    carry = lax.fori_loop(0, tm // 8, group, hcar[0:1, :])
    hcar[0:1, :] = carry
    y_ref[...] = (_gelu_tanh(gr_ref[...]) * h_scr[...]).astype(y_ref.dtype)

    @pl.when(t == pl.num_programs(1) - 1)
    def _():
        hl_ref[...] = carry


def _rnn_prompt(zr, s, conv_w, conv_b, wa, wx, ba, bx, lam):
    d = conv_w.shape[1]
    cw = 512
    tm = _pick(s, (256, 128))
    ncb = d // cw
    nb = cw // RNN_BW
    vec = lambda: pl.BlockSpec((1, cw), lambda c, t: (0, c))
    y, hl = pl.pallas_call(
        _rnn_prompt_kernel,
        out_shape=[jax.ShapeDtypeStruct((s, d), BF16), jax.ShapeDtypeStruct((1, d), F32)],
        grid=(ncb, s // tm),
        in_specs=[pl.BlockSpec((tm, cw), lambda c, t: (t, c)),
                  pl.BlockSpec((tm, cw), lambda c, t: (t, ncb + c)),
                  pl.BlockSpec((CONV_W, cw), lambda c, t: (0, c)),
                  vec(),
                  pl.BlockSpec((nb, RNN_BW, RNN_BW), lambda c, t: (c, 0, 0)),
                  pl.BlockSpec((nb, RNN_BW, RNN_BW), lambda c, t: (c, 0, 0)),
                  vec(), vec(), vec()],
        out_specs=[pl.BlockSpec((tm, cw), lambda c, t: (t, c)),
                   pl.BlockSpec((1, cw), lambda c, t: (0, c))],
        scratch_shapes=[pltpu.VMEM((tm + 8, cw), F32), pltpu.VMEM((8, cw), F32),
                        pltpu.VMEM((tm, cw), F32), pltpu.VMEM((tm, cw), F32),
                        pltpu.VMEM((tm, cw), F32)],
        compiler_params=_cparams(("parallel", "arbitrary")),
        name="rnn_prompt",
    )(zr, zr, conv_w, conv_b, wa, wx, ba, bx, lam)
    return y, hl


def _rnn_sample_kernel(xr_ref, gr_ref, c0_ref, c1_ref, c2_ref, h0_ref, cw_ref, cb_ref,
                       wa_ref, wx_ref, ba_ref, bx_ref, lam_ref, y_ref, h_ref):
    w = cw_ref[...]
    xc = (cb_ref[...] + w[3:4] * xr_ref[...] + w[2:3] * c2_ref[...]
          + w[1:2] * c1_ref[...] + w[0:1] * c0_ref[...])
    a, mult, gx = _rglru_gates(xc, wa_ref, wx_ref, ba_ref[...], bx_ref[...], lam_ref[...])
    h = a * h0_ref[...] + mult * gx
    h_ref[...] = h
    y_ref[...] = (_gelu_tanh(gr_ref[...]) * h).astype(y_ref.dtype)


def _rnn_sample(zr, conv_state, h0, conv_w, conv_b, wa, wx, ba, bx, lam):
    n, d = h0.shape
    cw = 512
    ncb = d // cw
    nb = cw // RNN_BW
    blk = lambda off: pl.BlockSpec((n, cw), lambda c: (0, off + c))
    vec = lambda: pl.BlockSpec((1, cw), lambda c: (0, c))
    wsp = lambda: pl.BlockSpec((nb, RNN_BW, RNN_BW), lambda c: (c, 0, 0))
    return pl.pallas_call(
        _rnn_sample_kernel,
        out_shape=[jax.ShapeDtypeStruct((n, d), BF16), jax.ShapeDtypeStruct((n, d), F32)],
        grid=(ncb,),
        in_specs=[blk(0), blk(ncb), blk(0), blk(0), blk(0), blk(0),
                  pl.BlockSpec((CONV_W, cw), lambda c: (0, c)), vec(), wsp(), wsp(),
                  vec(), vec(), vec()],
        out_specs=[blk(0), blk(0)],
        compiler_params=_cparams(("parallel",)),
        name="rnn_sample",
    )(zr, zr, conv_state[:, 0], conv_state[:, 1], conv_state[:, 2], h0,
      conv_w, conv_b, wa, wx, ba, bx, lam)


def _layer_norm(x, g, b):
    mu = jnp.mean(x, axis=-1, keepdims=True)
    xc = x - mu
    var = jnp.mean(xc * xc, axis=-1, keepdims=True)
    return xc * lax.rsqrt(var + LN_EPS) * g + b


def _merge_kernel(yr_ref, ya_ref, wr_ref, wa_ref, ga_ref, gb_ref, o_ref):
    a = jnp.dot(yr_ref[...], wr_ref[...], preferred_element_type=F32)
    b = jnp.dot(ya_ref[...], wa_ref[...], preferred_element_type=F32)
    m = _sigmoid(ga_ref[...]) * a + _sigmoid(gb_ref[...]) * b
    o_ref[...] = m.astype(o_ref.dtype)


def _merge(y_rnn, y_att, w_o_rnn, w_o_att, zr):
    t, d = y_rnn.shape
    tm = _pick(t, (512, 256, 128))
    tn = 512
    nj = d // tn
    return pl.pallas_call(
        _merge_kernel,
        out_shape=jax.ShapeDtypeStruct((t, d), BF16),
        grid=(t // tm, nj),
        in_specs=[pl.BlockSpec((tm, d), lambda i, j: (i, 0)),
                  pl.BlockSpec((tm, d), lambda i, j: (i, 0)),
                  pl.BlockSpec((d, tn), lambda i, j: (0, j)),
                  pl.BlockSpec((d, tn), lambda i, j: (0, j)),
                  pl.BlockSpec((tm, tn), lambda i, j: (i, 2 * nj + j)),
                  pl.BlockSpec((tm, tn), lambda i, j: (i, 3 * nj + j))],
        out_specs=pl.BlockSpec((tm, tn), lambda i, j: (i, j)),
        compiler_params=_cparams(("parallel", "parallel")),
        name="merge",
    )(y_rnn, y_att, w_o_rnn, w_o_att, zr, zr)


LN1_SUB_ROWS = 256


def _ln1_router_kernel(*refs, alpha, n_skip):
    m_ref, x_ref, w_ref, g_ref, b_ref, wr_ref, br_ref, x1_ref, sel_ref, gate_ref = refs[n_skip:]
    tm = x_ref.shape[0]
    sub = min(tm, LN1_SUB_ROWS)
    for r in range(0, tm, sub):
        rows = slice(r, r + sub)
        y = alpha * x_ref[rows, :] + jnp.dot(m_ref[rows, :], w_ref[...], preferred_element_type=F32)
        x1 = _layer_norm(y, g_ref[...], b_ref[...])
        x1_ref[rows, :] = x1
        x_hi = x1.astype(BF16)
        x_lo = (x1 - x_hi.astype(F32)).astype(BF16)
        part = (jnp.dot(x_hi, wr_ref[...], preferred_element_type=F32)
                + jnp.dot(x_lo, wr_ref[...], preferred_element_type=F32))
        logits = part[:, :LANES] + part[:, LANES:] + br_ref[...]
        lane = lax.broadcasted_iota(I32, logits.shape, 1)
        live = lane < N_EXPERTS
        cur = jnp.where(live, logits, -jnp.inf)
        top = jnp.max(cur, axis=1, keepdims=True)
        sel = jnp.zeros(logits.shape, jnp.bool_)
        for _ in range(TOP_K):
            mx = jnp.max(cur, axis=1, keepdims=True)
            first = jnp.min(jnp.where(cur == mx, lane, LANES), axis=1, keepdims=True)
            pick = lane == first
            sel = jnp.logical_or(sel, pick)
            cur = jnp.where(pick, -jnp.inf, cur)
        e = jnp.where(sel, jnp.exp(logits - top), 0.0)
        sel_ref[rows, :] = jnp.where(sel, 1.0, 0.0)
        gate_ref[rows, :] = e / jnp.sum(e, axis=1, keepdims=True)


def _ln1_router(m, x, w_out, g, b, w_router, b_router, alpha, row0, t_all, prev=None):
    t, d = x.shape
    tm = _pick(t, (2 * LN1_SUB_ROWS, LN1_SUB_ROWS, 128))
    assert row0 % tm == 0
    r0 = row0 // tm
    in_specs = [pl.BlockSpec((tm, d), lambda i: (i, 0)),
                pl.BlockSpec((tm, d), lambda i: (i, 0)),
                _const_spec((d, d)), _const_spec((1, d)), _const_spec((1, d)),
                _const_spec((d, 2 * LANES)), _const_spec((1, LANES))]
    args = [m, x, w_out, g, b, w_router, b_router]
    aliases = {}
    if prev is not None:
        in_specs = [pl.BlockSpec(memory_space=pl.ANY)] * 3 + in_specs
        args = list(prev) + args
        aliases = {0: 0, 1: 1, 2: 2}

    return pl.pallas_call(
        functools.partial(_ln1_router_kernel, alpha=alpha, n_skip=len(aliases)),
        out_shape=[jax.ShapeDtypeStruct((t_all, d), F32),
                   jax.ShapeDtypeStruct((t_all, LANES), F32),
                   jax.ShapeDtypeStruct((t_all, LANES), F32)],
        grid=(t // tm,),
        in_specs=in_specs,
        out_specs=[pl.BlockSpec((tm, d), lambda i: (r0 + i, 0)),
                   pl.BlockSpec((tm, LANES), lambda i: (r0 + i, 0)),
                   pl.BlockSpec((tm, LANES), lambda i: (r0 + i, 0))],
        input_output_aliases=aliases,
        compiler_params=_cparams(("parallel",)),
        name="ln1_router",
    )(*args)


def _sort_key(x):
    bits = lax.bitcast_convert_type(x, I32)
    return bits ^ (jnp.right_shift(bits, 31) & 0x7FFFFFFF)


def _count(keys_ref, nch, cw, tvec, strict):
    rows = keys_ref.shape[0]
    tb = jnp.broadcast_to(tvec, (rows, LANES))

    def body(c, cnt):
        off = c * cw
        for s in range(cw // LANES):
            k = keys_ref[:, pl.ds(pl.multiple_of(off + s * LANES, LANES), LANES)]
            hit = (k > tb) if strict else (k >= tb)
            cnt = cnt + jnp.where(hit, 1.0, 0.0)
        return cnt

    cnt = lax.fori_loop(0, nch, body, jnp.zeros((rows, LANES), F32))
    return jnp.sum(cnt, axis=1, keepdims=True)


def _select_threshold(keys_ref, nch, cw, n_sel, few):
    rows = keys_ref.shape[0]
    settled = few > 0.5

    def cond(st):
        b, _, cnt_cur = st
        open_rows = jnp.where(jnp.logical_or(settled, cnt_cur == n_sel), 0.0, 1.0)
        return jnp.logical_and(b < 32, jnp.max(open_rows) > 0.0)

    def bit_step(st):
        b, cur, cnt_cur = st
        cand = cur | lax.shift_left(jnp.int32(1), 31 - b)
        cnt = _count(keys_ref, nch, cw, cand ^ INT_MIN, False)
        take = cnt >= n_sel
        return b + 1, jnp.where(take, cand, cur), jnp.where(take, cnt, cnt_cur)

    everything = jnp.zeros((rows, 1), F32) + jnp.asarray(nch * cw, F32)
    _, cur, n_ge = lax.while_loop(cond, bit_step, (jnp.int32(0), jnp.zeros((rows, 1), I32), everything))
    t = cur ^ INT_MIN
    tied = jnp.where(jnp.logical_or(settled, n_ge <= n_sel), 0.0, 1.0)

    @pl.when(jnp.max(tied) > 0.0)
    def _():
        n_gt = _count(keys_ref, nch, cw, t, True)
        tb = jnp.broadcast_to(t, (rows, LANES))
        needb = jnp.broadcast_to(jnp.where(tied > 0.5, n_sel - n_gt, 1e9), (rows, LANES))
        r = lax.broadcasted_iota(I32, (LANES, LANES), 0)
        c = lax.broadcasted_iota(I32, (LANES, LANES), 1)
        upper = jnp.where(r < c, 1.0, 0.0).astype(BF16)

        def fix(j, run):
            sl = pl.ds(pl.multiple_of(j * LANES, LANES), LANES)
            k = keys_ref[:, sl]
            eq = k == tb
            eqf = jnp.where(eq, 1.0, 0.0)
            before = jnp.dot(eqf.astype(BF16), upper, preferred_element_type=F32) + run
            drop = jnp.logical_and(eq, before >= needb)
            keys_ref[:, sl] = jnp.where(drop, INT_MIN, k)
            return run + jnp.sum(eqf, axis=1, keepdims=True)

        lax.fori_loop(0, nch * (cw // LANES), fix, jnp.zeros((rows, 1), F32))

    return jnp.maximum(t, INT_MIN + 1)


def _attn_prompt_kernel(qi_ref, kw_ref, q_ref, kit_ref, kt_ref, vx_ref, o_ref,
                        keys_scr, lhs_scr, wb_scr, tb_scr, qg_scr, s_scr, p_scr, m_scr, acc_scr,
                        *, n_sel, kc):
    i = pl.program_id(0)
    qb = Q_BLOCK
    nch = (i * qb + qb + kc - 1) // kc
    kw = kw_ref[...]
    lane = lax.broadcasted_iota(I32, (qb, LANES), 1)
    for h in range(IDX_HEADS):
        wb_scr[h] = jnp.broadcast_to(kw[:, IDX_DIM + h:IDX_DIM + h + 1], (qb, LANES))
        blk = qi_ref[:, (h // 2) * LANES:(h // 2 + 1) * LANES]
        keep = (lane < IDX_DIM) if h % 2 == 0 else (lane >= IDX_DIM)
        lhs_scr[h] = jnp.where(keep, blk, jnp.zeros_like(blk))

    sw = min(256, kc)

    def score_chunk(c, carry):
        for s in range(kc // sw):
            o2 = pl.multiple_of(c * kc + s * sw, sw)
            kt = kit_ref[:, pl.ds(o2, sw)]
            acc = jnp.zeros((qb, sw), F32)
            for h in range(IDX_HEADS):
                sc = jnp.dot(lhs_scr[h], kt, preferred_element_type=F32)
                wb = wb_scr[h]
                acc = acc + jnp.maximum(sc, 0.0) * jnp.concatenate([wb] * (sw // LANES), axis=1)
            kpos = o2 + lax.broadcasted_iota(I32, (qb, sw), 1)
            qpos = i * qb + lax.broadcasted_iota(I32, (qb, sw), 0)
            keys_scr[:, pl.ds(o2, sw)] = jnp.where(kpos <= qpos, _sort_key(acc), INT_MIN)
        return carry

    lax.fori_loop(0, nch, score_chunk, 0)
    n_causal = i * qb + lax.broadcasted_iota(I32, (qb, 1), 0) + 1
    t = _select_threshold(keys_scr, nch, kc, n_sel, jnp.where(n_causal <= n_sel, 1.0, 0.0))
    rows = Q_PER_KV * qb
    strip = 32
    nrep = kc // LANES
    tb_scr[...] = jnp.broadcast_to(t, (qb, LANES))

    for g in range(N_KV):
        for j in range(Q_PER_KV):
            h = g * Q_PER_KV + j
            qg_scr[g, j * qb:(j + 1) * qb, :] = q_ref[:, h * HEAD_DIM:(h + 1) * HEAD_DIM]
    m_scr[...] = jnp.full(m_scr.shape, NEG_BIG, F32)
    acc_scr[...] = jnp.zeros_like(acc_scr)

    def body(c, carry):
        off = pl.multiple_of(c * kc, kc)
        for g in range(N_KV):
            s_scr[g] = jnp.dot(qg_scr[g], kt_ref[g * HEAD_DIM:(g + 1) * HEAD_DIM, pl.ds(off, kc)],
                               preferred_element_type=F32)
            for rq in range(0, qb, strip):
                tb = jnp.concatenate([tb_scr[rq:rq + strip, :]] * nrep, axis=1)
                for j in range(Q_PER_KV):
                    r = j * qb + rq
                    km = keys_scr[rq:rq + strip, pl.ds(off, kc)] >= tb
                    s = jnp.where(km, s_scr[g, r:r + strip, :], NEG_BIG)
                    s_scr[g, r:r + strip, :] = s
                    m_old = m_scr[g, r:r + strip, :]
                    m_new = jnp.maximum(m_old, jnp.max(s, axis=1, keepdims=True))
                    alpha = jnp.exp2(m_old - m_new)
                    m_scr[g, r:r + strip, :] = m_new
                    acc_scr[g, r:r + strip, :] = (acc_scr[g, r:r + strip, :]
                                                  * jnp.concatenate([alpha] * 2, axis=1))
            for r in range(0, rows, strip):
                m_new = m_scr[g, r:r + strip, :]
                p = jnp.exp2(s_scr[g, r:r + strip, :] - jnp.concatenate([m_new] * nrep, axis=1))
                p_scr[g, r:r + strip, :] = p.astype(BF16)
            acc_scr[g] += jnp.dot(p_scr[g], vx_ref[pl.ds(off, kc), g * 2 * HEAD_DIM:(g + 1) * 2 * HEAD_DIM],
                                  preferred_element_type=F32)
        return carry

    def body_pair(c2, carry):
        body(2 * c2, carry)
        return body(2 * c2 + 1, carry)

    lax.fori_loop(0, nch // 2, body_pair, 0)

    @pl.when(nch % 2 == 1)
    def _():
        body(nch - 1, 0)

    for g in range(N_KV):
        acc = acc_scr[g]
        out = acc[:, :HEAD_DIM] / acc[:, HEAD_DIM:HEAD_DIM + 1]
        for j in range(Q_PER_KV):
            h = g * Q_PER_KV + j
            o_ref[:, h * HEAD_DIM:(h + 1) * HEAD_DIM] = out[j * qb:(j + 1) * qb].astype(o_ref.dtype)


def _attn_prompt(qi, kw, q, kit2, kt, vx, s, n_sel):
    kc = min(512, s)
    d = N_HEADS * HEAD_DIM
    rows = Q_PER_KV * Q_BLOCK
    return pl.pallas_call(
        functools.partial(_attn_prompt_kernel, n_sel=n_sel, kc=kc),
        out_shape=jax.ShapeDtypeStruct((s, d), BF16),
        grid=(s // Q_BLOCK,),
        in_specs=[pl.BlockSpec((Q_BLOCK, IDX_HEADS * IDX_DIM), lambda i: (i, 0)),
                  pl.BlockSpec((Q_BLOCK, LANES), lambda i: (i, 0)),
                  pl.BlockSpec((Q_BLOCK, d), lambda i: (i, 0)),
                  _const_spec((LANES, s)), _const_spec((N_KV * HEAD_DIM, s)),
                  _const_spec((s, N_KV * 2 * HEAD_DIM))],
        out_specs=pl.BlockSpec((Q_BLOCK, d), lambda i: (i, 0)),
        scratch_shapes=[pltpu.VMEM((Q_BLOCK, s), I32),
                        pltpu.VMEM((IDX_HEADS, Q_BLOCK, LANES), BF16),
                        pltpu.VMEM((IDX_HEADS, Q_BLOCK, LANES), F32),
                        pltpu.VMEM((Q_BLOCK, LANES), I32),
                        pltpu.VMEM((N_KV, rows, HEAD_DIM), BF16),
                        pltpu.VMEM((N_KV, rows, kc), F32),
                        pltpu.VMEM((N_KV, rows, kc), BF16),
                        pltpu.VMEM((N_KV, rows, LANES), F32),
                        pltpu.VMEM((N_KV, rows, 2 * HEAD_DIM), F32)],
        compiler_params=_cparams(("parallel",)),
        name="attn_prompt",
    )(qi, kw, q, kit2, kt, vx)


IDX_SEQS_PER_STEP = 4


def _idx_sample_kernel(pt_ref, qi_ref, w_ref, kn_ref, *refs, npg, past, group):
    pages, o_ref = refs[:group * npg], refs[group * npg]
    lane = lax.broadcasted_iota(I32, (1, LANES), 1)
    for g in range(group):
        qi = qi_ref[g]
        kp = jnp.concatenate([p[...] for p in pages[g * npg:(g + 1) * npg]], axis=1).astype(BF16)
        sc = jnp.dot(qi, kp, preferred_element_type=F32)
        w = w_ref[g]
        o_ref[g, :, 0:past] = jnp.sum(jnp.maximum(sc, 0.0) * w, axis=0, keepdims=True)
        kn = kn_ref[g][:, :IDX_DIM].astype(BF16).astype(F32)
        sn = jnp.sum(qi.astype(F32) * kn, axis=1, keepdims=True)
        new = jnp.sum(jnp.maximum(sn, 0.0) * w, axis=0, keepdims=True)
        o_ref[g, :, past:past + LANES] = jnp.where(lane == 0, new, -jnp.inf)


def _idx_sample(page_table, qi3, w3, kn3, cache_kidx_t):
    n, npg = page_table.shape
    page = cache_kidx_t.shape[2]
    past = npg * page
    group = IDX_SEQS_PER_STEP if n % IDX_SEQS_PER_STEP == 0 else 1
    page_specs = [pl.BlockSpec((None, IDX_DIM, page),
                               lambda b, pt, g=g, p=p: (pt[(b * group + g) * npg + p], 0, 0))
                  for g in range(group) for p in range(npg)]
    return pl.pallas_call(
        functools.partial(_idx_sample_kernel, npg=npg, past=past, group=group),
        out_shape=jax.ShapeDtypeStruct((n, 1, past + LANES), F32),
        grid_spec=pltpu.PrefetchScalarGridSpec(
            num_scalar_prefetch=1, grid=(n // group,),
            in_specs=[pl.BlockSpec((group, IDX_HEADS, IDX_DIM), lambda b, pt: (b, 0, 0)),
                      pl.BlockSpec((group, IDX_HEADS, 1), lambda b, pt: (b, 0, 0)),
                      pl.BlockSpec((group, 1, LANES), lambda b, pt: (b, 0, 0))] + page_specs,
            out_specs=pl.BlockSpec((group, 1, past + LANES), lambda b, pt: (b, 0, 0))),
        compiler_params=_cparams(("parallel",)),
        name="idx_sample",
    )(page_table.reshape(-1), qi3, w3, kn3, *([cache_kidx_t] * (group * npg)))


def _mask_sample_kernel(sc_ref, m_ref, keys_scr, *, n_valid, n_sel):
    rows, width = sc_ref.shape
    pos = lax.broadcasted_iota(I32, (rows, width), 1)
    keys_scr[...] = jnp.where(pos < n_valid, _sort_key(sc_ref[...]), INT_MIN)
    few = jnp.full((rows, 1), 1.0 if n_valid <= n_sel else 0.0, F32)
    t = _select_threshold(keys_scr, width // LANES, LANES, n_sel, few)
    m_ref[...] = jnp.where(keys_scr[...] >= t, 1.0, 0.0)


def _mask_sample(sc, n_valid, n_sel):
    n, width = sc.shape
    return pl.pallas_call(
        functools.partial(_mask_sample_kernel, n_valid=n_valid, n_sel=n_sel),
        out_shape=jax.ShapeDtypeStruct((n, width), F32),
        scratch_shapes=[pltpu.VMEM((n, width), I32)],
        name="mask_sample",
    )(sc)


def _attn_sample_kernel(pt_ref, q_ref, m4_ref, mn_ref, kn_ref, vn_ref, *refs, npg):
    kpages, vpages, o_ref = refs[:npg], refs[npg:2 * npg], refs[2 * npg]
    q = q_ref[...]
    kp = jnp.concatenate([p[...] for p in kpages], axis=0).astype(BF16)
    s = lax.dot_general(q, kp, (((1,), (1,)), ((), ())), preferred_element_type=F32)
    grp = lax.broadcasted_iota(I32, s.shape, 0) // Q_PER_KV
    own = (lax.broadcasted_iota(I32, s.shape, 1) & (N_KV - 1)) == grp
    s = jnp.where(own, jnp.where(m4_ref[...] > 0.5, s, NEG_BIG), NEG_BIG)
    grp_h = lax.broadcasted_iota(I32, (N_HEADS, HEAD_DIM), 0) // Q_PER_KV
    kn = kn_ref[...].astype(F32)
    vn = vn_ref[...].astype(F32)
    kn_h = jnp.zeros((N_HEADS, HEAD_DIM), F32)
    vn_h = jnp.zeros((N_HEADS, HEAD_DIM), F32)
    for g in range(N_KV):
        kn_h = jnp.where(grp_h == g, kn[g:g + 1, :], kn_h)
        vn_h = jnp.where(grp_h == g, vn[g:g + 1, :], vn_h)
    sn = jnp.sum(q.astype(F32) * kn_h, axis=1, keepdims=True)
    sn = jnp.where(mn_ref[:, 0:1] > 0.5, sn, NEG_BIG)
    m = jnp.maximum(jnp.max(s, axis=1, keepdims=True), sn)
    p = jnp.exp2(s - m)
    pn = jnp.exp2(sn - m)
    l = jnp.sum(p, axis=1, keepdims=True) + pn
    vp = jnp.concatenate([r[...] for r in vpages], axis=0).astype(BF16)
    o = jnp.dot(p.astype(BF16), vp, preferred_element_type=F32)
    o_ref[...] = ((o + pn.astype(BF16).astype(F32) * vn_h) / l).astype(o_ref.dtype)


def _attn_sample(page_table, q3, mask4, mask_new, kn3, vn3, cache_k, cache_v):
    n, npg = page_table.shape
    prow = cache_k.shape[1]
    pspec = lambda p: pl.BlockSpec((None, prow, HEAD_DIM), lambda b, pt, p=p: (pt[b * npg + p], 0, 0))
    return pl.pallas_call(
        functools.partial(_attn_sample_kernel, npg=npg),
        out_shape=jax.ShapeDtypeStruct((n, N_HEADS, HEAD_DIM), BF16),
        grid_spec=pltpu.PrefetchScalarGridSpec(
            num_scalar_prefetch=1, grid=(n,),
            in_specs=[pl.BlockSpec((None, N_HEADS, HEAD_DIM), lambda b, pt: (b, 0, 0)),
                      pl.BlockSpec((None, 1, npg * prow), lambda b, pt: (b, 0, 0)),
                      pl.BlockSpec((None, 1, LANES), lambda b, pt: (b, 0, 0)),
                      pl.BlockSpec((None, N_KV, HEAD_DIM), lambda b, pt: (b, 0, 0)),
                      pl.BlockSpec((None, N_KV, HEAD_DIM), lambda b, pt: (b, 0, 0))]
            + [pspec(p) for p in range(npg)] + [pspec(p) for p in range(npg)],
            out_specs=pl.BlockSpec((None, N_HEADS, HEAD_DIM), lambda b, pt: (b, 0, 0))),
        compiler_params=_cparams(("parallel",)),
        name="attn_sample",
    )(page_table.reshape(-1), q3, mask4, mask_new, kn3, vn3, *([cache_k] * npg), *([cache_v] * npg))


MOE_RB = 256
MOE_GROUP = 2
MOE_RC = 1280
MOE_TF = 512
MOE_TN = 512


def _moe_sizes(n_tok):
    n_assign = n_tok * TOP_K
    n_slots = (n_assign // MOE_RB + N_EXPERTS) * MOE_RB
    n_units = N_EXPERTS + n_assign // MOE_RC
    return n_slots, n_units


def _moe_rank_kernel(sel_ref, rank_ref, cnt_ref, carry):
    i = pl.program_id(0)
    tp = sel_ref.shape[0]

    @pl.when(i == 0)
    def _():
        carry[...] = jnp.zeros_like(carry)

    a = sel_ref[...]
    r = lax.broadcasted_iota(I32, (tp, tp), 0)
    c = lax.broadcasted_iota(I32, (tp, tp), 1)
    lower = jnp.where(c < r, 1.0, 0.0).astype(BF16)
    rank_ref[...] = jnp.dot(lower, a.astype(BF16), preferred_element_type=F32) + carry[0:1, :]
    carry[...] = carry[...] + jnp.sum(a, axis=0, keepdims=True)
    cnt_ref[...] = carry[...]


def _moe_rank(sel):
    t = sel.shape[0]
    tp = _pick(t, (256, 128))
    return pl.pallas_call(
        _moe_rank_kernel,
        out_shape=[jax.ShapeDtypeStruct((t, LANES), F32), jax.ShapeDtypeStruct((8, LANES), F32)],
        grid=(t // tp,),
        in_specs=[pl.BlockSpec((tp, LANES), lambda i: (i, 0))],
        out_specs=[pl.BlockSpec((tp, LANES), lambda i: (i, 0)),
                   pl.BlockSpec((8, LANES), lambda i: (0, 0))],
        scratch_shapes=[pltpu.VMEM((8, LANES), F32)],
        compiler_params=_cparams(("arbitrary",)),
        name="moe_rank",
    )(sel)


def _moe_dest_kernel(sel_ref, gate_ref, rank_ref, cnt_ref, dest_ref, g4_ref, tab_ref):
    cnt = cnt_ref[...]
    lane8 = lax.broadcasted_iota(I32, cnt.shape, 1)
    padded = jnp.ceil(cnt * (1.0 / MOE_RB)) * MOE_RB
    incl = padded
    for d in (1, 2, 4, 8, 16, 32, 64):
        incl = incl + jnp.where(lane8 >= d, pltpu.roll(incl, d, 1), 0.0)
    start = incl - padded
    row8 = lax.broadcasted_iota(I32, cnt.shape, 0)
    tab_ref[...] = jnp.where(row8 == 0, start,
                             jnp.where(row8 == 1, padded, jnp.where(row8 == 2, cnt, 0.0))).astype(I32)
    dest = start[0:1, :] + rank_ref[...]
    gate = gate_ref[...]
    cur = sel_ref[...]
    lane = lax.broadcasted_iota(I32, cur.shape, 1)
    d4 = jnp.zeros(cur.shape, F32)
    g4 = jnp.zeros(cur.shape, F32)
    for k in range(TOP_K):
        first = jnp.min(jnp.where(cur > 0.5, lane, LANES), axis=1, keepdims=True)
        pick = lane == first
        dk = jnp.sum(jnp.where(pick, dest, 0.0), axis=1, keepdims=True)
        gk = jnp.sum(jnp.where(pick, gate, 0.0), axis=1, keepdims=True)
        d4 = jnp.where(lane == k, dk, d4)
        g4 = jnp.where(lane == k, gk, g4)
        cur = jnp.where(pick, 0.0, cur)
    dest_ref[...] = d4.astype(I32)
    g4_ref[...] = g4


def _moe_dest(sel, gate, rank, cnt):
    t = sel.shape[0]
    tp = _pick(t, (256, 128))
    row = lambda: pl.BlockSpec((tp, LANES), lambda i: (i, 0))
    one = lambda: pl.BlockSpec((8, LANES), lambda i: (0, 0))
    return pl.pallas_call(
        _moe_dest_kernel,
        out_shape=[jax.ShapeDtypeStruct((t, LANES), I32), jax.ShapeDtypeStruct((t, LANES), F32),
                   jax.ShapeDtypeStruct((8, LANES), I32)],
        grid=(t // tp,),
        in_specs=[row(), row(), row(), one()],
        out_specs=[row(), row(), one()],
        compiler_params=_cparams(("arbitrary",)),
        name="moe_dest",
    )(sel, gate, rank, cnt)


def _moe_units_kernel(tab_ref, unit_ref, *, n_units):
    rcb = MOE_RC // MOE_RB

    def per_expert(e, state):
        u0, _ = state
        nb = tab_ref[1, e] // MOE_RB
        b0 = tab_ref[0, e] // MOE_RB
        nu = (nb + rcb - 1) // rcb

        def per_unit(j, carry):
            unit_ref[0, u0 + j] = e
            unit_ref[1, u0 + j] = b0 + j * rcb
            unit_ref[2, u0 + j] = jnp.minimum(nb - j * rcb, rcb)
            return carry

        lax.fori_loop(0, nu, per_unit, 0)
        return u0 + nu, jnp.where(nu > 0, e, state[1])

    used, last = lax.fori_loop(0, N_EXPERTS, per_expert, (jnp.int32(0), jnp.int32(0)))

    def fill(u, carry):
        unit_ref[0, u] = last
        unit_ref[1, u] = 0
        unit_ref[2, u] = 0
        return carry

    lax.fori_loop(used, n_units, fill, 0)


def _moe_units(tab, n_units):
    return pl.pallas_call(
        functools.partial(_moe_units_kernel, n_units=n_units),
        out_shape=jax.ShapeDtypeStruct((3, n_units), I32),
        in_specs=[pl.BlockSpec(memory_space=pltpu.SMEM)],
        out_specs=pl.BlockSpec(memory_space=pltpu.SMEM),
        name="moe_units",
    )(tab)


def _moe_scatter_kernel(dest_ref, tab_ref, x_ref, xs_ref, zero_scr, sem, zsem):
    tm = x_ref.shape[0]

    @pl.when(pl.program_id(0) == 0)
    def _():
        zero_scr[...] = jnp.zeros_like(zero_scr)

        def row_copy(row):
            return pltpu.make_async_copy(zero_scr.at[pl.ds(0, 1)], xs_ref.at[pl.ds(row, 1)], zsem.at[0])

        def oct_copy(row):
            return pltpu.make_async_copy(zero_scr, xs_ref.at[pl.ds(pl.multiple_of(row, 8), 8)], zsem.at[1])

        def per_expert(e, totals):
            first = tab_ref[0, e] + tab_ref[2, e]
            end = tab_ref[0, e] + tab_ref[1, e]
            aligned = jnp.minimum((first + 7) // 8 * 8, end)
            n_oct = (end - aligned) // 8

            def one(r, carry):
                row_copy(first + r).start()
                return carry

            def octet(r, carry):
                oct_copy(aligned + 8 * r).start()
                return carry

            lax.fori_loop(0, aligned - first, one, 0)
            lax.fori_loop(0, n_oct, octet, 0)
            return totals[0] + aligned - first, totals[1] + n_oct

        n_row, n_oct = lax.fori_loop(0, N_EXPERTS, per_expert, (jnp.int32(0), jnp.int32(0)))

        def drain_rows(r, carry):
            row_copy(0).wait()
            return carry

        def drain_octs(r, carry):
            oct_copy(0).wait()
            return carry

        lax.fori_loop(0, n_row, drain_rows, 0)
        lax.fori_loop(0, n_oct, drain_octs, 0)

    def issue(r, carry):
        for k in range(TOP_K):
            pltpu.make_async_copy(x_ref.at[pl.ds(r, 1)], xs_ref.at[pl.ds(dest_ref[r * TOP_K + k], 1)],
                                  sem).start()
        return carry

    lax.fori_loop(0, tm, issue, 0)
    for _ in range(TOP_K):
        pltpu.make_async_copy(x_ref, xs_ref.at[pl.ds(0, tm)], sem).wait()


def _moe_scatter(dest_flat, tab, x1, n_slots):
    t, d = x1.shape
    tm = _pick(t, (256, 128))
    return pl.pallas_call(
        _moe_scatter_kernel,
        out_shape=jax.ShapeDtypeStruct((n_slots, d), F32),
        grid=(t // tm,),
        in_specs=[pl.BlockSpec((tm * TOP_K,), lambda i: (i,), memory_space=pltpu.SMEM),
                  pl.BlockSpec(memory_space=pltpu.SMEM),
                  pl.BlockSpec((tm, d), lambda i: (i, 0))],
        out_specs=pl.BlockSpec(memory_space=pl.ANY),
        scratch_shapes=[pltpu.VMEM((8, d), F32), pltpu.SemaphoreType.DMA(()),
                        pltpu.SemaphoreType.DMA((2,))],
        compiler_params=_cparams(("arbitrary",)),
        name="moe_scatter",
    )(dest_flat, tab, x1)


def _moe_expert_kernel(unit_ref, xs_ref, wg_ref, wl_ref, bg_ref, bl_ref, wd_ref, bd_ref, ys_ref,
                       x_scr, act_scr, xst, yst1, yst2, ystg, wg_bf, wl_bf, wd_bf,
                       xsem, ysem1, ysem2, ysemg, *, ju):
    u = pl.program_id(0)
    j = pl.program_id(1)
    b0 = unit_ref[1, u]
    ns = unit_ref[2, u]
    rb = MOE_RB

    def x_copy(s, slot):
        return pltpu.make_async_copy(xs_ref.at[pl.ds((b0 + s) * rb, rb)], xst.at[slot], xsem.at[slot])

    def rows_of(s, n=1):
        return pl.ds(s * rb, n * rb) if isinstance(s, int) else pl.ds(pl.multiple_of(s * rb, rb), n * rb)

    def up_block(s, n=1):
        xb = x_scr[rows_of(s, n), :]
        hg = jnp.dot(xb, wg_bf[...], preferred_element_type=F32) + bg_ref[...]
        hl = jnp.dot(xb, wl_bf[...], preferred_element_type=F32) + bl_ref[...]
        glu = jnp.minimum(hg, SWIGLU_LIMIT)
        lin = jnp.clip(hl, -SWIGLU_LIMIT, SWIGLU_LIMIT)
        act = glu * _sigmoid(SWIGLU_ALPHA * glu) * (lin + 1.0)
        act_scr[rows_of(s, n), pl.ds(pl.multiple_of(j * MOE_TF, MOE_TF), MOE_TF)] = act.astype(BF16)

    n_grp = (ns - 1) // MOE_GROUP
    rem = (ns - 1) % MOE_GROUP
    has2 = rem >= 2
    has1 = rem % 2 == 1
    tail2_at = 1 + MOE_GROUP * n_grp
    tail1_at = tail2_at + jnp.where(has2, 2, 0)

    def cast_up_weights():
        wg_bf[...] = wg_ref[...].astype(BF16)
        wl_bf[...] = wl_ref[...].astype(BF16)

    @pl.when(jnp.logical_and(ns > 0, j == 0))
    def _():
        x_copy(0, 0).start()
        cast_up_weights()

        @pl.when(ns > 1)
        def _():
            x_copy(1, 1).start()

        x_copy(0, 0).wait()
        x_scr[rows_of(0), :] = xst[0].astype(BF16)
        up_block(0)

        def body(s, carry):
            slot = s % 2

            @pl.when(s + 1 < ns)
            def _():
                x_copy(s + 1, 1 - slot).start()

            x_copy(s, slot).wait()
            x_scr[rows_of(s), :] = xst[slot].astype(BF16)
            up_block(s)
            return carry

        lax.fori_loop(1, ns, body, 0)

    @pl.when(jnp.logical_and(ns > 0, jnp.logical_and(j > 0, j < ju)))
    def _():
        cast_up_weights()
        up_block(0)

        def body(p, carry):
            up_block(1 + MOE_GROUP * p, MOE_GROUP)
            return carry

        lax.fori_loop(0, n_grp, body, 0)

        @pl.when(has2)
        def _():
            up_block(tail2_at, 2)

        @pl.when(has1)
        def _():
            up_block(tail1_at)

    @pl.when(jnp.logical_and(ns > 0, j >= ju))
    def _():
        col = pl.multiple_of((j - ju) * MOE_TN, MOE_TN)

        def y_copy(stage, sem, s, n):
            return pltpu.make_async_copy(
                stage, ys_ref.at[pl.ds((b0 + s) * rb, n * rb), pl.ds(col, MOE_TN)], sem)

        def yg_copy(p, slot):
            return y_copy(ystg.at[slot], ysemg.at[slot], 1 + MOE_GROUP * p, MOE_GROUP)

        def down(s, n=1):
            return jnp.dot(act_scr[rows_of(s, n), :], wd_bf[...], preferred_element_type=F32) + bd_ref[...]

        wd_bf[...] = wd_ref[...].astype(BF16)
        yst1[0] = down(0)
        y_copy(yst1.at[0], ysem1.at[0], 0, 1).start()

        def body(p, carry):
            slot = p % 2

            @pl.when(p >= 2)
            def _():
                yg_copy(p - 2, slot).wait()

            ystg[slot] = down(1 + MOE_GROUP * p, MOE_GROUP)
            yg_copy(p, slot).start()
            return carry

        lax.fori_loop(0, n_grp, body, 0)

        @pl.when(has2)
        def _():
            yst2[...] = down(tail2_at, 2)
            y_copy(yst2, ysem2.at[0], tail2_at, 2).start()

        @pl.when(has1)
        def _():
            yst1[1] = down(tail1_at)
            y_copy(yst1.at[1], ysem1.at[1], tail1_at, 1).start()

        @pl.when(n_grp >= 2)
        def _():
            yg_copy(n_grp - 2, n_grp % 2).wait()

        @pl.when(n_grp >= 1)
        def _():
            yg_copy(n_grp - 1, (n_grp - 1) % 2).wait()

        y_copy(yst1.at[0], ysem1.at[0], 0, 1).wait()

        @pl.when(has2)
        def _():
            y_copy(yst2, ysem2.at[0], tail2_at, 2).wait()

        @pl.when(has1)
        def _():
            y_copy(yst1.at[1], ysem1.at[1], tail1_at, 1).wait()


def _moe_experts(units, xs, w_up, b_up, w_down, b_down):
    n_slots, d = xs.shape
    n_units = units.shape[1]
    dff = w_down.shape[1]
    ju, jd = dff // MOE_TF, d // MOE_TN

    def up_idx(off):
        return lambda u, j, un: (un[0, u], 0, off + jnp.where(un[2, u] > 0, jnp.minimum(j, ju - 1), ju - 1))

    def dn_idx(u, j, un):
        return (un[0, u], 0, jnp.where(un[2, u] > 0, jnp.maximum(j - ju, 0), jd - 1))

    return pl.pallas_call(
        functools.partial(_moe_expert_kernel, ju=ju),
        out_shape=jax.ShapeDtypeStruct((n_slots, d), F32),
        grid_spec=pltpu.PrefetchScalarGridSpec(
            num_scalar_prefetch=1, grid=(n_units, ju + jd),
            in_specs=[pl.BlockSpec(memory_space=pl.ANY),
                      pl.BlockSpec((None, d, MOE_TF), up_idx(0)),
                      pl.BlockSpec((None, d, MOE_TF), up_idx(ju)),
                      pl.BlockSpec((None, 1, MOE_TF), up_idx(0)),
                      pl.BlockSpec((None, 1, MOE_TF), up_idx(ju)),
                      pl.BlockSpec((None, dff, MOE_TN), dn_idx),
                      pl.BlockSpec((None, 1, MOE_TN), dn_idx)],
            out_specs=pl.BlockSpec(memory_space=pl.ANY),
            scratch_shapes=[pltpu.VMEM((MOE_RC, d), BF16), pltpu.VMEM((MOE_RC, dff), BF16),
                            pltpu.VMEM((2, MOE_RB, d), F32), pltpu.VMEM((2, MOE_RB, MOE_TN), F32),
                            pltpu.VMEM((2 * MOE_RB, MOE_TN), F32),
                            pltpu.VMEM((2, MOE_GROUP * MOE_RB, MOE_TN), F32),
                            pltpu.VMEM((d, MOE_TF), BF16), pltpu.VMEM((d, MOE_TF), BF16),
                            pltpu.VMEM((dff, MOE_TN), BF16),
                            pltpu.SemaphoreType.DMA((2,)), pltpu.SemaphoreType.DMA((2,)),
                            pltpu.SemaphoreType.DMA((1,)), pltpu.SemaphoreType.DMA((2,))]),
        compiler_params=_cparams(("arbitrary", "arbitrary")),
        name="moe_experts",
    )(units, xs, w_up, w_up, b_up, b_up, w_down, b_down)


COMBINE_SUB_ROWS = 128


def _moe_combine_kernel(dest_ref, g4_ref, x1_ref, ys_ref, p_ref, wg_ref, wp_ref, g_ref, b_ref,
                        y_ref, gbuf, sem, *, alpha):
    tm = x1_ref.shape[0]

    def issue(r, carry):
        for k in range(TOP_K):
            pltpu.make_async_copy(ys_ref.at[pl.ds(dest_ref[r * TOP_K + k], 1)],
                                  gbuf.at[k, pl.ds(r, 1)], sem).start()
        return carry

    lax.fori_loop(0, tm, issue, 0)
    for k in range(TOP_K):
        pltpu.make_async_copy(ys_ref.at[pl.ds(0, tm)], gbuf.at[k], sem).wait()
    sub = min(tm, COMBINE_SUB_ROWS)
    for r in range(0, tm, sub):
        rows = slice(r, r + sub)
        g4 = g4_ref[rows, :]
        moe = g4[:, 0:1] * gbuf[0, rows, :]
        for k in range(1, TOP_K):
            moe = moe + g4[:, k:k + 1] * gbuf[k, rows, :]
        x2 = _layer_norm(alpha * x1_ref[rows, :] + moe, g_ref[...], b_ref[...])
        gate = _sigmoid(jnp.dot(x2.astype(BF16), wg_ref[...], preferred_element_type=F32))
        y_ref[rows, :] = x2 + gate * jnp.dot(p_ref[rows, :].astype(BF16), wp_ref[...],
                                             preferred_element_type=F32)


def _moe_combine(dest_flat, g4, x1, ys, p, w_gate, w_proj, g, b, alpha, row0, t):
    d = x1.shape[1]
    tm = _pick(t, (2 * COMBINE_SUB_ROWS, COMBINE_SUB_ROWS))
    assert row0 % tm == 0
    r0 = row0 // tm
    pd = p.shape[1]
    return pl.pallas_call(
        functools.partial(_moe_combine_kernel, alpha=alpha),
        out_shape=jax.ShapeDtypeStruct((t, d), F32),
        grid=(t // tm,),
        in_specs=[pl.BlockSpec((tm * TOP_K,), lambda i: (r0 + i,), memory_space=pltpu.SMEM),
                  pl.BlockSpec((tm, LANES), lambda i: (r0 + i, 0)),
                  pl.BlockSpec((tm, d), lambda i: (r0 + i, 0)),
                  pl.BlockSpec(memory_space=pl.ANY),
                  pl.BlockSpec((tm, pd), lambda i: (i, 0)),
                  _const_spec((d, d)), _const_spec((pd, d)), _const_spec((1, d)), _const_spec((1, d))],
        out_specs=pl.BlockSpec((tm, d), lambda i: (i, 0)),
        scratch_shapes=[pltpu.VMEM((TOP_K, tm, d), F32), pltpu.SemaphoreType.DMA(())],
        compiler_params=_cparams(("arbitrary",)),
        name="moe_combine",
    )(dest_flat, g4, x1, ys, p, w_gate, w_proj, g, b)


def _project_all(x, w, pos, prompt):
    t128, t64, tkw = _rope_tables(pos)
    xb = x.astype(BF16)
    dq = N_HEADS * HEAD_DIM
    dkv = N_KV * HEAD_DIM
    dqi = IDX_HEADS * IDX_DIM
    d = x.shape[1]
    (q,) = _proj(xb, w, 0, dq, t128, "rope128", ((BF16, None),), scale=LOG2_E * HEAD_DIM ** -0.5)
    k32, kb = _proj(xb, w, dq, dkv, t128, "rope128",
                    ((F32, None), (BF16, "transposed" if prompt else None)))
    v32, vb = _proj(xb, w, dq + dkv, dkv, (), "plain",
                    ((F32, None), (BF16, "with_ones" if prompt else None)))
    (qi,) = _proj(xb, w, dq + 2 * dkv, dqi, t64, "rope64", ((BF16, None),))
    c0 = dq + 2 * dkv + dqi
    (kw,) = _proj(xb, w, c0, LANES, tkw, "rope64", ((F32, None),))
    (zr,) = _proj(xb, w, c0, 4 * d, (), "plain", ((F32, None),), shift=IDX_DIM + IDX_HEADS)
    return q, k32, kb, v32, vb, qi, zr, kw


def kernel(x_prompt, x_sample, cache_k, cache_v, cache_kidx, state_h, state_conv, page_table, p_prompt, p_sample, w_in, conv_w, conv_b, rg_wa, rg_ba, rg_wx, rg_bx, rg_lambda, w_o_rnn, w_o_att, w_out, ln1_g, ln1_b, w_router, b_router, w_up, b_up, w_down, b_down, ln2_g, ln2_b, w_ple_gate, w_ple_proj):
    bsz, s, d = x_prompt.shape
    n, dec_t, _ = x_sample.shape
    assert bsz == 1 and dec_t == 1 and s % LANES == 0 and n % LANES == 0
    depth = w_in.shape[0]
    npg = page_table.shape[1]
    n_phys, page = cache_k.shape[1], cache_k.shape[2]
    past = npg * page
    alpha = (2 * depth) ** 0.25
    dq, dkv, dqi = N_HEADS * HEAD_DIM, N_KV * HEAD_DIM, IDX_HEADS * IDX_DIM
    t_all = s + n
    n_slots, n_units = _moe_sizes(t_all)
    pos_p = jnp.arange(s, dtype=I32)
    pos_s = jnp.full((n,), past, I32)
    row = lambda a: a[None, :]

    hp, hs = x_prompt[0], x_sample[:, 0]
    st_p = [[], [], [], [], []]
    st_s = [[], [], [], [], []]
    for i in range(depth):
        wa, wx = rg_wa[i].astype(BF16), rg_wx[i].astype(BF16)
        rnn_w = (conv_w[i], row(conv_b[i]), wa, wx, row(rg_ba[i]), row(rg_bx[i]), row(rg_lambda[i]))
        wor, woa, wo = w_o_rnn[i].astype(BF16), w_o_att[i].astype(BF16), w_out[i].astype(BF16)
        wr = jnp.pad(w_router[i], ((0, 0), (0, LANES - N_EXPERTS)))
        wr_hi = wr.astype(BF16)
        wr = jnp.concatenate([wr_hi, (wr - wr_hi.astype(F32)).astype(BF16)], axis=1)
        br = jnp.pad(b_router[i], (0, LANES - N_EXPERTS))[None, :]
        ln1 = (row(ln1_g[i]), row(ln1_b[i]))

        w_in_t = jnp.swapaxes(w_in[i], 0, 1)
        q, k32, kt, v32, vx, qi, zr, kw = _project_all(hp, w_in_t, pos_p, True)
        y_rnn, h_p = _rnn_prompt(zr, s, *rnn_w)
        kit = kw[:, :IDX_DIM].T.astype(BF16)
        y_att = _attn_prompt(qi, kw, q, jnp.concatenate([kit, kit], 0), kt, vx, s,
                             min(TOPK_MAX, s // 4))
        m = _merge(y_rnn, y_att, wor, woa, zr)
        bufs = _ln1_router(m, hp, wo, *ln1, wr, br, alpha, 0, t_all)
        st = (k32.reshape(1, s, N_KV, HEAD_DIM), v32.reshape(1, s, N_KV, HEAD_DIM),
              kw[None, :, :IDX_DIM], h_p, zr[None, s - (CONV_W - 1):s, :d])
        for lst, a in zip(st_p, st):
            lst.append(a)

        q, k32, kb, v32, vb, qi, zr, kw = _project_all(hs, w_in_t, pos_s, False)
        y_rnn, h_s = _rnn_sample(zr, state_conv[i], state_h[i], *rnn_w)
        sc = _idx_sample(page_table, qi.reshape(n, IDX_HEADS, IDX_DIM),
                         kw[:, IDX_DIM:IDX_DIM + IDX_HEADS, None], kw[:, None, :],
                         jnp.swapaxes(cache_kidx[i], 1, 2))
        mask = _mask_sample(sc[:, 0], past + 1, min(TOPK_MAX, (past + 1) // 4))
        y_att = _attn_sample(page_table, q.reshape(n, N_HEADS, HEAD_DIM),
                             jnp.repeat(mask[:, :past], N_KV, axis=1)[:, None, :],
                             mask[:, None, past:past + LANES],
                             kb.reshape(n, N_KV, HEAD_DIM), vb.reshape(n, N_KV, HEAD_DIM),
                             cache_k[i].reshape(n_phys, page * N_KV, HEAD_DIM),
                             cache_v[i].reshape(n_phys, page * N_KV, HEAD_DIM)).reshape(n, dq)
        m = _merge(y_rnn, y_att, wor, woa, zr)
        x1, sel, gate = _ln1_router(m, hs, wo, *ln1, wr, br, alpha, s, t_all, prev=bufs)
        st = (k32.reshape(n, 1, N_KV, HEAD_DIM), v32.reshape(n, 1, N_KV, HEAD_DIM),
              kw[:, None, :IDX_DIM], h_s,
              jnp.concatenate([state_conv[i][:, 1:], zr[:, None, :d]], axis=1))
        for lst, a in zip(st_s, st):
            lst.append(a)

        rank, cnt = _moe_rank(sel)
        dest4, g4, tab = _moe_dest(sel, gate, rank, cnt)
        units = _moe_units(tab, n_units)
        dest_flat = dest4[:, :TOP_K].reshape(-1)
        xs = _moe_scatter(dest_flat, tab, x1, n_slots)
        ys = _moe_experts(units, xs, w_up[i], b_up[i][:, None, :], w_down[i], b_down[i][:, None, :])
        tail = (w_ple_gate[i].astype(BF16), w_ple_proj[i].astype(BF16), row(ln2_g[i]), row(ln2_b[i]),
                alpha)
        hp = _moe_combine(dest_flat, g4, x1, ys, p_prompt[i][0], *tail, 0, s)
        hs = _moe_combine(dest_flat, g4, x1, ys, p_sample[i][:, 0], *tail, s, n)

    outs_p = [jnp.stack(l) for l in st_p]
    outs_s = [jnp.stack(l) for l in st_s]
    return (hp[None], hs[:, None], *outs_p, *outs_s)
```

```python
import functools

import jax
import jax.numpy as jnp
from jax import lax
from jax.experimental import pallas as pl
from jax.experimental.pallas import tpu as pltpu

F32 = jnp.float32
BF16 = jnp.bfloat16
I32 = jnp.int32

N_HEADS = 16
HEAD_DIM = 128
N_KV = 4
Q_PER_KV = N_HEADS // N_KV
IDX_HEADS = 16
IDX_DIM = 64
TOPK_MAX = 256
Q_BLOCK = 128
ROPE_THETA = 10000.0
RNN_BW = 128
CONV_W = 4
RG_C = 8.0
N_EXPERTS = 32
TOP_K = 4
SWIGLU_LIMIT = 7.0
SWIGLU_ALPHA = 1.702
LN_EPS = 1e-5
LANES = 128
INT_MIN = -(2 ** 31)
NEG_BIG = -1e30
LOG2_E = 1.4426950408889634
VMEM_LIMIT = 56 * 1024 * 1024


def _pick(n, cands):
    for c in cands:
        if n % c == 0:
            return c
    return n


def _cparams(sem):
    return pltpu.CompilerParams(dimension_semantics=sem, vmem_limit_bytes=VMEM_LIMIT)


def _const_spec(shape):
    nd = len(shape)
    return pl.BlockSpec(shape, lambda *a: (0,) * nd, pipeline_mode=pl.Buffered(1))


def _proj_kernel(*refs, mode, kinds, shift, n_tab, scale):
    n_w = 2 if shift else 1
    x_ref, w_refs = refs[0], refs[1:1 + n_w]
    tabs = refs[1 + n_w:1 + n_w + n_tab]
    outs = refs[1 + n_w + n_tab:1 + n_w + n_tab + len(kinds)]
    w_bf = refs[-1]
    tn = w_bf.shape[1]

    @pl.when(pl.program_id(1) == 0)
    def _():
        for c in range(0, tn, LANES):
            lo = c + shift
            if lo + LANES <= tn:
                rows = w_refs[0][lo:lo + LANES, :]
            elif lo >= tn:
                rows = w_refs[1][lo - tn:lo - tn + LANES, :]
            else:
                rows = jnp.concatenate([w_refs[0][lo:tn, :], w_refs[1][0:lo + LANES - tn, :]], axis=0)
            w_bf[:, c:c + LANES] = rows.T.astype(BF16)

    z = jnp.dot(x_ref[...], w_bf[...], preferred_element_type=F32)
    for h in range(tn // LANES):
        zh = z[:, h * LANES:(h + 1) * LANES]
        if mode == "plain":
            r = zh
        elif mode == "rope128":
            r = zh * tabs[0][...] + pltpu.roll(zh, 64, 1) * tabs[1][...]
        else:
            r = (zh * tabs[0][...] + pltpu.roll(zh, 96, 1) * tabs[1][...]
                 + pltpu.roll(zh, 32, 1) * tabs[2][...])
        if scale != 1.0:
            r = r * scale
        for o, kind in zip(outs, kinds):
            if kind == "transposed":
                o[h * LANES:(h + 1) * LANES, :] = r.T.astype(o.dtype)
            elif kind == "with_ones":
                lane = lax.broadcasted_iota(I32, r.shape, 1)
                o[:, 2 * h * LANES:(2 * h + 1) * LANES] = r.astype(o.dtype)
                o[:, (2 * h + 1) * LANES:(2 * h + 2) * LANES] = jnp.where(lane == 0, 1.0, 0.0).astype(o.dtype)
            else:
                o[:, h * LANES:(h + 1) * LANES] = r.astype(o.dtype)


def _proj(x, wt, col0, ncols, tabs, mode, outs, scale=1.0, shift=0):
    t, k = x.shape
    tm = _pick(t, (1024, 512, 256, 128))
    tn = _pick(ncols, (1024, 512, 256, 128))
    assert col0 % tn == 0 and shift % 8 == 0 and shift < LANES
    c0 = col0 // tn
    in_specs = [pl.BlockSpec((tm, k), lambda j, i: (i, 0)),
                pl.BlockSpec((tn, k), lambda j, i: (c0 + j, 0))]
    ws = [wt]
    if shift:
        in_specs.append(pl.BlockSpec((LANES, k), lambda j, i: ((c0 + j + 1) * (tn // LANES), 0)))
        ws.append(wt)
    in_specs += [pl.BlockSpec((tm, LANES), lambda j, i: (i, 0)) for _ in tabs]
    shapes, specs = [], []
    for dt, kind in outs:
        if kind == "transposed":
            shapes.append(jax.ShapeDtypeStruct((ncols, t), dt))
            specs.append(pl.BlockSpec((tn, tm), lambda j, i: (j, i)))
        elif kind == "with_ones":
            shapes.append(jax.ShapeDtypeStruct((t, 2 * ncols), dt))
            specs.append(pl.BlockSpec((tm, 2 * tn), lambda j, i: (i, j)))
        else:
            shapes.append(jax.ShapeDtypeStruct((t, ncols), dt))
            specs.append(pl.BlockSpec((tm, tn), lambda j, i: (i, j)))
    return pl.pallas_call(
        functools.partial(_proj_kernel, mode=mode, kinds=tuple(kd for _, kd in outs), shift=shift,
                          n_tab=len(tabs), scale=scale),
        out_shape=shapes,
        grid=(ncols // tn, t // tm),
        in_specs=in_specs,
        out_specs=specs,
        scratch_shapes=[pltpu.VMEM((k, tn), BF16)],
        compiler_params=_cparams(("parallel", "arbitrary")),
        name="proj_" + mode,
    )(x, *ws, *tabs)


def _rope_tables(pos):
    posf = pos.astype(F32)[:, None]
    h128 = HEAD_DIM // 2
    inv = ROPE_THETA ** (-jnp.arange(h128, dtype=F32) / h128)
    c, s = jnp.cos(posf * inv), jnp.sin(posf * inv)
    t128 = (jnp.concatenate([c, c], 1), jnp.concatenate([-s, s], 1))
    h64 = IDX_DIM // 2
    inv = ROPE_THETA ** (-jnp.arange(h64, dtype=F32) / h64)
    c, s = jnp.cos(posf * inv), jnp.sin(posf * inv)
    z = jnp.zeros_like(s)
    c64 = jnp.concatenate([c, c], 1)
    sa64 = jnp.concatenate([-s, z], 1)
    sb64 = jnp.concatenate([z, s], 1)
    t64 = tuple(jnp.concatenate([a, a], 1) for a in (c64, sa64, sb64))
    idx_scale = (IDX_HEADS * IDX_DIM) ** -0.5
    n = pos.shape[0]
    ck = jnp.concatenate([c64, jnp.full((n, IDX_HEADS), idx_scale, F32),
                          jnp.zeros((n, LANES - IDX_DIM - IDX_HEADS), F32)], 1)
    z64 = jnp.zeros((n, LANES - IDX_DIM), F32)
    tkw = (ck, jnp.concatenate([sa64, z64], 1), jnp.concatenate([sb64, z64], 1))
    return t128, t64, tkw


def _sigmoid(x):
    return 0.5 * jnp.tanh(0.5 * x) + 0.5


def _softplus_neg(lam):
    return jnp.maximum(-lam, 0.0) + jnp.log1p(jnp.exp(-jnp.abs(lam)))


def _gelu_tanh(x):
    return 0.5 * x * (1.0 + jnp.tanh(0.7978845608028654 * (x + 0.044715 * (x * x * x))))


def _rglru_gates(xc, wa_ref, wx_ref, ba, bx, lam):
    nblk = xc.shape[1] // RNN_BW
    rs, gs = [], []
    for j in range(nblk):
        xj = xc[:, j * RNN_BW:(j + 1) * RNN_BW].astype(BF16)
        rs.append(jnp.dot(xj, wa_ref[j], preferred_element_type=F32))
        gs.append(jnp.dot(xj, wx_ref[j], preferred_element_type=F32))
    r = _sigmoid(jnp.concatenate(rs, 1) + ba)
    g = _sigmoid(jnp.concatenate(gs, 1) + bx)
    log_a = -RG_C * r * _softplus_neg(lam)
    a = jnp.exp(log_a)
    mult = jnp.sqrt(jnp.tanh(-log_a) * (1.0 + a * a))
    return a, mult, g * xc


def _rnn_prompt_kernel(xr_ref, gr_ref, cw_ref, cb_ref, wa_ref, wx_ref, ba_ref, bx_ref, lam_ref,
                       y_ref, hl_ref, xbuf, hcar, a_scr, b_scr, h_scr):
    t = pl.program_id(1)
    tm, cw = xr_ref.shape

    @pl.when(t == 0)
    def _():
        xbuf[0:8, :] = jnp.zeros((8, cw), F32)
        hcar[...] = jnp.zeros_like(hcar)

    x = xr_ref[...]
    xbuf[8:8 + tm, :] = x
    w = cw_ref[...]
    xc = (cb_ref[...] + w[3:4] * x + w[2:3] * xbuf[7:7 + tm, :]
          + w[1:2] * xbuf[6:6 + tm, :] + w[0:1] * xbuf[5:5 + tm, :])
    xbuf[0:8, :] = x[tm - 8:tm, :]
    a, mult, gx = _rglru_gates(xc, wa_ref, wx_ref, ba_ref[...], bx_ref[...], lam_ref[...])
    pos = t * tm + lax.broadcasted_iota(I32, (tm, cw), 0)
    mult = jnp.where(pos == 0, 1.0, mult)
    a_scr[...] = a
    b_scr[...] = mult * gx
    row8 = lax.broadcasted_iota(I32, (8, cw), 0)

    def group(g, carry):
        r0 = pl.multiple_of(g * 8, 8)
        av = a_scr[pl.ds(r0, 8), :]
        bv = b_scr[pl.ds(r0, 8), :]
        for d in (1, 2, 4):
            a_s = pltpu.roll(av, d, 0)
            b_s = pltpu.roll(bv, d, 0)
            m = row8 >= d
            bv = jnp.where(m, av * b_s + bv, bv)
            av = jnp.where(m, av * a_s, av)
        h = av * carry + bv
        h_scr[pl.ds(r0, 8), :] = h
        return h[7:8, :]

    carry = lax.fori_loop(0, tm // 8, group, hcar[0:1, :])
    hcar[0:1, :] = carry
    y_ref[...] = (_gelu_tanh(gr_ref[...]) * h_scr[...]).astype(y_ref.dtype)

    @pl.when(t == pl.num_programs(1) - 1)
    def _():
        hl_ref[...] = carry


def _rnn_prompt(zr, s, conv_w, conv_b, wa, wx, ba, bx, lam):
    d = conv_w.shape[1]
    cw = 512
    tm = _pick(s, (256, 128))
    ncb = d // cw
    nb = cw // RNN_BW
    vec = lambda: pl.BlockSpec((1, cw), lambda c, t: (0, c))
    y, hl = pl.pallas_call(
        _rnn_prompt_kernel,
        out_shape=[jax.ShapeDtypeStruct((s, d), BF16), jax.ShapeDtypeStruct((1, d), F32)],
        grid=(ncb, s // tm),
        in_specs=[pl.BlockSpec((tm, cw), lambda c, t: (t, c)),
                  pl.BlockSpec((tm, cw), lambda c, t: (t, ncb + c)),
                  pl.BlockSpec((CONV_W, cw), lambda c, t: (0, c)),
                  vec(),
                  pl.BlockSpec((nb, RNN_BW, RNN_BW), lambda c, t: (c, 0, 0)),
                  pl.BlockSpec((nb, RNN_BW, RNN_BW), lambda c, t: (c, 0, 0)),
                  vec(), vec(), vec()],
        out_specs=[pl.BlockSpec((tm, cw), lambda c, t: (t, c)),
                   pl.BlockSpec((1, cw), lambda c, t: (0, c))],
        scratch_shapes=[pltpu.VMEM((tm + 8, cw), F32), pltpu.VMEM((8, cw), F32),
                        pltpu.VMEM((tm, cw), F32), pltpu.VMEM((tm, cw), F32),
                        pltpu.VMEM((tm, cw), F32)],
        compiler_params=_cparams(("parallel", "arbitrary")),
        name="rnn_prompt",
    )(zr, zr, conv_w, conv_b, wa, wx, ba, bx, lam)
    return y, hl


def _rnn_sample_kernel(xr_ref, gr_ref, c0_ref, c1_ref, c2_ref, h0_ref, cw_ref, cb_ref,
                       wa_ref, wx_ref, ba_ref, bx_ref, lam_ref, y_ref, h_ref):
    w = cw_ref[...]
    xc = (cb_ref[...] + w[3:4] * xr_ref[...] + w[2:3] * c2_ref[...]
          + w[1:2] * c1_ref[...] + w[0:1] * c0_ref[...])
    a, mult, gx = _rglru_gates(xc, wa_ref, wx_ref, ba_ref[...], bx_ref[...], lam_ref[...])
    h = a * h0_ref[...] + mult * gx
    h_ref[...] = h
    y_ref[...] = (_gelu_tanh(gr_ref[...]) * h).astype(y_ref.dtype)


def _rnn_sample(zr, conv_state, h0, conv_w, conv_b, wa, wx, ba, bx, lam):
    n, d = h0.shape
    cw = 512
    ncb = d // cw
    nb = cw // RNN_BW
    blk = lambda off: pl.BlockSpec((n, cw), lambda c: (0, off + c))
    vec = lambda: pl.BlockSpec((1, cw), lambda c: (0, c))
    wsp = lambda: pl.BlockSpec((nb, RNN_BW, RNN_BW), lambda c: (c, 0, 0))
    return pl.pallas_call(
        _rnn_sample_kernel,
        out_shape=[jax.ShapeDtypeStruct((n, d), BF16), jax.ShapeDtypeStruct((n, d), F32)],
        grid=(ncb,),
        in_specs=[blk(0), blk(ncb), blk(0), blk(0), blk(0), blk(0),
                  pl.BlockSpec((CONV_W, cw), lambda c: (0, c)), vec(), wsp(), wsp(),
                  vec(), vec(), vec()],
        out_specs=[blk(0), blk(0)],
        compiler_params=_cparams(("parallel",)),
        name="rnn_sample",
    )(zr, zr, conv_state[:, 0], conv_state[:, 1], conv_state[:, 2], h0,
      conv_w, conv_b, wa, wx, ba, bx, lam)


def _layer_norm(x, g, b):
    mu = jnp.mean(x, axis=-1, keepdims=True)
    xc = x - mu
    var = jnp.mean(xc * xc, axis=-1, keepdims=True)
    return xc * lax.rsqrt(var + LN_EPS) * g + b


def _merge_kernel(yr_ref, ya_ref, wr_ref, wa_ref, ga_ref, gb_ref, o_ref):
    a = jnp.dot(yr_ref[...], wr_ref[...], preferred_element_type=F32)
    b = jnp.dot(ya_ref[...], wa_ref[...], preferred_element_type=F32)
    m = _sigmoid(ga_ref[...]) * a + _sigmoid(gb_ref[...]) * b
    o_ref[...] = m.astype(o_ref.dtype)


def _merge(y_rnn, y_att, w_o_rnn, w_o_att, zr):
    t, d = y_rnn.shape
    tm = _pick(t, (512, 256, 128))
    tn = 512
    nj = d // tn
    return pl.pallas_call(
        _merge_kernel,
        out_shape=jax.ShapeDtypeStruct((t, d), BF16),
        grid=(t // tm, nj),
        in_specs=[pl.BlockSpec((tm, d), lambda i, j: (i, 0)),
                  pl.BlockSpec((tm, d), lambda i, j: (i, 0)),
                  pl.BlockSpec((d, tn), lambda i, j: (0, j)),
                  pl.BlockSpec((d, tn), lambda i, j: (0, j)),
                  pl.BlockSpec((tm, tn), lambda i, j: (i, 2 * nj + j)),
                  pl.BlockSpec((tm, tn), lambda i, j: (i, 3 * nj + j))],
        out_specs=pl.BlockSpec((tm, tn), lambda i, j: (i, j)),
        compiler_params=_cparams(("parallel", "parallel")),
        name="merge",
    )(y_rnn, y_att, w_o_rnn, w_o_att, zr, zr)


LN1_SUB_ROWS = 256


def _ln1_router_kernel(*refs, alpha, n_skip):
    m_ref, x_ref, w_ref, g_ref, b_ref, wr_ref, br_ref, x1_ref, sel_ref, gate_ref = refs[n_skip:]
    tm = x_ref.shape[0]
    sub = min(tm, LN1_SUB_ROWS)
    for r in range(0, tm, sub):
        rows = slice(r, r + sub)
        y = alpha * x_ref[rows, :] + jnp.dot(m_ref[rows, :], w_ref[...], preferred_element_type=F32)
        x1 = _layer_norm(y, g_ref[...], b_ref[...])
        x1_ref[rows, :] = x1
        x_hi = x1.astype(BF16)
        x_lo = (x1 - x_hi.astype(F32)).astype(BF16)
        part = (jnp.dot(x_hi, wr_ref[...], preferred_element_type=F32)
                + jnp.dot(x_lo, wr_ref[...], preferred_element_type=F32))
        logits = part[:, :LANES] + part[:, LANES:] + br_ref[...]
        lane = lax.broadcasted_iota(I32, logits.shape, 1)
        live = lane < N_EXPERTS
        cur = jnp.where(live, logits, -jnp.inf)
        top = jnp.max(cur, axis=1, keepdims=True)
        sel = jnp.zeros(logits.shape, jnp.bool_)
        for _ in range(TOP_K):
            mx = jnp.max(cur, axis=1, keepdims=True)
            first = jnp.min(jnp.where(cur == mx, lane, LANES), axis=1, keepdims=True)
            pick = lane == first
            sel = jnp.logical_or(sel, pick)
            cur = jnp.where(pick, -jnp.inf, cur)
        e = jnp.where(sel, jnp.exp(logits - top), 0.0)
        sel_ref[rows, :] = jnp.where(sel, 1.0, 0.0)
        gate_ref[rows, :] = e / jnp.sum(e, axis=1, keepdims=True)


def _ln1_router(m, x, w_out, g, b, w_router, b_router, alpha, row0, t_all, prev=None):
    t, d = x.shape
    tm = _pick(t, (2 * LN1_SUB_ROWS, LN1_SUB_ROWS, 128))
    assert row0 % tm == 0
    r0 = row0 // tm
    in_specs = [pl.BlockSpec((tm, d), lambda i: (i, 0)),
                pl.BlockSpec((tm, d), lambda i: (i, 0)),
                _const_spec((d, d)), _const_spec((1, d)), _const_spec((1, d)),
                _const_spec((d, 2 * LANES)), _const_spec((1, LANES))]
    args = [m, x, w_out, g, b, w_router, b_router]
    aliases = {}
    if prev is not None:
        in_specs = [pl.BlockSpec(memory_space=pl.ANY)] * 3 + in_specs
        args = list(prev) + args
        aliases = {0: 0, 1: 1, 2: 2}

    return pl.pallas_call(
        functools.partial(_ln1_router_kernel, alpha=alpha, n_skip=len(aliases)),
        out_shape=[jax.ShapeDtypeStruct((t_all, d), F32),
                   jax.ShapeDtypeStruct((t_all, LANES), F32),
                   jax.ShapeDtypeStruct((t_all, LANES), F32)],
        grid=(t // tm,),
        in_specs=in_specs,
        out_specs=[pl.BlockSpec((tm, d), lambda i: (r0 + i, 0)),
                   pl.BlockSpec((tm, LANES), lambda i: (r0 + i, 0)),
                   pl.BlockSpec((tm, LANES), lambda i: (r0 + i, 0))],
        input_output_aliases=aliases,
        compiler_params=_cparams(("parallel",)),
        name="ln1_router",
    )(*args)


def _sort_key(x):
    bits = lax.bitcast_convert_type(x, I32)
    return bits ^ (jnp.right_shift(bits, 31) & 0x7FFFFFFF)


def _count(keys_ref, nch, cw, tvec, strict):
    rows = keys_ref.shape[0]
    tb = jnp.broadcast_to(tvec, (rows, LANES))

    def body(c, cnt):
        off = c * cw
        for s in range(cw // LANES):
            k = keys_ref[:, pl.ds(pl.multiple_of(off + s * LANES, LANES), LANES)]
            hit = (k > tb) if strict else (k >= tb)
            cnt = cnt + jnp.where(hit, 1.0, 0.0)
        return cnt

    cnt = lax.fori_loop(0, nch, body, jnp.zeros((rows, LANES), F32))
    return jnp.sum(cnt, axis=1, keepdims=True)


def _select_threshold(keys_ref, nch, cw, n_sel, few):
    rows = keys_ref.shape[0]
    settled = few > 0.5

    def cond(st):
        b, _, cnt_cur = st
        open_rows = jnp.where(jnp.logical_or(settled, cnt_cur == n_sel), 0.0, 1.0)
        return jnp.logical_and(b < 32, jnp.max(open_rows) > 0.0)

    def bit_step(st):
        b, cur, cnt_cur = st
        cand = cur | lax.shift_left(jnp.int32(1), 31 - b)
        cnt = _count(keys_ref, nch, cw, cand ^ INT_MIN, False)
        take = cnt >= n_sel
        return b + 1, jnp.where(take, cand, cur), jnp.where(take, cnt, cnt_cur)

    everything = jnp.zeros((rows, 1), F32) + jnp.asarray(nch * cw, F32)
    _, cur, n_ge = lax.while_loop(cond, bit_step, (jnp.int32(0), jnp.zeros((rows, 1), I32), everything))
    t = cur ^ INT_MIN
    tied = jnp.where(jnp.logical_or(settled, n_ge <= n_sel), 0.0, 1.0)

    @pl.when(jnp.max(tied) > 0.0)
    def _():
        n_gt = _count(keys_ref, nch, cw, t, True)
        tb = jnp.broadcast_to(t, (rows, LANES))
        needb = jnp.broadcast_to(jnp.where(tied > 0.5, n_sel - n_gt, 1e9), (rows, LANES))
        r = lax.broadcasted_iota(I32, (LANES, LANES), 0)
        c = lax.broadcasted_iota(I32, (LANES, LANES), 1)
        upper = jnp.where(r < c, 1.0, 0.0).astype(BF16)

        def fix(j, run):
            sl = pl.ds(pl.multiple_of(j * LANES, LANES), LANES)
            k = keys_ref[:, sl]
            eq = k == tb
            eqf = jnp.where(eq, 1.0, 0.0)
            before = jnp.dot(eqf.astype(BF16), upper, preferred_element_type=F32) + run
            drop = jnp.logical_and(eq, before >= needb)
            keys_ref[:, sl] = jnp.where(drop, INT_MIN, k)
            return run + jnp.sum(eqf, axis=1, keepdims=True)

        lax.fori_loop(0, nch * (cw // LANES), fix, jnp.zeros((rows, 1), F32))

    return jnp.maximum(t, INT_MIN + 1)


def _attn_prompt_kernel(qi_ref, kw_ref, q_ref, kit_ref, kt_ref, vx_ref, o_ref,
                        keys_scr, lhs_scr, wb_scr, tb_scr, qg_scr, s_scr, p_scr, m_scr, acc_scr,
                        *, n_sel, kc):
    i = pl.program_id(0)
    qb = Q_BLOCK
    nch = (i * qb + qb + kc - 1) // kc
    kw = kw_ref[...]
    lane = lax.broadcasted_iota(I32, (qb, LANES), 1)
    for h in range(IDX_HEADS):
        wb_scr[h] = jnp.broadcast_to(kw[:, IDX_DIM + h:IDX_DIM + h + 1], (qb, LANES))
        blk = qi_ref[:, (h // 2) * LANES:(h // 2 + 1) * LANES]
        keep = (lane < IDX_DIM) if h % 2 == 0 else (lane >= IDX_DIM)
        lhs_scr[h] = jnp.where(keep, blk, jnp.zeros_like(blk))

    sw = min(256, kc)

    def score_chunk(c, carry):
        for s in range(kc // sw):
            o2 = pl.multiple_of(c * kc + s * sw, sw)
            kt = kit_ref[:, pl.ds(o2, sw)]
            acc = jnp.zeros((qb, sw), F32)
            for h in range(IDX_HEADS):
                sc = jnp.dot(lhs_scr[h], kt, preferred_element_type=F32)
                wb = wb_scr[h]
                acc = acc + jnp.maximum(sc, 0.0) * jnp.concatenate([wb] * (sw // LANES), axis=1)
            kpos = o2 + lax.broadcasted_iota(I32, (qb, sw), 1)
            qpos = i * qb + lax.broadcasted_iota(I32, (qb, sw), 0)
            keys_scr[:, pl.ds(o2, sw)] = jnp.where(kpos <= qpos, _sort_key(acc), INT_MIN)
        return carry

    lax.fori_loop(0, nch, score_chunk, 0)
    n_causal = i * qb + lax.broadcasted_iota(I32, (qb, 1), 0) + 1
    t = _select_threshold(keys_scr, nch, kc, n_sel, jnp.where(n_causal <= n_sel, 1.0, 0.0))
    rows = Q_PER_KV * qb
    strip = 32
    nrep = kc // LANES
    tb_scr[...] = jnp.broadcast_to(t, (qb, LANES))

    for g in range(N_KV):
        for j in range(Q_PER_KV):
            h = g * Q_PER_KV + j
            qg_scr[g, j * qb:(j + 1) * qb, :] = q_ref[:, h * HEAD_DIM:(h + 1) * HEAD_DIM]
    m_scr[...] = jnp.full(m_scr.shape, NEG_BIG, F32)
    acc_scr[...] = jnp.zeros_like(acc_scr)

    def body(c, carry):
        off = pl.multiple_of(c * kc, kc)
        for g in range(N_KV):
            s_scr[g] = jnp.dot(qg_scr[g], kt_ref[g * HEAD_DIM:(g + 1) * HEAD_DIM, pl.ds(off, kc)],
                               preferred_element_type=F32)
            for rq in range(0, qb, strip):
                tb = jnp.concatenate([tb_scr[rq:rq + strip, :]] * nrep, axis=1)
                for j in range(Q_PER_KV):
                    r = j * qb + rq
                    km = keys_scr[rq:rq + strip, pl.ds(off, kc)] >= tb
                    s = jnp.where(km, s_scr[g, r:r + strip, :], NEG_BIG)
                    s_scr[g, r:r + strip, :] = s
                    m_old = m_scr[g, r:r + strip, :]
                    m_new = jnp.maximum(m_old, jnp.max(s, axis=1, keepdims=True))
                    alpha = jnp.exp2(m_old - m_new)
                    m_scr[g, r:r + strip, :] = m_new
                    acc_scr[g, r:r + strip, :] = (acc_scr[g, r:r + strip, :]
                                                  * jnp.concatenate([alpha] * 2, axis=1))
            for r in range(0, rows, strip):
                m_new = m_scr[g, r:r + strip, :]
                p = jnp.exp2(s_scr[g, r:r + strip, :] - jnp.concatenate([m_new] * nrep, axis=1))
                p_scr[g, r:r + strip, :] = p.astype(BF16)
            acc_scr[g] += jnp.dot(p_scr[g], vx_ref[pl.ds(off, kc), g * 2 * HEAD_DIM:(g + 1) * 2 * HEAD_DIM],
                                  preferred_element_type=F32)
        return carry

    def body_pair(c2, carry):
        body(2 * c2, carry)
        return body(2 * c2 + 1, carry)

    lax.fori_loop(0, nch // 2, body_pair, 0)

    @pl.when(nch % 2 == 1)
    def _():
        body(nch - 1, 0)

    for g in range(N_KV):
        acc = acc_scr[g]
        out = acc[:, :HEAD_DIM] / acc[:, HEAD_DIM:HEAD_DIM + 1]
        for j in range(Q_PER_KV):
            h = g * Q_PER_KV + j
            o_ref[:, h * HEAD_DIM:(h + 1) * HEAD_DIM] = out[j * qb:(j + 1) * qb].astype(o_ref.dtype)


def _attn_prompt(qi, kw, q, kit2, kt, vx, s, n_sel):
    kc = min(512, s)
    d = N_HEADS * HEAD_DIM
    rows = Q_PER_KV * Q_BLOCK
    return pl.pallas_call(
        functools.partial(_attn_prompt_kernel, n_sel=n_sel, kc=kc),
        out_shape=jax.ShapeDtypeStruct((s, d), BF16),
        grid=(s // Q_BLOCK,),
        in_specs=[pl.BlockSpec((Q_BLOCK, IDX_HEADS * IDX_DIM), lambda i: (i, 0)),
                  pl.BlockSpec((Q_BLOCK, LANES), lambda i: (i, 0)),
                  pl.BlockSpec((Q_BLOCK, d), lambda i: (i, 0)),
                  _const_spec((LANES, s)), _const_spec((N_KV * HEAD_DIM, s)),
                  _const_spec((s, N_KV * 2 * HEAD_DIM))],
        out_specs=pl.BlockSpec((Q_BLOCK, d), lambda i: (i, 0)),
        scratch_shapes=[pltpu.VMEM((Q_BLOCK, s), I32),
                        pltpu.VMEM((IDX_HEADS, Q_BLOCK, LANES), BF16),
                        pltpu.VMEM((IDX_HEADS, Q_BLOCK, LANES), F32),
                        pltpu.VMEM((Q_BLOCK, LANES), I32),
                        pltpu.VMEM((N_KV, rows, HEAD_DIM), BF16),
                        pltpu.VMEM((N_KV, rows, kc), F32),
                        pltpu.VMEM((N_KV, rows, kc), BF16),
                        pltpu.VMEM((N_KV, rows, LANES), F32),
                        pltpu.VMEM((N_KV, rows, 2 * HEAD_DIM), F32)],
        compiler_params=_cparams(("parallel",)),
        name="attn_prompt",
    )(qi, kw, q, kit2, kt, vx)


IDX_SEQS_PER_STEP = 4


def _idx_sample_kernel(pt_ref, qi_ref, w_ref, kn_ref, *refs, npg, past, group):
    pages, o_ref = refs[:group * npg], refs[group * npg]
    lane = lax.broadcasted_iota(I32, (1, LANES), 1)
    for g in range(group):
        qi = qi_ref[g]
        kp = jnp.concatenate([p[...] for p in pages[g * npg:(g + 1) * npg]], axis=1).astype(BF16)
        sc = jnp.dot(qi, kp, preferred_element_type=F32)
        w = w_ref[g]
        o_ref[g, :, 0:past] = jnp.sum(jnp.maximum(sc, 0.0) * w, axis=0, keepdims=True)
        kn = kn_ref[g][:, :IDX_DIM].astype(BF16).astype(F32)
        sn = jnp.sum(qi.astype(F32) * kn, axis=1, keepdims=True)
        new = jnp.sum(jnp.maximum(sn, 0.0) * w, axis=0, keepdims=True)
        o_ref[g, :, past:past + LANES] = jnp.where(lane == 0, new, -jnp.inf)


def _idx_sample(page_table, qi3, w3, kn3, cache_kidx_t):
    n, npg = page_table.shape
    page = cache_kidx_t.shape[2]
    past = npg * page
    group = IDX_SEQS_PER_STEP if n % IDX_SEQS_PER_STEP == 0 else 1
    page_specs = [pl.BlockSpec((None, IDX_DIM, page),
                               lambda b, pt, g=g, p=p: (pt[(b * group + g) * npg + p], 0, 0))
                  for g in range(group) for p in range(npg)]
    return pl.pallas_call(
        functools.partial(_idx_sample_kernel, npg=npg, past=past, group=group),
        out_shape=jax.ShapeDtypeStruct((n, 1, past + LANES), F32),
        grid_spec=pltpu.PrefetchScalarGridSpec(
            num_scalar_prefetch=1, grid=(n // group,),
            in_specs=[pl.BlockSpec((group, IDX_HEADS, IDX_DIM), lambda b, pt: (b, 0, 0)),
                      pl.BlockSpec((group, IDX_HEADS, 1), lambda b, pt: (b, 0, 0)),
                      pl.BlockSpec((group, 1, LANES), lambda b, pt: (b, 0, 0))] + page_specs,
            out_specs=pl.BlockSpec((group, 1, past + LANES), lambda b, pt: (b, 0, 0))),
        compiler_params=_cparams(("parallel",)),
        name="idx_sample",
    )(page_table.reshape(-1), qi3, w3, kn3, *([cache_kidx_t] * (group * npg)))


def _mask_sample_kernel(sc_ref, m_ref, keys_scr, *, n_valid, n_sel):
    rows, width = sc_ref.shape
    pos = lax.broadcasted_iota(I32, (rows, width), 1)
    keys_scr[...] = jnp.where(pos < n_valid, _sort_key(sc_ref[...]), INT_MIN)
    few = jnp.full((rows, 1), 1.0 if n_valid <= n_sel else 0.0, F32)
    t = _select_threshold(keys_scr, width // LANES, LANES, n_sel, few)
    m_ref[...] = jnp.where(keys_scr[...] >= t, 1.0, 0.0)


def _mask_sample(sc, n_valid, n_sel):
    n, width = sc.shape
    return pl.pallas_call(
        functools.partial(_mask_sample_kernel, n_valid=n_valid, n_sel=n_sel),
        out_shape=jax.ShapeDtypeStruct((n, width), F32),
        scratch_shapes=[pltpu.VMEM((n, width), I32)],
        name="mask_sample",
    )(sc)


def _attn_sample_kernel(pt_ref, q_ref, m4_ref, mn_ref, kn_ref, vn_ref, *refs, npg):
    kpages, vpages, o_ref = refs[:npg], refs[npg:2 * npg], refs[2 * npg]
    q = q_ref[...]
    kp = jnp.concatenate([p[...] for p in kpages], axis=0).astype(BF16)
    s = lax.dot_general(q, kp, (((1,), (1,)), ((), ())), preferred_element_type=F32)
    grp = lax.broadcasted_iota(I32, s.shape, 0) // Q_PER_KV
    own = (lax.broadcasted_iota(I32, s.shape, 1) & (N_KV - 1)) == grp
    s = jnp.where(own, jnp.where(m4_ref[...] > 0.5, s, NEG_BIG), NEG_BIG)
    grp_h = lax.broadcasted_iota(I32, (N_HEADS, HEAD_DIM), 0) // Q_PER_KV
    kn = kn_ref[...].astype(F32)
    vn = vn_ref[...].astype(F32)
    kn_h = jnp.zeros((N_HEADS, HEAD_DIM), F32)
    vn_h = jnp.zeros((N_HEADS, HEAD_DIM), F32)
    for g in range(N_KV):
        kn_h = jnp.where(grp_h == g, kn[g:g + 1, :], kn_h)
        vn_h = jnp.where(grp_h == g, vn[g:g + 1, :], vn_h)
    sn = jnp.sum(q.astype(F32) * kn_h, axis=1, keepdims=True)
    sn = jnp.where(mn_ref[:, 0:1] > 0.5, sn, NEG_BIG)
    m = jnp.maximum(jnp.max(s, axis=1, keepdims=True), sn)
    p = jnp.exp2(s - m)
    pn = jnp.exp2(sn - m)
    l = jnp.sum(p, axis=1, keepdims=True) + pn
    vp = jnp.concatenate([r[...] for r in vpages], axis=0).astype(BF16)
    o = jnp.dot(p.astype(BF16), vp, preferred_element_type=F32)
    o_ref[...] = ((o + pn.astype(BF16).astype(F32) * vn_h) / l).astype(o_ref.dtype)


def _attn_sample(page_table, q3, mask4, mask_new, kn3, vn3, cache_k, cache_v):
    n, npg = page_table.shape
    prow = cache_k.shape[1]
    pspec = lambda p: pl.BlockSpec((None, prow, HEAD_DIM), lambda b, pt, p=p: (pt[b * npg + p], 0, 0))
    return pl.pallas_call(
        functools.partial(_attn_sample_kernel, npg=npg),
        out_shape=jax.ShapeDtypeStruct((n, N_HEADS, HEAD_DIM), BF16),
        grid_spec=pltpu.PrefetchScalarGridSpec(
            num_scalar_prefetch=1, grid=(n,),
            in_specs=[pl.BlockSpec((None, N_HEADS, HEAD_DIM), lambda b, pt: (b, 0, 0)),
                      pl.BlockSpec((None, 1, npg * prow), lambda b, pt: (b, 0, 0)),
                      pl.BlockSpec((None, 1, LANES), lambda b, pt: (b, 0, 0)),
                      pl.BlockSpec((None, N_KV, HEAD_DIM), lambda b, pt: (b, 0, 0)),
                      pl.BlockSpec((None, N_KV, HEAD_DIM), lambda b, pt: (b, 0, 0))]
            + [pspec(p) for p in range(npg)] + [pspec(p) for p in range(npg)],
            out_specs=pl.BlockSpec((None, N_HEADS, HEAD_DIM), lambda b, pt: (b, 0, 0))),
        compiler_params=_cparams(("parallel",)),
        name="attn_sample",
    )(page_table.reshape(-1), q3, mask4, mask_new, kn3, vn3, *([cache_k] * npg), *([cache_v] * npg))


MOE_RB = 256
MOE_GROUP = 2
MOE_RC = 1280
MOE_TF = 512
MOE_TN = 512


def _moe_sizes(n_tok):
    n_assign = n_tok * TOP_K
    n_slots = (n_assign // MOE_RB + N_EXPERTS) * MOE_RB
    n_units = N_EXPERTS + n_assign // MOE_RC
    return n_slots, n_units


def _moe_rank_kernel(sel_ref, rank_ref, cnt_ref, carry):
    i = pl.program_id(0)
    tp = sel_ref.shape[0]

    @pl.when(i == 0)
    def _():
        carry[...] = jnp.zeros_like(carry)

    a = sel_ref[...]
    r = lax.broadcasted_iota(I32, (tp, tp), 0)
    c = lax.broadcasted_iota(I32, (tp, tp), 1)
    lower = jnp.where(c < r, 1.0, 0.0).astype(BF16)
    rank_ref[...] = jnp.dot(lower, a.astype(BF16), preferred_element_type=F32) + carry[0:1, :]
    carry[...] = carry[...] + jnp.sum(a, axis=0, keepdims=True)
    cnt_ref[...] = carry[...]


def _moe_rank(sel):
    t = sel.shape[0]
    tp = _pick(t, (256, 128))
    return pl.pallas_call(
        _moe_rank_kernel,
        out_shape=[jax.ShapeDtypeStruct((t, LANES), F32), jax.ShapeDtypeStruct((8, LANES), F32)],
        grid=(t // tp,),
        in_specs=[pl.BlockSpec((tp, LANES), lambda i: (i, 0))],
        out_specs=[pl.BlockSpec((tp, LANES), lambda i: (i, 0)),
                   pl.BlockSpec((8, LANES), lambda i: (0, 0))],
        scratch_shapes=[pltpu.VMEM((8, LANES), F32)],
        compiler_params=_cparams(("arbitrary",)),
        name="moe_rank",
    )(sel)


def _moe_dest_kernel(sel_ref, gate_ref, rank_ref, cnt_ref, dest_ref, g4_ref, tab_ref):
    cnt = cnt_ref[...]
    lane8 = lax.broadcasted_iota(I32, cnt.shape, 1)
    padded = jnp.ceil(cnt * (1.0 / MOE_RB)) * MOE_RB
    incl = padded
    for d in (1, 2, 4, 8, 16, 32, 64):
        incl = incl + jnp.where(lane8 >= d, pltpu.roll(incl, d, 1), 0.0)
    start = incl - padded
    row8 = lax.broadcasted_iota(I32, cnt.shape, 0)
    tab_ref[...] = jnp.where(row8 == 0, start,
                             jnp.where(row8 == 1, padded, jnp.where(row8 == 2, cnt, 0.0))).astype(I32)
    dest = start[0:1, :] + rank_ref[...]
    gate = gate_ref[...]
    cur = sel_ref[...]
    lane = lax.broadcasted_iota(I32, cur.shape, 1)
    d4 = jnp.zeros(cur.shape, F32)
    g4 = jnp.zeros(cur.shape, F32)
    for k in range(TOP_K):
        first = jnp.min(jnp.where(cur > 0.5, lane, LANES), axis=1, keepdims=True)
        pick = lane == first
        dk = jnp.sum(jnp.where(pick, dest, 0.0), axis=1, keepdims=True)
        gk = jnp.sum(jnp.where(pick, gate, 0.0), axis=1, keepdims=True)
        d4 = jnp.where(lane == k, dk, d4)
        g4 = jnp.where(lane == k, gk, g4)
        cur = jnp.where(pick, 0.0, cur)
    dest_ref[...] = d4.astype(I32)
    g4_ref[...] = g4


def _moe_dest(sel, gate, rank, cnt):
    t = sel.shape[0]
    tp = _pick(t, (256, 128))
    row = lambda: pl.BlockSpec((tp, LANES), lambda i: (i, 0))
    one = lambda: pl.BlockSpec((8, LANES), lambda i: (0, 0))
    return pl.pallas_call(
        _moe_dest_kernel,
        out_shape=[jax.ShapeDtypeStruct((t, LANES), I32), jax.ShapeDtypeStruct((t, LANES), F32),
                   jax.ShapeDtypeStruct((8, LANES), I32)],
        grid=(t // tp,),
        in_specs=[row(), row(), row(), one()],
        out_specs=[row(), row(), one()],
        compiler_params=_cparams(("arbitrary",)),
        name="moe_dest",
    )(sel, gate, rank, cnt)


def _moe_units_kernel(tab_ref, unit_ref, *, n_units):
    rcb = MOE_RC // MOE_RB

    def per_expert(e, state):
        u0, _ = state
        nb = tab_ref[1, e] // MOE_RB
        b0 = tab_ref[0, e] // MOE_RB
        nu = (nb + rcb - 1) // rcb

        def per_unit(j, carry):
            unit_ref[0, u0 + j] = e
            unit_ref[1, u0 + j] = b0 + j * rcb
            unit_ref[2, u0 + j] = jnp.minimum(nb - j * rcb, rcb)
            return carry

        lax.fori_loop(0, nu, per_unit, 0)
        return u0 + nu, jnp.where(nu > 0, e, state[1])

    used, last = lax.fori_loop(0, N_EXPERTS, per_expert, (jnp.int32(0), jnp.int32(0)))

    def fill(u, carry):
        unit_ref[0, u] = last
        unit_ref[1, u] = 0
        unit_ref[2, u] = 0
        return carry

    lax.fori_loop(used, n_units, fill, 0)


def _moe_units(tab, n_units):
    return pl.pallas_call(
        functools.partial(_moe_units_kernel, n_units=n_units),
        out_shape=jax.ShapeDtypeStruct((3, n_units), I32),
        in_specs=[pl.BlockSpec(memory_space=pltpu.SMEM)],
        out_specs=pl.BlockSpec(memory_space=pltpu.SMEM),
        name="moe_units",
    )(tab)


def _moe_scatter_kernel(dest_ref, tab_ref, x_ref, xs_ref, zero_scr, sem, zsem):
    tm = x_ref.shape[0]

    @pl.when(pl.program_id(0) == 0)
    def _():
        zero_scr[...] = jnp.zeros_like(zero_scr)

        def row_copy(row):
            return pltpu.make_async_copy(zero_scr.at[pl.ds(0, 1)], xs_ref.at[pl.ds(row, 1)], zsem.at[0])

        def oct_copy(row):
            return pltpu.make_async_copy(zero_scr, xs_ref.at[pl.ds(pl.multiple_of(row, 8), 8)], zsem.at[1])

        def per_expert(e, totals):
            first = tab_ref[0, e] + tab_ref[2, e]
            end = tab_ref[0, e] + tab_ref[1, e]
            aligned = jnp.minimum((first + 7) // 8 * 8, end)
            n_oct = (end - aligned) // 8

            def one(r, carry):
                row_copy(first + r).start()
                return carry

            def octet(r, carry):
                oct_copy(aligned + 8 * r).start()
                return carry

            lax.fori_loop(0, aligned - first, one, 0)
            lax.fori_loop(0, n_oct, octet, 0)
            return totals[0] + aligned - first, totals[1] + n_oct

        n_row, n_oct = lax.fori_loop(0, N_EXPERTS, per_expert, (jnp.int32(0), jnp.int32(0)))

        def drain_rows(r, carry):
            row_copy(0).wait()
            return carry

        def drain_octs(r, carry):
            oct_copy(0).wait()
            return carry

        lax.fori_loop(0, n_row, drain_rows, 0)
        lax.fori_loop(0, n_oct, drain_octs, 0)

    def issue(r, carry):
        for k in range(TOP_K):
            pltpu.make_async_copy(x_ref.at[pl.ds(r, 1)], xs_ref.at[pl.ds(dest_ref[r * TOP_K + k], 1)],
                                  sem).start(priority=k % 2)
        return carry

    lax.fori_loop(0, tm, issue, 0)
    for _ in range(TOP_K):
        pltpu.make_async_copy(x_ref, xs_ref.at[pl.ds(0, tm)], sem).wait()


def _moe_scatter(dest_flat, tab, x1, n_slots):
    t, d = x1.shape
    tm = _pick(t, (256, 128))
    return pl.pallas_call(
        _moe_scatter_kernel,
        out_shape=jax.ShapeDtypeStruct((n_slots, d), F32),
        grid=(t // tm,),
        in_specs=[pl.BlockSpec((tm * TOP_K,), lambda i: (i,), memory_space=pltpu.SMEM),
                  pl.BlockSpec(memory_space=pltpu.SMEM),
                  pl.BlockSpec((tm, d), lambda i: (i, 0))],
        out_specs=pl.BlockSpec(memory_space=pl.ANY),
        scratch_shapes=[pltpu.VMEM((8, d), F32), pltpu.SemaphoreType.DMA(()),
                        pltpu.SemaphoreType.DMA((2,))],
        compiler_params=_cparams(("arbitrary",)),
        name="moe_scatter",
    )(dest_flat, tab, x1)


def _moe_expert_kernel(unit_ref, xs_ref, wg_ref, wl_ref, bg_ref, bl_ref, wd_ref, bd_ref, ys_ref,
                       x_scr, act_scr, xst, yst1, yst2, ystg, wg_bf, wl_bf, wd_bf,
                       xsem, ysem1, ysem2, ysemg, *, ju):
    u = pl.program_id(0)
    j = pl.program_id(1)
    b0 = unit_ref[1, u]
    ns = unit_ref[2, u]
    rb = MOE_RB

    def x_copy(s, slot):
        return pltpu.make_async_copy(xs_ref.at[pl.ds((b0 + s) * rb, rb)], xst.at[slot], xsem.at[slot])

    def rows_of(s, n=1):
        return pl.ds(s * rb, n * rb) if isinstance(s, int) else pl.ds(pl.multiple_of(s * rb, rb), n * rb)

    def up_block(s, n=1):
        xb = x_scr[rows_of(s, n), :]
        hg = jnp.dot(xb, wg_bf[...], preferred_element_type=F32) + bg_ref[...]
        hl = jnp.dot(xb, wl_bf[...], preferred_element_type=F32) + bl_ref[...]
        glu = jnp.minimum(hg, SWIGLU_LIMIT)
        lin = jnp.clip(hl, -SWIGLU_LIMIT, SWIGLU_LIMIT)
        act = glu * _sigmoid(SWIGLU_ALPHA * glu) * (lin + 1.0)
        act_scr[rows_of(s, n), pl.ds(pl.multiple_of(j * MOE_TF, MOE_TF), MOE_TF)] = act.astype(BF16)

    n_grp = (ns - 1) // MOE_GROUP
    rem = (ns - 1) % MOE_GROUP
    has2 = rem >= 2
    has1 = rem % 2 == 1
    tail2_at = 1 + MOE_GROUP * n_grp
    tail1_at = tail2_at + jnp.where(has2, 2, 0)

    def cast_up_weights():
        wg_bf[...] = wg_ref[...].astype(BF16)
        wl_bf[...] = wl_ref[...].astype(BF16)

    @pl.when(jnp.logical_and(ns > 0, j == 0))
    def _():
        x_copy(0, 0).start()
        cast_up_weights()

        @pl.when(ns > 1)
        def _():
            x_copy(1, 1).start()

        x_copy(0, 0).wait()
        x_scr[rows_of(0), :] = xst[0].astype(BF16)
        up_block(0)

        def body(s, carry):
            slot = s % 2

            @pl.when(s + 1 < ns)
            def _():
                x_copy(s + 1, 1 - slot).start()

            x_copy(s, slot).wait()
            x_scr[rows_of(s), :] = xst[slot].astype(BF16)
            up_block(s)
            return carry

        lax.fori_loop(1, ns, body, 0)

    @pl.when(jnp.logical_and(ns > 0, jnp.logical_and(j > 0, j < ju)))
    def _():
        cast_up_weights()
        up_block(0)

        def body(p, carry):
            up_block(1 + MOE_GROUP * p, MOE_GROUP)
            return carry

        lax.fori_loop(0, n_grp, body, 0)

        @pl.when(has2)
        def _():
            up_block(tail2_at, 2)

        @pl.when(has1)
        def _():
            up_block(tail1_at)

    @pl.when(jnp.logical_and(ns > 0, j >= ju))
    def _():
        col = pl.multiple_of((j - ju) * MOE_TN, MOE_TN)

        def y_copy(stage, sem, s, n):
            return pltpu.make_async_copy(
                stage, ys_ref.at[pl.ds((b0 + s) * rb, n * rb), pl.ds(col, MOE_TN)], sem)

        def yg_copy(p, slot):
            return y_copy(ystg.at[slot], ysemg.at[slot], 1 + MOE_GROUP * p, MOE_GROUP)

        def down(s, n=1):
            return jnp.dot(act_scr[rows_of(s, n), :], wd_bf[...], preferred_element_type=F32) + bd_ref[...]

        wd_bf[...] = wd_ref[...].astype(BF16)
        yst1[0] = down(0)
        y_copy(yst1.at[0], ysem1.at[0], 0, 1).start()

        def body(p, carry):
            slot = p % 2

            @pl.when(p >= 2)
            def _():
                yg_copy(p - 2, slot).wait()

            ystg[slot] = down(1 + MOE_GROUP * p, MOE_GROUP)
            yg_copy(p, slot).start()
            return carry

        lax.fori_loop(0, n_grp, body, 0)

        @pl.when(has2)
        def _():
            yst2[...] = down(tail2_at, 2)
            y_copy(yst2, ysem2.at[0], tail2_at, 2).start()

        @pl.when(has1)
        def _():
            yst1[1] = down(tail1_at)
            y_copy(yst1.at[1], ysem1.at[1], tail1_at, 1).start()

        @pl.when(n_grp >= 2)
        def _():
            yg_copy(n_grp - 2, n_grp % 2).wait()

        @pl.when(n_grp >= 1)
        def _():
            yg_copy(n_grp - 1, (n_grp - 1) % 2).wait()

        y_copy(yst1.at[0], ysem1.at[0], 0, 1).wait()

        @pl.when(has2)
        def _():
            y_copy(yst2, ysem2.at[0], tail2_at, 2).wait()

        @pl.when(has1)
        def _():
            y_copy(yst1.at[1], ysem1.at[1], tail1_at, 1).wait()


def _moe_experts(units, xs, w_up, b_up, w_down, b_down):
    n_slots, d = xs.shape
    n_units = units.shape[1]
    dff = w_down.shape[1]
    ju, jd = dff // MOE_TF, d // MOE_TN

    def up_idx(off):
        return lambda u, j, un: (un[0, u], 0, off + jnp.where(un[2, u] > 0, jnp.minimum(j, ju - 1), ju - 1))

    def dn_idx(u, j, un):
        return (un[0, u], 0, jnp.where(un[2, u] > 0, jnp.maximum(j - ju, 0), jd - 1))

    return pl.pallas_call(
        functools.partial(_moe_expert_kernel, ju=ju),
        out_shape=jax.ShapeDtypeStruct((n_slots, d), F32),
        grid_spec=pltpu.PrefetchScalarGridSpec(
            num_scalar_prefetch=1, grid=(n_units, ju + jd),
            in_specs=[pl.BlockSpec(memory_space=pl.ANY),
                      pl.BlockSpec((None, d, MOE_TF), up_idx(0)),
                      pl.BlockSpec((None, d, MOE_TF), up_idx(ju)),
                      pl.BlockSpec((None, 1, MOE_TF), up_idx(0)),
                      pl.BlockSpec((None, 1, MOE_TF), up_idx(ju)),
                      pl.BlockSpec((None, dff, MOE_TN), dn_idx),
                      pl.BlockSpec((None, 1, MOE_TN), dn_idx)],
            out_specs=pl.BlockSpec(memory_space=pl.ANY),
            scratch_shapes=[pltpu.VMEM((MOE_RC, d), BF16), pltpu.VMEM((MOE_RC, dff), BF16),
                            pltpu.VMEM((2, MOE_RB, d), F32), pltpu.VMEM((2, MOE_RB, MOE_TN), F32),
                            pltpu.VMEM((2 * MOE_RB, MOE_TN), F32),
                            pltpu.VMEM((2, MOE_GROUP * MOE_RB, MOE_TN), F32),
                            pltpu.VMEM((d, MOE_TF), BF16), pltpu.VMEM((d, MOE_TF), BF16),
                            pltpu.VMEM((dff, MOE_TN), BF16),
                            pltpu.SemaphoreType.DMA((2,)), pltpu.SemaphoreType.DMA((2,)),
                            pltpu.SemaphoreType.DMA((1,)), pltpu.SemaphoreType.DMA((2,))]),
        compiler_params=_cparams(("arbitrary", "arbitrary")),
        name="moe_experts",
    )(units, xs, w_up, w_up, b_up, b_up, w_down, b_down)


COMBINE_SUB_ROWS = 128


def _moe_combine_kernel(dest_ref, g4_ref, x1_ref, ys_ref, p_ref, wg_ref, wp_ref, g_ref, b_ref,
                        y_ref, gbuf, sem, *, alpha):
    tm = x1_ref.shape[0]

    def issue(r, carry):
        for k in range(TOP_K):
            pltpu.make_async_copy(ys_ref.at[pl.ds(dest_ref[r * TOP_K + k], 1)],
                                  gbuf.at[k, pl.ds(r, 1)], sem).start(priority=k % 2)
        return carry

    lax.fori_loop(0, tm, issue, 0)
    for k in range(TOP_K):
        pltpu.make_async_copy(ys_ref.at[pl.ds(0, tm)], gbuf.at[k], sem).wait()
    sub = min(tm, COMBINE_SUB_ROWS)
    for r in range(0, tm, sub):
        rows = slice(r, r + sub)
        g4 = g4_ref[rows, :]
        moe = g4[:, 0:1] * gbuf[0, rows, :]
        for k in range(1, TOP_K):
            moe = moe + g4[:, k:k + 1] * gbuf[k, rows, :]
        x2 = _layer_norm(alpha * x1_ref[rows, :] + moe, g_ref[...], b_ref[...])
        gate = _sigmoid(jnp.dot(x2.astype(BF16), wg_ref[...], preferred_element_type=F32))
        y_ref[rows, :] = x2 + gate * jnp.dot(p_ref[rows, :].astype(BF16), wp_ref[...],
                                             preferred_element_type=F32)


def _moe_combine(dest_flat, g4, x1, ys, p, w_gate, w_proj, g, b, alpha, row0, t):
    d = x1.shape[1]
    tm = _pick(t, (2 * COMBINE_SUB_ROWS, COMBINE_SUB_ROWS))
    assert row0 % tm == 0
    r0 = row0 // tm
    pd = p.shape[1]
    return pl.pallas_call(
        functools.partial(_moe_combine_kernel, alpha=alpha),
        out_shape=jax.ShapeDtypeStruct((t, d), F32),
        grid=(t // tm,),
        in_specs=[pl.BlockSpec((tm * TOP_K,), lambda i: (r0 + i,), memory_space=pltpu.SMEM),
                  pl.BlockSpec((tm, LANES), lambda i: (r0 + i, 0)),
                  pl.BlockSpec((tm, d), lambda i: (r0 + i, 0)),
                  pl.BlockSpec(memory_space=pl.ANY),
                  pl.BlockSpec((tm, pd), lambda i: (i, 0)),
                  _const_spec((d, d)), _const_spec((pd, d)), _const_spec((1, d)), _const_spec((1, d))],
        out_specs=pl.BlockSpec((tm, d), lambda i: (i, 0)),
        scratch_shapes=[pltpu.VMEM((TOP_K, tm, d), F32), pltpu.SemaphoreType.DMA(())],
        compiler_params=_cparams(("arbitrary",)),
        name="moe_combine",
    )(dest_flat, g4, x1, ys, p, w_gate, w_proj, g, b)


def _project_all(x, w, pos, prompt):
    t128, t64, tkw = _rope_tables(pos)
    xb = x.astype(BF16)
    dq = N_HEADS * HEAD_DIM
    dkv = N_KV * HEAD_DIM
    dqi = IDX_HEADS * IDX_DIM
    d = x.shape[1]
    (q,) = _proj(xb, w, 0, dq, t128, "rope128", ((BF16, None),), scale=LOG2_E * HEAD_DIM ** -0.5)
    k32, kb = _proj(xb, w, dq, dkv, t128, "rope128",
                    ((F32, None), (BF16, "transposed" if prompt else None)))
    v32, vb = _proj(xb, w, dq + dkv, dkv, (), "plain",
                    ((F32, None), (BF16, "with_ones" if prompt else None)))
    (qi,) = _proj(xb, w, dq + 2 * dkv, dqi, t64, "rope64", ((BF16, None),))
    c0 = dq + 2 * dkv + dqi
    (kw,) = _proj(xb, w, c0, LANES, tkw, "rope64", ((F32, None),))
    (zr,) = _proj(xb, w, c0, 4 * d, (), "plain", ((F32, None),), shift=IDX_DIM + IDX_HEADS)
    return q, k32, kb, v32, vb, qi, zr, kw


def kernel(x_prompt, x_sample, cache_k, cache_v, cache_kidx, state_h, state_conv, page_table, p_prompt, p_sample, w_in, conv_w, conv_b, rg_wa, rg_ba, rg_wx, rg_bx, rg_lambda, w_o_rnn, w_o_att, w_out, ln1_g, ln1_b, w_router, b_router, w_up, b_up, w_down, b_down, ln2_g, ln2_b, w_ple_gate, w_ple_proj):
    bsz, s, d = x_prompt.shape
    n, dec_t, _ = x_sample.shape
    assert bsz == 1 and dec_t == 1 and s % LANES == 0 and n % LANES == 0
    depth = w_in.shape[0]
    npg = page_table.shape[1]
    n_phys, page = cache_k.shape[1], cache_k.shape[2]
    past = npg * page
    alpha = (2 * depth) ** 0.25
    dq, dkv, dqi = N_HEADS * HEAD_DIM, N_KV * HEAD_DIM, IDX_HEADS * IDX_DIM
    t_all = s + n
    n_slots, n_units = _moe_sizes(t_all)
    pos_p = jnp.arange(s, dtype=I32)
    pos_s = jnp.full((n,), past, I32)
    row = lambda a: a[None, :]

    hp, hs = x_prompt[0], x_sample[:, 0]
    st_p = [[], [], [], [], []]
    st_s = [[], [], [], [], []]
    for i in range(depth):
        wa, wx = rg_wa[i].astype(BF16), rg_wx[i].astype(BF16)
        rnn_w = (conv_w[i], row(conv_b[i]), wa, wx, row(rg_ba[i]), row(rg_bx[i]), row(rg_lambda[i]))
        wor, woa, wo = w_o_rnn[i].astype(BF16), w_o_att[i].astype(BF16), w_out[i].astype(BF16)
        wr = jnp.pad(w_router[i], ((0, 0), (0, LANES - N_EXPERTS)))
        wr_hi = wr.astype(BF16)
        wr = jnp.concatenate([wr_hi, (wr - wr_hi.astype(F32)).astype(BF16)], axis=1)
        br = jnp.pad(b_router[i], (0, LANES - N_EXPERTS))[None, :]
        ln1 = (row(ln1_g[i]), row(ln1_b[i]))

        w_in_t = jnp.swapaxes(w_in[i], 0, 1)
        q, k32, kt, v32, vx, qi, zr, kw = _project_all(hp, w_in_t, pos_p, True)
        y_rnn, h_p = _rnn_prompt(zr, s, *rnn_w)
        kit = kw[:, :IDX_DIM].T.astype(BF16)
        y_att = _attn_prompt(qi, kw, q, jnp.concatenate([kit, kit], 0), kt, vx, s,
                             min(TOPK_MAX, s // 4))
        m = _merge(y_rnn, y_att, wor, woa, zr)
        bufs = _ln1_router(m, hp, wo, *ln1, wr, br, alpha, 0, t_all)
        st = (k32.reshape(1, s, N_KV, HEAD_DIM), v32.reshape(1, s, N_KV, HEAD_DIM),
              kw[None, :, :IDX_DIM], h_p, zr[None, s - (CONV_W - 1):s, :d])
        for lst, a in zip(st_p, st):
            lst.append(a)

        q, k32, kb, v32, vb, qi, zr, kw = _project_all(hs, w_in_t, pos_s, False)
        y_rnn, h_s = _rnn_sample(zr, state_conv[i], state_h[i], *rnn_w)
        sc = _idx_sample(page_table, qi.reshape(n, IDX_HEADS, IDX_DIM),
                         kw[:, IDX_DIM:IDX_DIM + IDX_HEADS, None], kw[:, None, :],
                         jnp.swapaxes(cache_kidx[i], 1, 2))
        mask = _mask_sample(sc[:, 0], past + 1, min(TOPK_MAX, (past + 1) // 4))
        y_att = _attn_sample(page_table, q.reshape(n, N_HEADS, HEAD_DIM),
                             jnp.repeat(mask[:, :past], N_KV, axis=1)[:, None, :],
                             mask[:, None, past:past + LANES],
                             kb.reshape(n, N_KV, HEAD_DIM), vb.reshape(n, N_KV, HEAD_DIM),
                             cache_k[i].reshape(n_phys, page * N_KV, HEAD_DIM),
                             cache_v[i].reshape(n_phys, page * N_KV, HEAD_DIM)).reshape(n, dq)
        m = _merge(y_rnn, y_att, wor, woa, zr)
        x1, sel, gate = _ln1_router(m, hs, wo, *ln1, wr, br, alpha, s, t_all, prev=bufs)
        st = (k32.reshape(n, 1, N_KV, HEAD_DIM), v32.reshape(n, 1, N_KV, HEAD_DIM),
              kw[:, None, :IDX_DIM], h_s,
              jnp.concatenate([state_conv[i][:, 1:], zr[:, None, :d]], axis=1))
        for lst, a in zip(st_s, st):
            lst.append(a)

        rank, cnt = _moe_rank(sel)
        dest4, g4, tab = _moe_dest(sel, gate, rank, cnt)
        units = _moe_units(tab, n_units)
        dest_flat = dest4[:, :TOP_K].reshape(-1)
        xs = _moe_scatter(dest_flat, tab, x1, n_slots)
        ys = _moe_experts(units, xs, w_up[i], b_up[i][:, None, :], w_down[i], b_down[i][:, None, :])
        tail = (w_ple_gate[i].astype(BF16), w_ple_proj[i].astype(BF16), row(ln2_g[i]), row(ln2_b[i]),
                alpha)
        hp = _moe_combine(dest_flat, g4, x1, ys, p_prompt[i][0], *tail, 0, s)
        hs = _moe_combine(dest_flat, g4, x1, ys, p_sample[i][:, 0], *tail, s, n)

    outs_p = [jnp.stack(l) for l in st_p]
    outs_s = [jnp.stack(l) for l in st_s]
    return (hp[None], hs[:, None], *outs_p, *outs_s)
```
